```python
import math
import jax
import jax.numpy as jnp
from jax import lax
import numpy as np

D_MODEL = 1024
BATCH = 32
SEQ = 2048
DEPTH = 1
DEC_BATCH = 4
DEC_SEQ = 8192
PAST_LEN = 128

HEAD_DIM = 64
A_HEADS = 8
A_KV_HEADS = 2
A_GROUP = A_HEADS // A_KV_HEADS
B_HEADS = 8
B_KV_HEADS = 2
B_GROUP = B_HEADS // B_KV_HEADS
A_WIDTH = A_HEADS * HEAD_DIM
B_WIDTH = B_HEADS * HEAD_DIM
KV_A_WIDTH = A_KV_HEADS * HEAD_DIM
KV_B_WIDTH = B_KV_HEADS * HEAD_DIM
Q_BLOCK = 128
WINDOW = 128
GRID_W = 64
ROPE_THETA = 10000.0
AXIS_DIM = HEAD_DIM // 2
N_BUCKETS = 32
MAX_DISTANCE = 128
N_EXPERTS = 16
CAPACITY_FACTOR = 2
D_FF = 2048
N_MOD = 6
EPS = 1e-6
NEG_INF = -1e30

SPLIT_POINTS = (
    A_WIDTH,
    A_WIDTH + KV_A_WIDTH,
    A_WIDTH + 2 * KV_A_WIDTH,
    A_WIDTH + 2 * KV_A_WIDTH + B_WIDTH,
    A_WIDTH + 2 * KV_A_WIDTH + B_WIDTH + KV_B_WIDTH,
    A_WIDTH + 2 * KV_A_WIDTH + B_WIDTH + 2 * KV_B_WIDTH,
    A_WIDTH + 2 * KV_A_WIDTH + B_WIDTH + 2 * KV_B_WIDTH + D_MODEL,
)
IN_COLS = A_WIDTH + 2 * KV_A_WIDTH + B_WIDTH + 2 * KV_B_WIDTH + 2 * D_MODEL

kernel_name = "hybrid_bidir_encoder_ec_moe"


def rms_norm(x, g):
    xf = x.astype(jnp.float32)
    y = xf * lax.rsqrt(jnp.mean(xf * xf, axis=-1, keepdims=True) + EPS)
    return (y * g.astype(jnp.float32)).astype(x.dtype)


def axial_angles(n_tokens):
    rows = n_tokens // GRID_W
    row_pos = jnp.repeat(jnp.arange(rows, dtype=jnp.float32), GRID_W)
    col_pos = jnp.tile(jnp.arange(GRID_W, dtype=jnp.float32), rows)
    inv_freq = ROPE_THETA ** (-jnp.arange(0, AXIS_DIM, 2, dtype=jnp.float32) / AXIS_DIM)
    return row_pos[:, None] * inv_freq, col_pos[:, None] * inv_freq


def rotate_axis(x, ang):
    xf = x.astype(jnp.float32)
    x1, x2 = jnp.split(xf, 2, axis=-1)
    cos = jnp.cos(ang)[None, :, None, :]
    sin = jnp.sin(ang)[None, :, None, :]
    return jnp.concatenate([x1 * cos - x2 * sin, x2 * cos + x1 * sin], axis=-1).astype(x.dtype)


def axial_rope(x, ang_r, ang_c):
    return jnp.concatenate([rotate_axis(x[..., :AXIS_DIM], ang_r),
                            rotate_axis(x[..., AXIS_DIM:], ang_c)], axis=-1)


def t5_bucket(rel):
    half = N_BUCKETS // 2
    max_exact = half // 2
    base = (rel > 0).astype(jnp.int32) * half
    n = jnp.abs(rel)
    large = max_exact + (jnp.log(jnp.maximum(n, 1).astype(jnp.float32) / max_exact)
                         / math.log(MAX_DISTANCE / max_exact) * (half - max_exact)).astype(jnp.int32)
    large = jnp.minimum(large, half - 1)
    return base + jnp.where(n < max_exact, n, large)


def global_attention(q, k, v):
    b, s = q.shape[0], q.shape[1]
    nb = s // Q_BLOCK
    qb = q.reshape(b, nb, Q_BLOCK, A_KV_HEADS, A_GROUP, HEAD_DIM).transpose(1, 0, 2, 3, 4, 5)

    def one_block(qi):
        logits = jnp.einsum('bqkgd,bskd->bkgqs', qi, k, preferred_element_type=jnp.float32)
        p = jax.nn.softmax(logits, axis=-1)
        return jnp.einsum('bkgqs,bskd->bqkgd', p.astype(v.dtype), v)

    o = lax.map(one_block, qb)
    return o.transpose(1, 0, 2, 3, 4, 5).reshape(b, s, A_WIDTH)


def window_attention(q, k, v, rel_bias, sink):
    b, s = q.shape[0], q.shape[1]
    nb = s // Q_BLOCK
    nside = WINDOW // Q_BLOCK
    span = Q_BLOCK + 2 * WINDOW
    pad = ((0, 0), (WINDOW, WINDOW), (0, 0), (0, 0))
    kblk = jnp.pad(k, pad).reshape(b, nb + 2 * nside, Q_BLOCK, B_KV_HEADS, HEAD_DIM)
    vblk = jnp.pad(v, pad).reshape(b, nb + 2 * nside, Q_BLOCK, B_KV_HEADS, HEAD_DIM)
    kw = jnp.concatenate([kblk[:, j:j + nb] for j in range(2 * nside + 1)], axis=2)
    vw = jnp.concatenate([vblk[:, j:j + nb] for j in range(2 * nside + 1)], axis=2)
    qb = q.reshape(b, nb, Q_BLOCK, B_KV_HEADS, B_GROUP, HEAD_DIM)
    logits = jnp.einsum('bnqkgd,bnskd->bnkgqs', qb, kw, preferred_element_type=jnp.float32)

    qpos = jnp.arange(Q_BLOCK)
    spos = jnp.arange(span)
    rel = spos[None, :] - WINDOW - qpos[:, None]
    band = jnp.abs(rel) <= WINDOW
    kabs = jnp.arange(nb)[:, None] * Q_BLOCK - WINDOW + spos[None, :]
    inside = (kabs >= 0) & (kabs < s)
    mask = band[None, :, :] & inside[:, None, :]
    bias = rel_bias.astype(jnp.float32)[t5_bucket(rel)]
    bias = bias.transpose(2, 0, 1).reshape(B_KV_HEADS, B_GROUP, Q_BLOCK, span)
    logits = jnp.where(mask[None, :, None, None], logits + bias, NEG_INF)

    sink_l = sink.astype(jnp.float32).reshape(B_KV_HEADS, B_GROUP)[:, :, None, None]
    m = jnp.maximum(jnp.max(logits, axis=-1, keepdims=True), sink_l)
    p = jnp.exp(logits - m)
    p = p / (jnp.sum(p, axis=-1, keepdims=True) + jnp.exp(sink_l - m))
    o = jnp.einsum('bnkgqs,bnskd->bnqkgd', p.astype(v.dtype), vw)
    return o.reshape(b, s, B_WIDTH)


def expert_choice_moe(h, w_router, w_e_gate, w_e_up, w_e_down):
    b, s, d = h.shape
    n = b * s
    cap = CAPACITY_FACTOR * n // N_EXPERTS
    t = h.reshape(n, d)
    aff = jax.nn.softmax(jnp.einsum('nd,de->ne', t, w_router, preferred_element_type=jnp.float32), axis=-1)
    g, idx = lax.top_k(aff.T, cap)
    xe = t[idx]
    hid = jax.nn.silu(jnp.einsum('ecd,edf->ecf', xe, w_e_gate)) * jnp.einsum('ecd,edf->ecf', xe, w_e_up)
    ye = jnp.einsum('ecf,efd->ecd', hid, w_e_down) * g[..., None].astype(h.dtype)
    out = jnp.zeros((n, d), ye.dtype).at[idx.reshape(-1)].add(ye.reshape(-1, d))
    return out.reshape(b, s, d)


def encoder_layer(x, c, ang_r, ang_c, w_ada, b_ada, norm1_g, w_in, q_norm_g, k_norm_g, sink,
                  w_branch_a, w_branch_b, w_out, norm2_g, w_router, w_e_gate, w_e_up, w_e_down, rel_bias):
    b, s = x.shape[0], x.shape[1]
    mod = (jax.nn.silu(c) @ w_ada + b_ada)[:, None, :]
    sh1, sc1, gt1, sh2, sc2, gt2 = jnp.split(mod, N_MOD, axis=-1)

    h = rms_norm(x, norm1_g) * (1 + sc1) + sh1
    qa, ka, va, qb, kb, vb, ga, gb = jnp.split(h @ w_in, SPLIT_POINTS, axis=-1)

    qa = axial_rope(rms_norm(qa.reshape(b, s, A_HEADS, HEAD_DIM), q_norm_g), ang_r, ang_c) * (HEAD_DIM ** -0.5)
    ka = axial_rope(rms_norm(ka.reshape(b, s, A_KV_HEADS, HEAD_DIM), k_norm_g), ang_r, ang_c)
    oa = global_attention(qa, ka, va.reshape(b, s, A_KV_HEADS, HEAD_DIM)) @ w_branch_a

    ob = window_attention(qb.reshape(b, s, B_HEADS, HEAD_DIM) * (HEAD_DIM ** -0.5),
                          kb.reshape(b, s, B_KV_HEADS, HEAD_DIM),
                          vb.reshape(b, s, B_KV_HEADS, HEAD_DIM), rel_bias, sink) @ w_branch_b

    merged = jax.nn.sigmoid(ga) * oa + jax.nn.sigmoid(gb) * ob
    x = x + gt1 * (merged @ w_out)

    h2 = rms_norm(x, norm2_g) * (1 + sc2) + sh2
    x = x + gt2 * expert_choice_moe(h2, w_router, w_e_gate, w_e_up, w_e_down)
    return x


def setup_inputs(seed: int = 0) -> dict:
    key = jax.random.key(seed)
    ks = jax.random.split(key, 24)

    def nrm(k, shape, scale):
        return jax.random.normal(k, shape, jnp.float32) * scale

    def gain(k, shape):
        return 1.0 + 0.02 * jax.random.normal(k, shape, jnp.float32)

    return {
        "x_prompt": nrm(ks[0], (BATCH, SEQ, D_MODEL), 1.0),
        "x_sample": nrm(ks[1], (DEC_BATCH, DEC_SEQ, D_MODEL), 1.0),
        "c_prompt": nrm(ks[2], (BATCH, D_MODEL), 1.0),
        "c_sample": nrm(ks[3], (DEC_BATCH, D_MODEL), 1.0),
        "w_ada": nrm(ks[4], (DEPTH, D_MODEL, N_MOD * D_MODEL), 0.5 * D_MODEL ** -0.5),
        "b_ada": nrm(ks[5], (DEPTH, N_MOD * D_MODEL), 0.01),
        "norm1_g": gain(ks[6], (DEPTH, D_MODEL)),
        "w_in": nrm(ks[7], (DEPTH, D_MODEL, IN_COLS), D_MODEL ** -0.5),
        "q_norm_g": gain(ks[8], (DEPTH, HEAD_DIM)),
        "k_norm_g": gain(ks[9], (DEPTH, HEAD_DIM)),
        "sink": nrm(ks[10], (DEPTH, B_HEADS), 0.5),
        "w_branch_a": nrm(ks[11], (DEPTH, A_WIDTH, D_MODEL), A_WIDTH ** -0.5),
        "w_branch_b": nrm(ks[12], (DEPTH, B_WIDTH, D_MODEL), B_WIDTH ** -0.5),
        "w_out": nrm(ks[13], (DEPTH, D_MODEL, D_MODEL), D_MODEL ** -0.5),
        "norm2_g": gain(ks[14], (DEPTH, D_MODEL)),
        "w_router": nrm(ks[15], (DEPTH, D_MODEL, N_EXPERTS), D_MODEL ** -0.5),
        "w_e_gate": nrm(ks[16], (DEPTH, N_EXPERTS, D_MODEL, D_FF), D_MODEL ** -0.5),
        "w_e_up": nrm(ks[17], (DEPTH, N_EXPERTS, D_MODEL, D_FF), D_MODEL ** -0.5),
        "w_e_down": nrm(ks[18], (DEPTH, N_EXPERTS, D_FF, D_MODEL), D_FF ** -0.5),
        "rel_bias": nrm(ks[19], (N_BUCKETS, B_HEADS), 0.5),
        "norm_f_g": gain(ks[20], (D_MODEL,)),
    }


def reference(x_prompt, x_sample, c_prompt, c_sample, w_ada, b_ada, norm1_g, w_in, q_norm_g, k_norm_g,
              sink, w_branch_a, w_branch_b, w_out, norm2_g, w_router, w_e_gate, w_e_up, w_e_down,
              rel_bias, norm_f_g):
    def trunk(x, c):
        ang_r, ang_c = axial_angles(x.shape[1])
        for l in range(DEPTH):
            x = encoder_layer(x, c, ang_r, ang_c, w_ada[l], b_ada[l], norm1_g[l], w_in[l], q_norm_g[l],
                              k_norm_g[l], sink[l], w_branch_a[l], w_branch_b[l], w_out[l], norm2_g[l],
                              w_router[l], w_e_gate[l], w_e_up[l], w_e_down[l], rel_bias)
        return rms_norm(x, norm_f_g)

    y_prompt = trunk(x_prompt, c_prompt)
    y_sample = trunk(x_sample, c_sample)
    return (y_prompt, y_sample)
```

```python
import functools
import math

import jax
import jax.numpy as jnp
import numpy as np
from jax import lax
from jax.experimental import pallas as pl
from jax.experimental.pallas import tpu as pltpu

F32 = jnp.float32
BF16 = jnp.bfloat16
I32 = jnp.int32
U32 = jnp.uint32

D_MODEL = 1024
HEAD_DIM = 64
A_HEADS = 8
A_KV_HEADS = 2
B_HEADS = 8
B_KV_HEADS = 2
A_WIDTH = A_HEADS * HEAD_DIM
B_WIDTH = B_HEADS * HEAD_DIM
KV_WIDTH = A_KV_HEADS * HEAD_DIM
Q_BLOCK = 128
WINDOW = 128
GRID_W = 64
ROPE_THETA = 10000.0
AXIS_DIM = HEAD_DIM // 2
N_BUCKETS = 32
MAX_DISTANCE = 128
N_EXPERTS = 16
CAPACITY_FACTOR = 2
D_FF = 2048
N_MOD = 6
EPS = 1e-6
NEG_INF = -1e30
IN_COLS = A_WIDTH + 2 * KV_WIDTH + B_WIDTH + 2 * KV_WIDTH + 2 * D_MODEL

LANES = 128
PACKED = D_MODEL // 2
VMEM_LIMIT = 56 * 1024 * 1024

_QA, _KA, _VA = 0, A_WIDTH, A_WIDTH + KV_WIDTH
_QB = A_WIDTH + 2 * KV_WIDTH
_KB, _VB = _QB + B_WIDTH, _QB + B_WIDTH + KV_WIDTH
_GA = _QB + B_WIDTH + 2 * KV_WIDTH
_GB = _GA + D_MODEL


def _params(sem, vmem=VMEM_LIMIT):
    return pltpu.CompilerParams(dimension_semantics=sem, vmem_limit_bytes=vmem)


def _dot(a, b):
    return jnp.dot(a, b, preferred_element_type=F32)


def _mod_kernel(c_ref, w_ref, b_ref, o_ref):
    c = c_ref[...]
    s = c * jax.nn.sigmoid(c)
    o_ref[...] = _dot(s.astype(BF16), w_ref[...].astype(BF16)) + b_ref[...]


def _modulation(c, w_ada, b_ada):
    bp, d = c.shape
    n = w_ada.shape[1]
    tn = 1536
    return pl.pallas_call(
        _mod_kernel,
        grid=(n // tn,),
        in_specs=[pl.BlockSpec((bp, d), lambda j: (0, 0)),
                  pl.BlockSpec((d, tn), lambda j: (0, j)),
                  pl.BlockSpec((1, tn), lambda j: (0, j))],
        out_specs=pl.BlockSpec((bp, tn), lambda j: (0, j)),
        out_shape=jax.ShapeDtypeStruct((bp, n), F32),
        compiler_params=_params(("arbitrary",)),
        name="modulation",
    )(c, w_ada, b_ada.reshape(1, n))


def _swap16(x):
    n = x.shape[-1]
    left = pltpu.roll(x, n - 16, axis=1)
    right = pltpu.roll(x, 16, axis=1)
    lane = lax.broadcasted_iota(I32, x.shape, 1)
    return jnp.where((lane % 32) < 16, left, right)


def _inproj_kernel(x_ref, sc_ref, sh_ref, g1_ref, w_ref, aq_ref, bq_ref, ak_ref, bk_ref, hm_ref,
                   qa_ref, kat_ref, va_ref, qb_ref, kbt_ref, vb_ref, sga_ref, sgb_ref):
    x = x_ref[0]
    ms = jnp.mean(x * x, axis=-1, keepdims=True)
    xn = x * lax.rsqrt(ms + EPS) * g1_ref[...]
    h = (xn * (1.0 + sc_ref[0]) + sh_ref[0]).astype(BF16)

    def proj(lo, width):
        return _dot(h, w_ref[:, lo:lo + width])

    def head_rms(q):
        pieces = []
        for j in range(q.shape[1] // LANES):
            qq = q[:, j * LANES:(j + 1) * LANES]
            pieces.append(_dot((qq * qq).astype(BF16), hm_ref[...]))
        msq = pieces[0] if len(pieces) == 1 else jnp.concatenate(pieces, axis=1)
        return q * lax.rsqrt(msq + EPS)

    def rope(qh, a_ref, b_ref):
        reps = qh.shape[1] // LANES
        a = a_ref[...]
        b = b_ref[...]
        if reps > 1:
            a = jnp.concatenate([a] * reps, axis=1)
            b = jnp.concatenate([b] * reps, axis=1)
        return qh * a + _swap16(qh) * b

    qa = rope(head_rms(proj(_QA, A_WIDTH)), aq_ref, bq_ref)
    qa_ref[0] = qa.astype(BF16)
    ka = rope(head_rms(proj(_KA, KV_WIDTH)), ak_ref, bk_ref)
    kat_ref[0] = ka.T.astype(BF16)
    va_ref[0] = proj(_VA, KV_WIDTH).astype(BF16)
    qb_ref[0] = (proj(_QB, B_WIDTH) * (HEAD_DIM ** -0.5)).astype(BF16)
    kbt_ref[0] = proj(_KB, KV_WIDTH).T.astype(BF16)
    vb_ref[0] = proj(_VB, KV_WIDTH).astype(BF16)
    sga_ref[0] = jax.nn.sigmoid(proj(_GA, D_MODEL)).astype(BF16)
    sgb_ref[0] = jax.nn.sigmoid(proj(_GB, D_MODEL)).astype(BF16)


def _rope_tables(s, gain, scale):
    pos = jnp.arange(s, dtype=I32)
    row = (pos // GRID_W).astype(F32)
    col = (pos % GRID_W).astype(F32)
    inv_freq = ROPE_THETA ** (-jnp.arange(0, AXIS_DIM, 2, dtype=F32) / AXIS_DIM)
    ang_r = row[:, None] * inv_freq
    ang_c = col[:, None] * inv_freq
    cos64 = jnp.concatenate([jnp.cos(ang_r)] * 2 + [jnp.cos(ang_c)] * 2, axis=1)
    sin64 = jnp.concatenate([-jnp.sin(ang_r), jnp.sin(ang_r), -jnp.sin(ang_c), jnp.sin(ang_c)], axis=1)
    g = gain.astype(F32)
    gp = jnp.concatenate([g[16:32], g[0:16], g[48:64], g[32:48]])
    a = cos64 * g[None, :] * scale
    b = sin64 * gp[None, :] * scale
    return jnp.concatenate([a, a], axis=1), jnp.concatenate([b, b], axis=1)


def _in_projection(x, sc1, sh1, g1, w_in, q_gain, k_gain, tm):
    b, s, d = x.shape
    aq, bq = _rope_tables(s, q_gain, HEAD_DIM ** -0.5)
    ak, bk = _rope_tables(s, k_gain, 1.0)
    head_mean = jnp.asarray(np.kron(np.eye(2), np.full((HEAD_DIM, HEAD_DIM), 1.0 / HEAD_DIM)), BF16)
    tok = lambda w: pl.BlockSpec((1, tm, w), lambda i, t: (i, t, 0))
    tr = pl.BlockSpec((1, KV_WIDTH, tm), lambda i, t: (i, 0, t))
    vec = pl.BlockSpec((1, 1, d), lambda i, t: (i, 0, 0))
    tab = pl.BlockSpec((tm, LANES), lambda i, t: (t, 0))
    sd = jax.ShapeDtypeStruct
    return pl.pallas_call(
        _inproj_kernel,
        grid=(b, s // tm),
        in_specs=[tok(d), vec, vec,
                  pl.BlockSpec((1, d), lambda i, t: (0, 0)),
                  pl.BlockSpec((d, IN_COLS), lambda i, t: (0, 0)),
                  tab, tab, tab, tab,
                  pl.BlockSpec((LANES, LANES), lambda i, t: (0, 0))],
        out_specs=[tok(A_WIDTH), tr, tok(KV_WIDTH), tok(B_WIDTH), tr, tok(KV_WIDTH), tok(d), tok(d)],
        out_shape=[sd((b, s, A_WIDTH), BF16), sd((b, KV_WIDTH, s), BF16), sd((b, s, KV_WIDTH), BF16),
                   sd((b, s, B_WIDTH), BF16), sd((b, KV_WIDTH, s), BF16), sd((b, s, KV_WIDTH), BF16),
                   sd((b, s, d), BF16), sd((b, s, d), BF16)],
        compiler_params=_params(("arbitrary", "arbitrary")),
        name="in_projection",
    )(x, sc1, sh1, g1.reshape(1, d), w_in, aq, bq, ak, bk, head_mean)


def _half_lane_variants(v, k_is_one):
    lane = lax.broadcasted_iota(I32, v.shape, 1)
    vr = pltpu.roll(v, HEAD_DIM, axis=1)
    own_lo = jnp.where(k_is_one, vr, v)
    own_hi = jnp.where(k_is_one, v, vr)
    lo = jnp.where(lane < HEAD_DIM, own_lo, 0.0)
    hi = jnp.where(lane >= HEAD_DIM, own_hi, 0.0)
    return lo, hi


def _gattn_kernel(q_ref, kt_ref, v_ref, o_ref, vv_ref, s_ref, *, kc):
    k = pl.program_id(1)
    qi = pl.program_id(2)
    s_len = kt_ref.shape[2]
    nc = s_len // kc

    @pl.when(qi == 0)
    def _():
        lo, hi = _half_lane_variants(v_ref[0].astype(F32), k == 1)
        vv_ref[0] = lo.astype(BF16)
        vv_ref[1] = hi.astype(BF16)

    group = A_HEADS // A_KV_HEADS
    for j in range(group // 2):
        pair = None
        for half in range(2):
            g = 2 * j + half
            q = q_ref[0, :, g * HEAD_DIM:(g + 1) * HEAD_DIM]
            m = None
            for c in range(nc):
                sc = _dot(q, kt_ref[0, :, c * kc:(c + 1) * kc])
                s_ref[:, c * kc:(c + 1) * kc] = sc
                mc = jnp.max(sc, axis=-1, keepdims=True)
                m = mc if m is None else jnp.maximum(m, mc)
            l = None
            acc = None
            for c in range(nc):
                p = jnp.exp(s_ref[:, c * kc:(c + 1) * kc] - m)
                lc = jnp.sum(p, axis=-1, keepdims=True)
                ac = _dot(p.astype(BF16), vv_ref[half, c * kc:(c + 1) * kc, :])
                l = lc if l is None else l + lc
                acc = ac if acc is None else acc + ac
            o = acc / l
            pair = o if pair is None else pair + o
        o_ref[0, :, j * LANES:(j + 1) * LANES] = pair.astype(o_ref.dtype)


def _global_attention(qa, kat, va, tq, kc):
    b, s, _ = qa.shape
    gw = A_WIDTH // A_KV_HEADS
    kc = min(kc, s)
    return pl.pallas_call(
        functools.partial(_gattn_kernel, kc=kc),
        grid=(b, A_KV_HEADS, s // tq),
        in_specs=[pl.BlockSpec((1, tq, gw), lambda i, k, t: (i, t, k)),
                  pl.BlockSpec((1, HEAD_DIM, s), lambda i, k, t: (i, k, 0)),
                  pl.BlockSpec((1, s, KV_WIDTH), lambda i, k, t: (i, 0, 0))],
        out_specs=pl.BlockSpec((1, tq, gw), lambda i, k, t: (i, t, k)),
        out_shape=jax.ShapeDtypeStruct((b, s, A_WIDTH), BF16),
        scratch_shapes=[pltpu.VMEM((2, s, KV_WIDTH), BF16), pltpu.VMEM((tq, s), F32)],
        compiler_params=_params(("arbitrary", "arbitrary", "arbitrary")),
        name="global_attention",
    )(qa, kat, va)


def _t5_bucket_np(rel):
    half = N_BUCKETS // 2
    max_exact = half // 2
    base = (rel > 0).astype(np.int32) * half
    n = np.abs(rel)
    large = max_exact + (np.log(np.maximum(n, 1).astype(np.float32) / max_exact)
                         / math.log(MAX_DISTANCE / max_exact) * (half - max_exact)).astype(np.int32)
    large = np.minimum(large, half - 1)
    return base + np.where(n < max_exact, n, large)


def _wattn_kernel(q_ref, ktp_ref, ktc_ref, ktn_ref, vp_ref, vc_ref, vn_ref, bias_ref, sink_ref, o_ref):
    i = pl.program_id(1)
    nb = pl.num_programs(1)
    span = Q_BLOCK + 2 * WINDOW
    kt = jnp.concatenate([ktp_ref[0], ktc_ref[0], ktn_ref[0]], axis=1)
    v = jnp.concatenate([vp_ref[0], vc_ref[0], vn_ref[0]], axis=0).astype(F32)
    col = lax.broadcasted_iota(I32, (1, span), 1)
    inside = ((col >= WINDOW) | (i > 0)) & ((col < WINDOW + Q_BLOCK) | (i < nb - 1))
    group = B_HEADS // B_KV_HEADS
    for k in range(B_KV_HEADS):
        lo, hi = _half_lane_variants(v, k == 1)
        vv = (lo.astype(BF16), hi.astype(BF16))
        ktk = kt[k * HEAD_DIM:(k + 1) * HEAD_DIM, :]
        for j in range(group // 2):
            pair = None
            for half in range(2):
                h = k * group + 2 * j + half
                q = q_ref[0, :, h * HEAD_DIM:(h + 1) * HEAD_DIM]
                logits = _dot(q, ktk) + bias_ref[h]
                logits = jnp.where(inside, logits, NEG_INF)
                sink = sink_ref[h:h + 1, 0:1]
                m = jnp.maximum(jnp.max(logits, axis=-1, keepdims=True), sink)
                p = jnp.exp(logits - m)
                den = jnp.sum(p, axis=-1, keepdims=True) + jnp.exp(sink - m)
                o = _dot(p.astype(BF16), vv[half]) / den
                pair = o if pair is None else pair + o
            pj = k * (group // 2) + j
            o_ref[0, :, pj * LANES:(pj + 1) * LANES] = pair.astype(o_ref.dtype)


def _window_attention(qb, kbt, vb, rel_bias, sink):
    b, s, _ = qb.shape
    nb = s // Q_BLOCK
    span = Q_BLOCK + 2 * WINDOW
    rel = np.arange(span)[None, :] - WINDOW - np.arange(Q_BLOCK)[:, None]
    band = np.abs(rel) <= WINDOW
    bias = rel_bias.astype(F32)[jnp.asarray(_t5_bucket_np(rel))]
    bias = jnp.where(jnp.asarray(band)[None], bias.transpose(2, 0, 1), NEG_INF)
    sink_b = jnp.broadcast_to(sink.astype(F32)[:, None], (B_HEADS, LANES))
    prev = lambda i, t: jnp.maximum(t - 1, 0)
    nxt = lambda i, t: jnp.minimum(t + 1, nb - 1)
    ktspec = lambda f: pl.BlockSpec((1, KV_WIDTH, Q_BLOCK), lambda i, t: (i, 0, f(i, t)))
    vspec = lambda f: pl.BlockSpec((1, Q_BLOCK, KV_WIDTH), lambda i, t: (i, f(i, t), 0))
    cur = lambda i, t: t
    return pl.pallas_call(
        _wattn_kernel,
        grid=(b, nb),
        in_specs=[pl.BlockSpec((1, Q_BLOCK, B_WIDTH), lambda i, t: (i, t, 0)),
                  ktspec(prev), ktspec(cur), ktspec(nxt),
                  vspec(prev), vspec(cur), vspec(nxt),
                  pl.BlockSpec((B_HEADS, Q_BLOCK, span), lambda i, t: (0, 0, 0)),
                  pl.BlockSpec((B_HEADS, LANES), lambda i, t: (0, 0))],
        out_specs=pl.BlockSpec((1, Q_BLOCK, B_WIDTH), lambda i, t: (i, t, 0)),
        out_shape=jax.ShapeDtypeStruct((b, s, B_WIDTH), BF16),
        compiler_params=_params(("arbitrary", "arbitrary")),
        name="window_attention",
    )(qb, kbt, kbt, kbt, vb, vb, vb, bias, sink_b)


def _pack_bf16_pairs(x):
    k = x.shape[1] // 2
    hi = pltpu.bitcast(x[:, :k].astype(BF16).astype(F32), U32)
    lo = pltpu.bitcast(x[:, k:].astype(BF16).astype(F32), U32)
    return hi | (lo >> 16)


def _unpack_bf16_pairs(w):
    hi = pltpu.bitcast(w & jnp.uint32(0xFFFF0000), F32)
    lo = pltpu.bitcast(w << 16, F32)
    return jnp.concatenate([hi, lo], axis=1).astype(BF16)


def _outproj_kernel(oa_ref, ob_ref, sga_ref, sgb_ref, x_ref, gt1_ref, sc2_ref, sh2_ref, g2_ref,
                    wa_ref, wb_ref, wo_ref, wr_ref, x1_ref, h2p_ref, afft_ref):
    a = _dot(oa_ref[0], wa_ref[...])
    b = _dot(ob_ref[0], wb_ref[...])
    merged = sga_ref[0].astype(F32) * a + sgb_ref[0].astype(F32) * b
    y = _dot(merged.astype(BF16), wo_ref[...])
    x1 = x_ref[0] + gt1_ref[0] * y
    x1_ref[0] = x1
    ms = jnp.mean(x1 * x1, axis=-1, keepdims=True)
    h2 = (x1 * lax.rsqrt(ms + EPS) * g2_ref[...]) * (1.0 + sc2_ref[0]) + sh2_ref[0]
    h2p_ref[...] = _pack_bf16_pairs(h2).reshape(h2p_ref.shape)
    logits = _dot(h2.astype(BF16), wr_ref[...])
    lane = lax.broadcasted_iota(I32, logits.shape, 1)
    logits = jnp.where(lane < N_EXPERTS, logits, -jnp.inf)
    m = jnp.max(logits, axis=-1, keepdims=True)
    e = jnp.exp(logits - m)
    aff = e / jnp.sum(e, axis=-1, keepdims=True)
    afft_ref[...] = aff.T[:N_EXPERTS, :]


def _out_projection(oa, ob, sga, sgb, x, gt1, sc2, sh2, g2, wa, wb, wo, wr, tm):
    b, s, d = x.shape
    nt = s // tm
    n = b * s
    tok = lambda w: pl.BlockSpec((1, tm, w), lambda i, t: (i, t, 0))
    vec = pl.BlockSpec((1, 1, d), lambda i, t: (i, 0, 0))
    full = lambda r, c: pl.BlockSpec((r, c), lambda i, t: (0, 0))
    sd = jax.ShapeDtypeStruct
    return pl.pallas_call(
        _outproj_kernel,
        grid=(b, nt),
        in_specs=[tok(A_WIDTH), tok(B_WIDTH), tok(d), tok(d), tok(d), vec, vec, vec, full(1, d),
                  full(A_WIDTH, d), full(B_WIDTH, d), full(d, d), full(d, LANES)],
        out_specs=[tok(d),
                   pl.BlockSpec((tm, 1, PACKED), lambda i, t: (i * nt + t, 0, 0)),
                   pl.BlockSpec((N_EXPERTS, tm), lambda i, t: (0, i * nt + t))],
        out_shape=[sd((b, s, d), F32), sd((n, 1, PACKED), U32), sd((N_EXPERTS, n), F32)],
        compiler_params=_params(("arbitrary", "arbitrary")),
        name="out_projection",
    )(oa, ob, sga, sgb, x, gt1, sc2, sh2, g2.reshape(1, d), wa, wb, wo, wr)


def _threshold_kernel(aff_ref, thr_ref, *, cap):
    bits = pltpu.bitcast(aff_ref[...], I32)

    def body(i, lo):
        cand = lo | jnp.left_shift(jnp.int32(1), 30 - i)
        cnt = jnp.sum((bits >= cand).astype(F32), axis=1, keepdims=True)
        return jnp.where(cnt >= cap, cand, lo)

    lo = lax.fori_loop(0, 31, body, jnp.zeros((N_EXPERTS, 1), I32))
    thr_ref[...] = jnp.broadcast_to(lo, thr_ref.shape)


def _capacity_threshold(afft, cap):
    e, n = afft.shape
    return pl.pallas_call(
        functools.partial(_threshold_kernel, cap=cap),
        grid=(1,),
        in_specs=[pl.BlockSpec((e, n), lambda i: (0, 0))],
        out_specs=pl.BlockSpec((e, LANES), lambda i: (0, 0)),
        out_shape=jax.ShapeDtypeStruct((e, LANES), I32),
        compiler_params=_params(("arbitrary",)),
        name="capacity_threshold",
    )(afft)


def _moe_kernel(meta_hbm, gate_ref, h2_hbm, wg_ref, wu_ref, wd_ref, z_hbm,
                meta_smem, xbuf, x2d, zbuf, sem_m, sem_g, sem_s, *, tm, nsteps):
    nt = pl.num_programs(1)
    s = pl.program_id(0) * nt + pl.program_id(1)
    slot = s % 2

    def meta_copy(step, mslot):
        return pltpu.make_async_copy(meta_hbm.at[step], meta_smem.at[mslot], sem_m)

    def issue_gather(mslot, bslot):
        for r in range(tm):
            tok = meta_smem[mslot, r]
            pltpu.make_async_copy(h2_hbm.at[tok], xbuf.at[bslot, r], sem_g.at[bslot]).start()

    def wait_gather(bslot):
        pltpu.make_async_copy(h2_hbm.at[pl.ds(0, tm)], xbuf.at[bslot], sem_g.at[bslot]).wait()

    def issue_scatter(mslot, bslot):
        for r in range(tm):
            dst = meta_smem[mslot, tm + r]
            pltpu.make_async_copy(zbuf.at[bslot, r], z_hbm.at[dst], sem_s.at[bslot]).start()

    def wait_scatter(bslot):
        pltpu.make_async_copy(zbuf.at[bslot], z_hbm.at[pl.ds(0, tm)], sem_s.at[bslot]).wait()

    @pl.when(s == 0)
    def _():
        c0 = meta_copy(0, 0)
        c0.start()
        c0.wait()
        if nsteps > 1:
            c1 = meta_copy(1, 1)
            c1.start()
            c1.wait()
        issue_gather(0, 0)

    @pl.when(s + 2 < nsteps)
    def _():
        meta_copy(s + 2, (s + 2) % 3).start()

    @pl.when(s + 1 < nsteps)
    def _():
        issue_gather((s + 1) % 3, 1 - slot)

    wait_gather(slot)
    x2d[...] = xbuf[slot].reshape(tm, PACKED)
    xe = _unpack_bf16_pairs(x2d[...])
    gate = _dot(xe, wg_ref[...])
    up = _dot(xe, wu_ref[...])
    hid = (gate * jax.nn.sigmoid(gate) * up).astype(BF16)
    ye = _dot(hid, wd_ref[...])
    z = ye * gate_ref[...].T[:, 0:1]

    @pl.when(s >= 2)
    def _():
        wait_scatter(slot)

    zbuf[slot] = _pack_bf16_pairs(z).reshape(tm, 1, PACKED)
    issue_scatter(s % 3, slot)

    @pl.when(s + 2 < nsteps)
    def _():
        meta_copy(s + 2, (s + 2) % 3).wait()

    @pl.when(s == nsteps - 1)
    def _():
        if nsteps > 1:
            wait_scatter(1 - slot)
        wait_scatter(slot)


def _expert_mlp(meta, gate8, h2p, wg, wu, wd, tm):
    nsteps = meta.shape[0]
    nt = nsteps // N_EXPERTS
    d, f = wg.shape[1], wg.shape[2]
    step = lambda e, i: (e * nt + i, 0, 0)
    return pl.pallas_call(
        functools.partial(_moe_kernel, tm=tm, nsteps=nsteps),
        grid=(N_EXPERTS, nt),
        in_specs=[pl.BlockSpec(memory_space=pl.ANY),
                  pl.BlockSpec((None, 8, tm), step),
                  pl.BlockSpec(memory_space=pl.ANY),
                  pl.BlockSpec((None, d, f), lambda e, i: (e, 0, 0)),
                  pl.BlockSpec((None, d, f), lambda e, i: (e, 0, 0)),
                  pl.BlockSpec((None, f, d), lambda e, i: (e, 0, 0))],
        out_specs=pl.BlockSpec(memory_space=pl.ANY),
        out_shape=jax.ShapeDtypeStruct((nsteps * tm, 1, PACKED), U32),
        scratch_shapes=[pltpu.SMEM((3, 2 * tm), I32),
                        pltpu.VMEM((2, tm, 1, PACKED), U32),
                        pltpu.VMEM((tm, PACKED), U32),
                        pltpu.VMEM((2, tm, 1, PACKED), U32),
                        pltpu.SemaphoreType.DMA,
                        pltpu.SemaphoreType.DMA((2,)),
                        pltpu.SemaphoreType.DMA((2,))],
        compiler_params=_params(("arbitrary", "arbitrary")),
        name="expert_mlp",
    )(meta, gate8, h2p, wg, wu, wd)


def _combine_kernel(wtile_ref, wblk_ref, wvalid_ref, z_ref, ptok_ref, x1_ref, gt2_ref, gf_ref,
                    y_ref, acc_ref, z2d, *, tt):
    w = pl.program_id(0)
    nw = pl.num_programs(0)
    tile = wtile_ref[w]
    first = jnp.logical_or(w == 0, wtile_ref[jnp.maximum(w - 1, 0)] != tile)
    last = jnp.logical_or(w == nw - 1, wtile_ref[jnp.minimum(w + 1, nw - 1)] != tile)

    @pl.when(first)
    def _():
        acc_ref[...] = jnp.zeros_like(acc_ref)

    @pl.when(wvalid_ref[w] == 1)
    def _():
        z2d[...] = z_ref[...].reshape(z2d.shape)
        zrows = _unpack_bf16_pairs(z2d[...])
        local = ptok_ref[0] - tile * tt
        row = lax.broadcasted_iota(I32, (tt, local.shape[1]), 0)
        onehot = jnp.where(row == local, 1.0, 0.0).astype(BF16)
        acc_ref[...] += _dot(onehot, zrows)

    @pl.when(last)
    def _():
        x2 = x1_ref[...] + gt2_ref[0] * acc_ref[...]
        ms = jnp.mean(x2 * x2, axis=-1, keepdims=True)
        y_ref[...] = x2 * lax.rsqrt(ms + EPS) * gf_ref[...]


def _combine(wtile, wblk, wvalid, z, ptok, x1, gt2, gf, s, tt, blk):
    n, d = x1.shape
    nw = wtile.shape[0]
    grid_spec = pltpu.PrefetchScalarGridSpec(
        num_scalar_prefetch=3,
        grid=(nw,),
        in_specs=[pl.BlockSpec((blk, 1, PACKED), lambda w, wt, wb, wv: (wb[w], 0, 0)),
                  pl.BlockSpec((1, 1, blk), lambda w, wt, wb, wv: (wb[w], 0, 0)),
                  pl.BlockSpec((tt, d), lambda w, wt, wb, wv: (wt[w], 0)),
                  pl.BlockSpec((1, 1, d), lambda w, wt, wb, wv: (wt[w] * tt // s, 0, 0)),
                  pl.BlockSpec((1, d), lambda w, wt, wb, wv: (0, 0))],
        out_specs=pl.BlockSpec((tt, d), lambda w, wt, wb, wv: (wt[w], 0)),
        scratch_shapes=[pltpu.VMEM((tt, d), F32), pltpu.VMEM((blk, PACKED), U32)],
    )
    return pl.pallas_call(
        functools.partial(_combine_kernel, tt=tt),
        grid_spec=grid_spec,
        out_shape=jax.ShapeDtypeStruct((n, d), F32),
        compiler_params=_params(("arbitrary",)),
        name="combine_final_norm",
    )(wtile, wblk, wvalid, z, ptok, x1, gt2, gf.reshape(1, d))


def _routing_lists(afft, thr_bits, cap, tm, tt, blk):
    e, n = afft.shape
    npairs = e * cap
    bits = lax.bitcast_convert_type(afft, I32)
    thr = thr_bits[:, None]
    gt = bits > thr
    eq = bits == thr
    need = cap - jnp.sum(gt, axis=1, dtype=I32)
    sel = gt | (eq & (jnp.cumsum(eq.astype(I32), axis=1) <= need[:, None]))
    seli = sel.astype(I32)
    incl = jnp.cumsum(seli, axis=1)
    ranks = jnp.arange(1, cap + 1, dtype=I32)
    idx = jax.vmap(lambda row: jnp.searchsorted(row, ranks, side="left"))(incl).astype(I32)
    cnt = jnp.sum(seli, axis=0)
    tok_end = jnp.cumsum(cnt)
    tok_off = tok_end - cnt
    slot = tok_off[None, :] + jnp.cumsum(seli, axis=0) - seli
    dst = jnp.take_along_axis(slot, idx, axis=1)
    gate = jnp.take_along_axis(afft, idx, axis=1)
    ptok = jnp.searchsorted(tok_end, jnp.arange(npairs, dtype=I32), side="right").astype(I32)

    nsteps = npairs // tm
    meta = jnp.concatenate([idx.reshape(nsteps, tm), dst.reshape(nsteps, tm)], axis=1)
    gate8 = jnp.broadcast_to(gate.reshape(nsteps, 1, tm), (nsteps, 8, tm))

    ntile = n // tt
    nblk = npairs // blk
    start = tok_off[::tt]
    end = jnp.concatenate([start[1:], jnp.full((1,), npairs, I32)])
    b0 = jnp.minimum(start // blk, nblk - 1)
    b1 = jnp.where(end > start, (end - 1) // blk, b0)
    nb = b1 - b0 + 1
    wend = jnp.cumsum(nb)
    woff = wend - nb
    nw = nblk + ntile
    w = jnp.arange(nw, dtype=I32)
    wt = jnp.minimum(jnp.searchsorted(wend, w, side="right").astype(I32), ntile - 1)
    valid = (w < wend[-1]).astype(I32)
    wblk = jnp.where(valid == 1, b0[wt] + (w - woff[wt]), b1[ntile - 1])
    return meta, gate8, ptok.reshape(nblk, 1, blk), wt, wblk.astype(I32), valid


def _trunk(x, mod, w, tm_proj=512, tq=256, kc=2048, tm_moe=256, tt=256, blk=256):
    b, s, d = x.shape
    n = b * s
    cap = CAPACITY_FACTOR * n // N_EXPERTS
    sh1, sc1, gt1, sh2, sc2, gt2 = [m.reshape(b, 1, d) for m in jnp.split(mod, N_MOD, axis=-1)]

    qa, kat, va, qb, kbt, vb, sga, sgb = _in_projection(
        x, sc1, sh1, w["norm1_g"], w["w_in"], w["q_norm_g"], w["k_norm_g"], tm_proj)
    oa = _global_attention(qa, kat, va, tq, kc)
    ob = _window_attention(qb, kbt, vb, w["rel_bias"], w["sink"])
    x1, h2p, afft = _out_projection(oa, ob, sga, sgb, x, gt1, sc2, sh2, w["norm2_g"],
                                    w["w_branch_a"], w["w_branch_b"], w["w_out"], w["w_router"], tm_proj)
    thr = _capacity_threshold(afft, cap)[:, 0]
    meta, gate8, ptok, wtile, wblk, wvalid = _routing_lists(afft, thr, cap, tm_moe, tt, blk)
    z = _expert_mlp(meta, gate8, h2p, w["w_e_gate"], w["w_e_up"], w["w_e_down"], tm_moe)
    y = _combine(wtile, wblk, wvalid, z, ptok, x1.reshape(n, d), gt2, w["norm_f_g"], s, tt, blk)
    return y.reshape(b, s, d)


def kernel(x_prompt, x_sample, c_prompt, c_sample, w_ada, b_ada, norm1_g, w_in, q_norm_g, k_norm_g, sink,
           w_branch_a, w_branch_b, w_out, norm2_g, w_router, w_e_gate, w_e_up, w_e_down, rel_bias, norm_f_g):
    assert w_ada.shape[0] == 1, "single layer"
    bp, bs = c_prompt.shape[0], c_sample.shape[0]
    rows = -(-(bp + bs) // 16) * 16
    c = jnp.concatenate([c_prompt, c_sample, jnp.zeros((rows - bp - bs, D_MODEL), F32)], axis=0)
    mod = _modulation(c, w_ada[0], b_ada[0])

    router = jnp.zeros((D_MODEL, LANES), BF16).at[:, :N_EXPERTS].set(w_router[0].astype(BF16))
    w = dict(norm1_g=norm1_g[0], w_in=w_in[0].astype(BF16), q_norm_g=q_norm_g[0], k_norm_g=k_norm_g[0],
             sink=sink[0], w_branch_a=w_branch_a[0].astype(BF16), w_branch_b=w_branch_b[0].astype(BF16),
             w_out=w_out[0].astype(BF16), norm2_g=norm2_g[0], w_router=router,
             w_e_gate=w_e_gate[0].astype(BF16), w_e_up=w_e_up[0].astype(BF16),
             w_e_down=w_e_down[0].astype(BF16), rel_bias=rel_bias, norm_f_g=norm_f_g)
    y_prompt = _trunk(x_prompt, mod[:bp], w)
    y_sample = _trunk(x_sample, mod[bp:bp + bs], w)
    return (y_prompt, y_sample)
```

```python
import functools
import math

import jax
import jax.numpy as jnp
import numpy as np
from jax import lax
from jax.experimental import pallas as pl
from jax.experimental.pallas import tpu as pltpu

F32 = jnp.float32
BF16 = jnp.bfloat16
I32 = jnp.int32
U32 = jnp.uint32

D_MODEL = 1024
HEAD_DIM = 64
A_HEADS = 8
A_KV_HEADS = 2
B_HEADS = 8
B_KV_HEADS = 2
A_WIDTH = A_HEADS * HEAD_DIM
B_WIDTH = B_HEADS * HEAD_DIM
KV_WIDTH = A_KV_HEADS * HEAD_DIM
Q_BLOCK = 128
WINDOW = 128
GRID_W = 64
ROPE_THETA = 10000.0
AXIS_DIM = HEAD_DIM // 2
N_BUCKETS = 32
MAX_DISTANCE = 128
N_EXPERTS = 16
CAPACITY_FACTOR = 2
D_FF = 2048
N_MOD = 6
EPS = 1e-6
NEG_INF = -1e30
IN_COLS = A_WIDTH + 2 * KV_WIDTH + B_WIDTH + 2 * KV_WIDTH + 2 * D_MODEL

LANES = 128
PACKED = D_MODEL // 2
VMEM_LIMIT = 56 * 1024 * 1024

_QA, _KA, _VA = 0, A_WIDTH, A_WIDTH + KV_WIDTH
_QB = A_WIDTH + 2 * KV_WIDTH
_KB, _VB = _QB + B_WIDTH, _QB + B_WIDTH + KV_WIDTH
_GA = _QB + B_WIDTH + 2 * KV_WIDTH
_GB = _GA + D_MODEL


def _params(sem, vmem=VMEM_LIMIT):
    return pltpu.CompilerParams(dimension_semantics=sem, vmem_limit_bytes=vmem)


def _dot(a, b):
    return jnp.dot(a, b, preferred_element_type=F32)


def _mod_kernel(c_ref, w_ref, b_ref, o_ref):
    c = c_ref[...]
    s = c * jax.nn.sigmoid(c)
    o_ref[...] = _dot(s.astype(BF16), w_ref[...].astype(BF16)) + b_ref[...]


def _modulation(c, w_ada, b_ada):
    bp, d = c.shape
    n = w_ada.shape[1]
    tn = 1536
    return pl.pallas_call(
        _mod_kernel,
        grid=(n // tn,),
        in_specs=[pl.BlockSpec((bp, d), lambda j: (0, 0)),
                  pl.BlockSpec((d, tn), lambda j: (0, j)),
                  pl.BlockSpec((1, tn), lambda j: (0, j))],
        out_specs=pl.BlockSpec((bp, tn), lambda j: (0, j)),
        out_shape=jax.ShapeDtypeStruct((bp, n), F32),
        compiler_params=_params(("arbitrary",)),
        name="modulation",
    )(c, w_ada, b_ada.reshape(1, n))


def _swap16(x):
    n = x.shape[-1]
    left = pltpu.roll(x, n - 16, axis=1)
    right = pltpu.roll(x, 16, axis=1)
    lane = lax.broadcasted_iota(I32, x.shape, 1)
    return jnp.where((lane % 32) < 16, left, right)


def _inproj_kernel(x_ref, sc_ref, sh_ref, g1_ref, w_ref, aq_ref, bq_ref, ak_ref, bk_ref, hm_ref,
                   qa_ref, kat_ref, va_ref, qb_ref, kbt_ref, vb_ref, sga_ref, sgb_ref):
    x = x_ref[0]
    ms = jnp.mean(x * x, axis=-1, keepdims=True)
    xn = x * lax.rsqrt(ms + EPS) * g1_ref[...]
    h = (xn * (1.0 + sc_ref[0]) + sh_ref[0]).astype(BF16)

    def proj(lo, width):
        return _dot(h, w_ref[:, lo:lo + width])

    def head_rms(q):
        pieces = []
        for j in range(q.shape[1] // LANES):
            qq = q[:, j * LANES:(j + 1) * LANES]
            pieces.append(_dot((qq * qq).astype(BF16), hm_ref[...]))
        msq = pieces[0] if len(pieces) == 1 else jnp.concatenate(pieces, axis=1)
        return q * lax.rsqrt(msq + EPS)

    def rope(qh, a_ref, b_ref):
        reps = qh.shape[1] // LANES
        a = a_ref[...]
        b = b_ref[...]
        if reps > 1:
            a = jnp.concatenate([a] * reps, axis=1)
            b = jnp.concatenate([b] * reps, axis=1)
        return qh * a + _swap16(qh) * b

    qa = rope(head_rms(proj(_QA, A_WIDTH)), aq_ref, bq_ref)
    qa_ref[0] = qa.astype(BF16)
    ka = rope(head_rms(proj(_KA, KV_WIDTH)), ak_ref, bk_ref)
    kat_ref[0] = ka.T.astype(BF16)
    va_ref[0] = proj(_VA, KV_WIDTH).astype(BF16)
    qb_ref[0] = (proj(_QB, B_WIDTH) * (HEAD_DIM ** -0.5)).astype(BF16)
    kbt_ref[0] = proj(_KB, KV_WIDTH).T.astype(BF16)
    vb_ref[0] = proj(_VB, KV_WIDTH).astype(BF16)
    sga_ref[0] = jax.nn.sigmoid(proj(_GA, D_MODEL)).astype(BF16)
    sgb_ref[0] = jax.nn.sigmoid(proj(_GB, D_MODEL)).astype(BF16)


def _rope_tables(s, gain, scale):
    pos = jnp.arange(s, dtype=I32)
    row = (pos // GRID_W).astype(F32)
    col = (pos % GRID_W).astype(F32)
    inv_freq = ROPE_THETA ** (-jnp.arange(0, AXIS_DIM, 2, dtype=F32) / AXIS_DIM)
    ang_r = row[:, None] * inv_freq
    ang_c = col[:, None] * inv_freq
    cos64 = jnp.concatenate([jnp.cos(ang_r)] * 2 + [jnp.cos(ang_c)] * 2, axis=1)
    sin64 = jnp.concatenate([-jnp.sin(ang_r), jnp.sin(ang_r), -jnp.sin(ang_c), jnp.sin(ang_c)], axis=1)
    g = gain.astype(F32)
    gp = jnp.concatenate([g[16:32], g[0:16], g[48:64], g[32:48]])
    a = cos64 * g[None, :] * scale
    b = sin64 * gp[None, :] * scale
    return jnp.concatenate([a, a], axis=1), jnp.concatenate([b, b], axis=1)


def _in_projection(x, sc1, sh1, g1, w_in, q_gain, k_gain, tm):
    b, s, d = x.shape
    aq, bq = _rope_tables(s, q_gain, HEAD_DIM ** -0.5)
    ak, bk = _rope_tables(s, k_gain, 1.0)
    head_mean = jnp.asarray(np.kron(np.eye(2), np.full((HEAD_DIM, HEAD_DIM), 1.0 / HEAD_DIM)), BF16)
    tok = lambda w: pl.BlockSpec((1, tm, w), lambda i, t: (i, t, 0))
    tr = pl.BlockSpec((1, KV_WIDTH, tm), lambda i, t: (i, 0, t))
    vec = pl.BlockSpec((1, 1, d), lambda i, t: (i, 0, 0))
    tab = pl.BlockSpec((tm, LANES), lambda i, t: (t, 0))
    sd = jax.ShapeDtypeStruct
    return pl.pallas_call(
        _inproj_kernel,
        grid=(b, s // tm),
        in_specs=[tok(d), vec, vec,
                  pl.BlockSpec((1, d), lambda i, t: (0, 0)),
                  pl.BlockSpec((d, IN_COLS), lambda i, t: (0, 0)),
                  tab, tab, tab, tab,
                  pl.BlockSpec((LANES, LANES), lambda i, t: (0, 0))],
        out_specs=[tok(A_WIDTH), tr, tok(KV_WIDTH), tok(B_WIDTH), tr, tok(KV_WIDTH), tok(d), tok(d)],
        out_shape=[sd((b, s, A_WIDTH), BF16), sd((b, KV_WIDTH, s), BF16), sd((b, s, KV_WIDTH), BF16),
                   sd((b, s, B_WIDTH), BF16), sd((b, KV_WIDTH, s), BF16), sd((b, s, KV_WIDTH), BF16),
                   sd((b, s, d), BF16), sd((b, s, d), BF16)],
        compiler_params=_params(("arbitrary", "arbitrary")),
        name="in_projection",
    )(x, sc1, sh1, g1.reshape(1, d), w_in, aq, bq, ak, bk, head_mean)


def _half_lane_variants(v, k_is_one):
    lane = lax.broadcasted_iota(I32, v.shape, 1)
    vr = pltpu.roll(v, HEAD_DIM, axis=1)
    own_lo = jnp.where(k_is_one, vr, v)
    own_hi = jnp.where(k_is_one, v, vr)
    lo = jnp.where(lane < HEAD_DIM, own_lo, 0.0)
    hi = jnp.where(lane >= HEAD_DIM, own_hi, 0.0)
    return lo, hi


def _gattn_kernel(q_ref, kt_ref, v_ref, o_ref, vv_ref, s_ref, *, kc):
    k = pl.program_id(1)
    qi = pl.program_id(2)
    s_len = kt_ref.shape[2]
    nc = s_len // kc

    @pl.when(qi == 0)
    def _():
        lo, hi = _half_lane_variants(v_ref[0].astype(F32), k == 1)
        vv_ref[0] = lo.astype(BF16)
        vv_ref[1] = hi.astype(BF16)

    group = A_HEADS // A_KV_HEADS
    for j in range(group // 2):
        pair = None
        for half in range(2):
            g = 2 * j + half
            q = q_ref[0, :, g * HEAD_DIM:(g + 1) * HEAD_DIM]
            m = None
            for c in range(nc):
                sc = _dot(q, kt_ref[0, :, c * kc:(c + 1) * kc])
                s_ref[:, c * kc:(c + 1) * kc] = sc
                mc = jnp.max(sc, axis=-1, keepdims=True)
                m = mc if m is None else jnp.maximum(m, mc)
            l = None
            acc = None
            for c in range(nc):
                p = jnp.exp(s_ref[:, c * kc:(c + 1) * kc] - m)
                lc = jnp.sum(p, axis=-1, keepdims=True)
                ac = _dot(p.astype(BF16), vv_ref[half, c * kc:(c + 1) * kc, :])
                l = lc if l is None else l + lc
                acc = ac if acc is None else acc + ac
            o = acc / l
            pair = o if pair is None else pair + o
        o_ref[0, :, j * LANES:(j + 1) * LANES] = pair.astype(o_ref.dtype)


def _global_attention(qa, kat, va, tq, kc):
    b, s, _ = qa.shape
    gw = A_WIDTH // A_KV_HEADS
    kc = min(kc, s)
    return pl.pallas_call(
        functools.partial(_gattn_kernel, kc=kc),
        grid=(b, A_KV_HEADS, s // tq),
        in_specs=[pl.BlockSpec((1, tq, gw), lambda i, k, t: (i, t, k)),
                  pl.BlockSpec((1, HEAD_DIM, s), lambda i, k, t: (i, k, 0)),
                  pl.BlockSpec((1, s, KV_WIDTH), lambda i, k, t: (i, 0, 0))],
        out_specs=pl.BlockSpec((1, tq, gw), lambda i, k, t: (i, t, k)),
        out_shape=jax.ShapeDtypeStruct((b, s, A_WIDTH), BF16),
        scratch_shapes=[pltpu.VMEM((2, s, KV_WIDTH), BF16), pltpu.VMEM((tq, s), F32)],
        compiler_params=_params(("arbitrary", "arbitrary", "arbitrary")),
        name="global_attention",
    )(qa, kat, va)


def _t5_bucket_np(rel):
    half = N_BUCKETS // 2
    max_exact = half // 2
    base = (rel > 0).astype(np.int32) * half
    n = np.abs(rel)
    large = max_exact + (np.log(np.maximum(n, 1).astype(np.float32) / max_exact)
                         / math.log(MAX_DISTANCE / max_exact) * (half - max_exact)).astype(np.int32)
    large = np.minimum(large, half - 1)
    return base + np.where(n < max_exact, n, large)


def _wattn_kernel(q_ref, ktp_ref, ktc_ref, ktn_ref, vp_ref, vc_ref, vn_ref, bias_ref, sink_ref, o_ref):
    i = pl.program_id(1)
    nb = pl.num_programs(1)
    span = Q_BLOCK + 2 * WINDOW
    kt = jnp.concatenate([ktp_ref[0], ktc_ref[0], ktn_ref[0]], axis=1)
    v = jnp.concatenate([vp_ref[0], vc_ref[0], vn_ref[0]], axis=0).astype(F32)
    col = lax.broadcasted_iota(I32, (1, span), 1)
    inside = ((col >= WINDOW) | (i > 0)) & ((col < WINDOW + Q_BLOCK) | (i < nb - 1))
    group = B_HEADS // B_KV_HEADS
    for k in range(B_KV_HEADS):
        lo, hi = _half_lane_variants(v, k == 1)
        vv = (lo.astype(BF16), hi.astype(BF16))
        ktk = kt[k * HEAD_DIM:(k + 1) * HEAD_DIM, :]
        for j in range(group // 2):
            pair = None
            for half in range(2):
                h = k * group + 2 * j + half
                q = q_ref[0, :, h * HEAD_DIM:(h + 1) * HEAD_DIM]
                logits = _dot(q, ktk) + bias_ref[h]
                logits = jnp.where(inside, logits, NEG_INF)
                sink = sink_ref[h:h + 1, 0:1]
                m = jnp.maximum(jnp.max(logits, axis=-1, keepdims=True), sink)
                p = jnp.exp(logits - m)
                den = jnp.sum(p, axis=-1, keepdims=True) + jnp.exp(sink - m)
                o = _dot(p.astype(BF16), vv[half]) / den
                pair = o if pair is None else pair + o
            pj = k * (group // 2) + j
            o_ref[0, :, pj * LANES:(pj + 1) * LANES] = pair.astype(o_ref.dtype)


def _window_attention(qb, kbt, vb, rel_bias, sink):
    b, s, _ = qb.shape
    nb = s // Q_BLOCK
    span = Q_BLOCK + 2 * WINDOW
    rel = np.arange(span)[None, :] - WINDOW - np.arange(Q_BLOCK)[:, None]
    band = np.abs(rel) <= WINDOW
    bias = rel_bias.astype(F32)[jnp.asarray(_t5_bucket_np(rel))]
    bias = jnp.where(jnp.asarray(band)[None], bias.transpose(2, 0, 1), NEG_INF)
    sink_b = jnp.broadcast_to(sink.astype(F32)[:, None], (B_HEADS, LANES))
    prev = lambda i, t: jnp.maximum(t - 1, 0)
    nxt = lambda i, t: jnp.minimum(t + 1, nb - 1)
    ktspec = lambda f: pl.BlockSpec((1, KV_WIDTH, Q_BLOCK), lambda i, t: (i, 0, f(i, t)))
    vspec = lambda f: pl.BlockSpec((1, Q_BLOCK, KV_WIDTH), lambda i, t: (i, f(i, t), 0))
    cur = lambda i, t: t
    return pl.pallas_call(
        _wattn_kernel,
        grid=(b, nb),
        in_specs=[pl.BlockSpec((1, Q_BLOCK, B_WIDTH), lambda i, t: (i, t, 0)),
                  ktspec(prev), ktspec(cur), ktspec(nxt),
                  vspec(prev), vspec(cur), vspec(nxt),
                  pl.BlockSpec((B_HEADS, Q_BLOCK, span), lambda i, t: (0, 0, 0)),
                  pl.BlockSpec((B_HEADS, LANES), lambda i, t: (0, 0))],
        out_specs=pl.BlockSpec((1, Q_BLOCK, B_WIDTH), lambda i, t: (i, t, 0)),
        out_shape=jax.ShapeDtypeStruct((b, s, B_WIDTH), BF16),
        compiler_params=_params(("arbitrary", "arbitrary")),
        name="window_attention",
    )(qb, kbt, kbt, kbt, vb, vb, vb, bias, sink_b)


def _pack_bf16_pairs(x):
    k = x.shape[1] // 2
    hi = pltpu.bitcast(x[:, :k].astype(BF16).astype(F32), U32)
    lo = pltpu.bitcast(x[:, k:].astype(BF16).astype(F32), U32)
    return hi | (lo >> 16)


def _unpack_bf16_pairs(w):
    hi = pltpu.bitcast(w & jnp.uint32(0xFFFF0000), F32)
    lo = pltpu.bitcast(w << 16, F32)
    return jnp.concatenate([hi, lo], axis=1).astype(BF16)


def _outproj_kernel(oa_ref, ob_ref, sga_ref, sgb_ref, x_ref, gt1_ref, sc2_ref, sh2_ref, g2_ref,
                    wa_ref, wb_ref, wo_ref, wr_ref, x1_ref, h2p_ref, afft_ref):
    a = _dot(oa_ref[0], wa_ref[...])
    b = _dot(ob_ref[0], wb_ref[...])
    merged = sga_ref[0].astype(F32) * a + sgb_ref[0].astype(F32) * b
    y = _dot(merged.astype(BF16), wo_ref[...])
    x1 = x_ref[0] + gt1_ref[0] * y
    x1_ref[0] = x1
    ms = jnp.mean(x1 * x1, axis=-1, keepdims=True)
    h2 = (x1 * lax.rsqrt(ms + EPS) * g2_ref[...]) * (1.0 + sc2_ref[0]) + sh2_ref[0]
    h2p_ref[...] = _pack_bf16_pairs(h2).reshape(h2p_ref.shape)
    logits = _dot(h2.astype(BF16), wr_ref[...])
    lane = lax.broadcasted_iota(I32, logits.shape, 1)
    logits = jnp.where(lane < N_EXPERTS, logits, -jnp.inf)
    m = jnp.max(logits, axis=-1, keepdims=True)
    e = jnp.exp(logits - m)
    aff = e / jnp.sum(e, axis=-1, keepdims=True)
    afft_ref[...] = aff.T[:N_EXPERTS, :]


def _out_projection(oa, ob, sga, sgb, x, gt1, sc2, sh2, g2, wa, wb, wo, wr, tm):
    b, s, d = x.shape
    nt = s // tm
    n = b * s
    tok = lambda w: pl.BlockSpec((1, tm, w), lambda i, t: (i, t, 0))
    vec = pl.BlockSpec((1, 1, d), lambda i, t: (i, 0, 0))
    full = lambda r, c: pl.BlockSpec((r, c), lambda i, t: (0, 0))
    sd = jax.ShapeDtypeStruct
    return pl.pallas_call(
        _outproj_kernel,
        grid=(b, nt),
        in_specs=[tok(A_WIDTH), tok(B_WIDTH), tok(d), tok(d), tok(d), vec, vec, vec, full(1, d),
                  full(A_WIDTH, d), full(B_WIDTH, d), full(d, d), full(d, LANES)],
        out_specs=[tok(d),
                   pl.BlockSpec((tm, 1, PACKED), lambda i, t: (i * nt + t, 0, 0)),
                   pl.BlockSpec((N_EXPERTS, tm), lambda i, t: (0, i * nt + t))],
        out_shape=[sd((b, s, d), F32), sd((n, 1, PACKED), U32), sd((N_EXPERTS, n), F32)],
        compiler_params=_params(("arbitrary", "arbitrary")),
        name="out_projection",
    )(oa, ob, sga, sgb, x, gt1, sc2, sh2, g2.reshape(1, d), wa, wb, wo, wr)


def _threshold_kernel(aff_ref, thr_ref, cut_ref, *, cap):
    bits = pltpu.bitcast(aff_ref[...], I32)
    n = bits.shape[1]

    def value_step(i, lo):
        cand = lo | jnp.left_shift(jnp.int32(1), 30 - i)
        cnt = jnp.sum((bits >= cand).astype(F32), axis=1, keepdims=True)
        return jnp.where(cnt >= cap, cand, lo)

    thr = lax.fori_loop(0, 31, value_step, jnp.zeros((N_EXPERTS, 1), I32))
    need = cap - jnp.sum((bits > thr).astype(F32), axis=1, keepdims=True)
    eq = bits == thr
    tpos = lax.broadcasted_iota(I32, (1, n), 1)
    nbits = max(n.bit_length(), 1)

    def index_step(i, cut):
        cand = cut + jnp.left_shift(jnp.int32(1), nbits - 1 - i)
        below = jnp.sum(jnp.where(eq & (tpos < cand), 1.0, 0.0), axis=1, keepdims=True)
        return jnp.where((below <= need) & (cand <= n), cand, cut)

    cut = lax.fori_loop(0, nbits, index_step, jnp.zeros((N_EXPERTS, 1), I32))
    thr_ref[...] = jnp.broadcast_to(thr, thr_ref.shape)
    cut_ref[...] = jnp.broadcast_to(cut, cut_ref.shape)


def _capacity_threshold(afft, cap):
    e, n = afft.shape
    out = pl.BlockSpec((e, LANES), lambda i: (0, 0))
    return pl.pallas_call(
        functools.partial(_threshold_kernel, cap=cap),
        grid=(1,),
        in_specs=[pl.BlockSpec((e, n), lambda i: (0, 0))],
        out_specs=[out, out],
        out_shape=[jax.ShapeDtypeStruct((e, LANES), I32)] * 2,
        compiler_params=_params(("arbitrary",)),
        name="capacity_threshold",
    )(afft)


_AUX_ROWS = 8


def _prefix_kernel(aff_ref, thr_ref, cut_ref, u_ref, ls_ref, posm_ref, slot_ref, tokrow_ref, offs_ref,
                   run_ref, *, tb, ch):
    i = pl.program_id(0)

    @pl.when(i == 0)
    def _():
        run_ref[...] = jnp.zeros_like(run_ref)

    bits = pltpu.bitcast(aff_ref[...], I32)
    tpos = i * tb + lax.broadcasted_iota(I32, (1, tb), 1)
    thr = thr_ref[:, 0:1]
    sel = (bits > thr) | ((bits == thr) & (tpos < cut_ref[:, 0:1]))
    s = jnp.where(sel, 1.0, 0.0)
    for j in range(tb // ch):
        sj = s[:, j * ch:(j + 1) * ch]
        cntj = jnp.sum(sj, axis=0, keepdims=True)
        x = jnp.concatenate([sj, cntj, jnp.zeros((_AUX_ROWS - 1, ch), F32)], axis=0).astype(BF16)
        run = run_ref[:, 0:1]
        incl = _dot(x, u_ref[...]) + run
        posm_ref[:, j * ch:(j + 1) * ch] = jnp.where(sj > 0.0, incl[:N_EXPERTS] - sj, -1.0)
        tok_end = incl[N_EXPERTS:N_EXPERTS + 1]
        tok_off = tok_end - cntj
        slot_ref[:, j * ch:(j + 1) * ch] = tok_off + _dot(ls_ref[...], sj.astype(BF16))
        tokrow_ref[:, j * ch:(j + 1) * ch] = jnp.concatenate(
            [tok_off, tok_end, jnp.zeros((6, ch), F32)], axis=0)
        offs_ref[j] = jnp.broadcast_to(run[:N_EXPERTS], (N_EXPERTS, LANES))
        run_ref[...] = jnp.broadcast_to(incl[:, ch - 1:ch], run_ref.shape)


def _routing_prefix(afft, thr, cut, tb, ch):
    e, n = afft.shape
    upper = jnp.asarray(np.triu(np.ones((ch, ch), np.float32)), BF16)
    lower_strict = jnp.asarray(np.tril(np.ones((e, e), np.float32), -1), BF16)
    rows = lambda r: pl.BlockSpec((r, tb), lambda i: (0, i))
    const = lambda r, c: pl.BlockSpec((r, c), lambda i: (0, 0))
    sd = jax.ShapeDtypeStruct
    return pl.pallas_call(
        functools.partial(_prefix_kernel, tb=tb, ch=ch),
        grid=(n // tb,),
        in_specs=[rows(e), const(e, LANES), const(e, LANES), const(ch, ch), const(e, e)],
        out_specs=[rows(e), rows(e), rows(8), pl.BlockSpec((tb // ch, e, LANES), lambda i: (i, 0, 0))],
        out_shape=[sd((e, n), F32), sd((e, n), F32), sd((8, n), F32), sd((n // ch, e, LANES), F32)],
        scratch_shapes=[pltpu.VMEM((e + _AUX_ROWS, LANES), F32)],
        compiler_params=_params(("arbitrary",)),
        name="routing_prefix",
    )(afft, thr, cut, upper, lower_strict)


def _compact_kernel(win_ref, aff_ref, posm_ref, slot_ref, list_ref, *, tb, ch, nc):
    i = pl.program_id(0)

    @pl.when(i == 0)
    def _():
        list_ref[...] = jnp.zeros_like(list_ref)

    rank = lax.broadcasted_iota(I32, (2 * ch, ch), 0).astype(F32)
    tok_local = lax.broadcasted_iota(I32, (1, ch), 1).astype(F32)

    def expert_body(e, carry):
        for j in range(tb // ch):
            c = i * (tb // ch) + j
            w0 = win_ref[e * nc + c]
            lanes = slice(j * ch, (j + 1) * ch)
            rel = posm_ref[pl.ds(e, 1), lanes] - (w0 * ch).astype(F32)
            onehot_t = jnp.where(rank == rel, 1.0, 0.0).astype(BF16)
            slot = slot_ref[pl.ds(e, 1), lanes].astype(I32)
            g = aff_ref[pl.ds(e, 1), lanes]
            g_hi = g.astype(BF16).astype(F32)
            g_mid = (g - g_hi).astype(BF16).astype(F32)
            g_lo = g - g_hi - g_mid
            vals = jnp.concatenate(
                [tok_local, jnp.full((1, ch), c, I32).astype(F32),
                 (slot & 255).astype(F32), ((slot >> 8) & 255).astype(F32), (slot >> 16).astype(F32),
                 g_hi, g_mid, g_lo], axis=0).astype(BF16)
            out = lax.dot_general(vals, onehot_t, (((1,), (1,)), ((), ())), preferred_element_type=F32)
            list_ref[e, w0] += out[:, :ch]
            list_ref[e, w0 + 1] += out[:, ch:]
        return carry

    lax.fori_loop(0, N_EXPERTS, expert_body, 0)


def _routing_compact(win, afft, posm, slot, cap, tb, ch):
    e, n = afft.shape
    nc = n // ch
    nwin = cap // ch
    rows = pl.BlockSpec((e, tb), lambda i, w: (0, i))
    grid_spec = pltpu.PrefetchScalarGridSpec(
        num_scalar_prefetch=1,
        grid=(n // tb,),
        in_specs=[rows, rows, rows],
        out_specs=pl.BlockSpec((e, nwin + 2, 8, ch), lambda i, w: (0, 0, 0, 0)),
    )
    return pl.pallas_call(
        functools.partial(_compact_kernel, tb=tb, ch=ch, nc=nc),
        grid_spec=grid_spec,
        out_shape=jax.ShapeDtypeStruct((e, nwin + 2, 8, ch), F32),
        compiler_params=_params(("arbitrary",)),
        name="routing_compact",
    )(win, afft, posm, slot)


def _routing(afft, cap, tm, tt, blk, tb=2048, ch=256):
    e, n = afft.shape
    assert n // ch <= 256 and tm == ch and e * cap < (1 << 24)
    tb = min(tb, n)
    thr, cut = _capacity_threshold(afft, cap)
    posm, slot, tokrow, offs = _routing_prefix(afft, thr, cut, tb, ch)
    win = (offs[:, :, 0].astype(I32) // ch).T.reshape(-1)
    lists = _routing_compact(win, afft, posm, slot, cap, tb, ch)[:, :cap // ch]
    nsteps = e * cap // tm
    idx = (lists[:, :, 1] * ch + lists[:, :, 0]).astype(I32).reshape(nsteps, tm)
    dst = (lists[:, :, 2] + lists[:, :, 3] * 256.0 + lists[:, :, 4] * 65536.0).astype(I32).reshape(nsteps, tm)
    gate = ((lists[:, :, 5] + lists[:, :, 6]) + lists[:, :, 7]).reshape(nsteps, 1, tm)
    meta = jnp.concatenate([idx, dst], axis=1)
    gate8 = jnp.broadcast_to(gate, (nsteps, 8, tm))

    npairs = e * cap
    ntile, nblk = n // tt, npairs // blk
    start = tokrow[0, ::tt].astype(I32)
    end = jnp.concatenate([start[1:], jnp.full((1,), npairs, I32)])
    b0 = jnp.minimum(start // blk, nblk - 1)
    b1 = jnp.where(end > start, (end - 1) // blk, b0)
    nb = b1 - b0 + 1
    wend = jnp.cumsum(nb)
    woff = wend - nb
    w = jnp.arange(nblk + ntile, dtype=I32)
    wt = jnp.minimum(jnp.sum((w[:, None] >= wend[None, :]).astype(I32), axis=1), ntile - 1)
    valid = (w < wend[-1]).astype(I32)
    wblk = jnp.where(valid == 1, b0[wt] + (w - woff[wt]), b1[ntile - 1]).astype(I32)
    return meta, gate8, tokrow, wt, wblk, valid


def _moe_kernel(meta_hbm, gate_ref, h2_hbm, wg_ref, wu_ref, wd_ref, z_hbm,
                meta_smem, xbuf, x2d, zbuf, sem_m, sem_g, sem_s, *, tm, nsteps):
    nt = pl.num_programs(1)
    s = pl.program_id(0) * nt + pl.program_id(1)
    slot = s % 2

    def meta_copy(step, mslot):
        return pltpu.make_async_copy(meta_hbm.at[step], meta_smem.at[mslot], sem_m)

    def issue_gather(mslot, bslot):
        for r in range(tm):
            tok = meta_smem[mslot, r]
            pltpu.make_async_copy(h2_hbm.at[tok], xbuf.at[bslot, r], sem_g.at[bslot]).start()

    def wait_gather(bslot):
        pltpu.make_async_copy(h2_hbm.at[pl.ds(0, tm)], xbuf.at[bslot], sem_g.at[bslot]).wait()

    def issue_scatter(mslot, bslot):
        for r in range(tm):
            dst = meta_smem[mslot, tm + r]
            pltpu.make_async_copy(zbuf.at[bslot, r], z_hbm.at[dst], sem_s.at[bslot]).start()

    def wait_scatter(bslot):
        pltpu.make_async_copy(zbuf.at[bslot], z_hbm.at[pl.ds(0, tm)], sem_s.at[bslot]).wait()

    @pl.when(s == 0)
    def _():
        c0 = meta_copy(0, 0)
        c0.start()
        c0.wait()
        if nsteps > 1:
            c1 = meta_copy(1, 1)
            c1.start()
            c1.wait()
        issue_gather(0, 0)

    @pl.when(s + 2 < nsteps)
    def _():
        meta_copy(s + 2, (s + 2) % 3).start()

    @pl.when(s + 1 < nsteps)
    def _():
        issue_gather((s + 1) % 3, 1 - slot)

    wait_gather(slot)
    x2d[...] = xbuf[slot].reshape(tm, PACKED)
    xe = _unpack_bf16_pairs(x2d[...])
    gate = _dot(xe, wg_ref[...])
    up = _dot(xe, wu_ref[...])
    hid = (gate * jax.nn.sigmoid(gate) * up).astype(BF16)
    ye = _dot(hid, wd_ref[...])
    z = ye * gate_ref[...].T[:, 0:1]

    @pl.when(s >= 2)
    def _():
        wait_scatter(slot)

    zbuf[slot] = _pack_bf16_pairs(z).reshape(tm, 1, PACKED)
    issue_scatter(s % 3, slot)

    @pl.when(s + 2 < nsteps)
    def _():
        meta_copy(s + 2, (s + 2) % 3).wait()

    @pl.when(s == nsteps - 1)
    def _():
        if nsteps > 1:
            wait_scatter(1 - slot)
        wait_scatter(slot)


def _expert_mlp(meta, gate8, h2p, wg, wu, wd, tm):
    nsteps = meta.shape[0]
    nt = nsteps // N_EXPERTS
    d, f = wg.shape[1], wg.shape[2]
    step = lambda e, i: (e * nt + i, 0, 0)
    return pl.pallas_call(
        functools.partial(_moe_kernel, tm=tm, nsteps=nsteps),
        grid=(N_EXPERTS, nt),
        in_specs=[pl.BlockSpec(memory_space=pl.ANY),
                  pl.BlockSpec((None, 8, tm), step),
                  pl.BlockSpec(memory_space=pl.ANY),
                  pl.BlockSpec((None, d, f), lambda e, i: (e, 0, 0)),
                  pl.BlockSpec((None, d, f), lambda e, i: (e, 0, 0)),
                  pl.BlockSpec((None, f, d), lambda e, i: (e, 0, 0))],
        out_specs=pl.BlockSpec(memory_space=pl.ANY),
        out_shape=jax.ShapeDtypeStruct((nsteps * tm, 1, PACKED), U32),
        scratch_shapes=[pltpu.SMEM((3, 2 * tm), I32),
                        pltpu.VMEM((2, tm, 1, PACKED), U32),
                        pltpu.VMEM((tm, PACKED), U32),
                        pltpu.VMEM((2, tm, 1, PACKED), U32),
                        pltpu.SemaphoreType.DMA,
                        pltpu.SemaphoreType.DMA((2,)),
                        pltpu.SemaphoreType.DMA((2,))],
        compiler_params=_params(("arbitrary", "arbitrary")),
        name="expert_mlp",
    )(meta, gate8, h2p, wg, wu, wd)


def _combine_kernel(wtile_ref, wblk_ref, wvalid_ref, z_ref, tokrow_ref, x1_ref, gt2_ref, gf_ref,
                    y_ref, acc_ref, z2d, *, tt):
    w = pl.program_id(0)
    nw = pl.num_programs(0)
    tile = wtile_ref[w]
    first = jnp.logical_or(w == 0, wtile_ref[jnp.maximum(w - 1, 0)] != tile)
    last = jnp.logical_or(w == nw - 1, wtile_ref[jnp.minimum(w + 1, nw - 1)] != tile)

    @pl.when(first)
    def _():
        acc_ref[...] = jnp.zeros_like(acc_ref)

    @pl.when(wvalid_ref[w] == 1)
    def _():
        blk = z2d.shape[0]
        z2d[...] = z_ref[...].reshape(z2d.shape)
        zrows = _unpack_bf16_pairs(z2d[...])
        pair = (wblk_ref[w] * blk + lax.broadcasted_iota(I32, (blk, tt), 0)).astype(F32)
        owned = (pair >= tokrow_ref[0:1, :]) & (pair < tokrow_ref[1:2, :])
        onehot = jnp.where(owned, 1.0, 0.0).T.astype(BF16)
        acc_ref[...] += _dot(onehot, zrows)

    @pl.when(last)
    def _():
        x2 = x1_ref[...] + gt2_ref[0] * acc_ref[...]
        ms = jnp.mean(x2 * x2, axis=-1, keepdims=True)
        y_ref[...] = x2 * lax.rsqrt(ms + EPS) * gf_ref[...]


def _combine(wtile, wblk, wvalid, z, tokrow, x1, gt2, gf, s, tt, blk):
    n, d = x1.shape
    nw = wtile.shape[0]
    grid_spec = pltpu.PrefetchScalarGridSpec(
        num_scalar_prefetch=3,
        grid=(nw,),
        in_specs=[pl.BlockSpec((blk, 1, PACKED), lambda w, wt, wb, wv: (wb[w], 0, 0)),
                  pl.BlockSpec((8, tt), lambda w, wt, wb, wv: (0, wt[w])),
                  pl.BlockSpec((tt, d), lambda w, wt, wb, wv: (wt[w], 0)),
                  pl.BlockSpec((1, 1, d), lambda w, wt, wb, wv: (wt[w] * tt // s, 0, 0)),
                  pl.BlockSpec((1, d), lambda w, wt, wb, wv: (0, 0))],
        out_specs=pl.BlockSpec((tt, d), lambda w, wt, wb, wv: (wt[w], 0)),
        scratch_shapes=[pltpu.VMEM((tt, d), F32), pltpu.VMEM((blk, PACKED), U32)],
    )
    return pl.pallas_call(
        functools.partial(_combine_kernel, tt=tt),
        grid_spec=grid_spec,
        out_shape=jax.ShapeDtypeStruct((n, d), F32),
        compiler_params=_params(("arbitrary",)),
        name="combine_final_norm",
    )(wtile, wblk, wvalid, z, tokrow, x1, gt2, gf.reshape(1, d))


def _trunk(x, mod, w, tm_proj=512, tq=256, kc=2048, tm_moe=256, tt=256, blk=256):
    b, s, d = x.shape
    n = b * s
    cap = CAPACITY_FACTOR * n // N_EXPERTS
    sh1, sc1, gt1, sh2, sc2, gt2 = [m.reshape(b, 1, d) for m in jnp.split(mod, N_MOD, axis=-1)]

    qa, kat, va, qb, kbt, vb, sga, sgb = _in_projection(
        x, sc1, sh1, w["norm1_g"], w["w_in"], w["q_norm_g"], w["k_norm_g"], tm_proj)
    oa = _global_attention(qa, kat, va, tq, kc)
    ob = _window_attention(qb, kbt, vb, w["rel_bias"], w["sink"])
    x1, h2p, afft = _out_projection(oa, ob, sga, sgb, x, gt1, sc2, sh2, w["norm2_g"],
                                    w["w_branch_a"], w["w_branch_b"], w["w_out"], w["w_router"], tm_proj)
    meta, gate8, tokrow, wtile, wblk, wvalid = _routing(afft, cap, tm_moe, tt, blk, ch=tm_moe)
    z = _expert_mlp(meta, gate8, h2p, w["w_e_gate"], w["w_e_up"], w["w_e_down"], tm_moe)
    y = _combine(wtile, wblk, wvalid, z, tokrow, x1.reshape(n, d), gt2, w["norm_f_g"], s, tt, blk)
    return y.reshape(b, s, d)


def kernel(x_prompt, x_sample, c_prompt, c_sample, w_ada, b_ada, norm1_g, w_in, q_norm_g, k_norm_g, sink,
           w_branch_a, w_branch_b, w_out, norm2_g, w_router, w_e_gate, w_e_up, w_e_down, rel_bias, norm_f_g):
    assert w_ada.shape[0] == 1, "single layer"
    bp, bs = c_prompt.shape[0], c_sample.shape[0]
    rows = -(-(bp + bs) // 16) * 16
    c = jnp.concatenate([c_prompt, c_sample, jnp.zeros((rows - bp - bs, D_MODEL), F32)], axis=0)
    mod = _modulation(c, w_ada[0], b_ada[0])

    router = jnp.zeros((D_MODEL, LANES), BF16).at[:, :N_EXPERTS].set(w_router[0].astype(BF16))
    w = dict(norm1_g=norm1_g[0], w_in=w_in[0].astype(BF16), q_norm_g=q_norm_g[0], k_norm_g=k_norm_g[0],
             sink=sink[0], w_branch_a=w_branch_a[0].astype(BF16), w_branch_b=w_branch_b[0].astype(BF16),
             w_out=w_out[0].astype(BF16), norm2_g=norm2_g[0], w_router=router,
             w_e_gate=w_e_gate[0].astype(BF16), w_e_up=w_e_up[0].astype(BF16),
             w_e_down=w_e_down[0].astype(BF16), rel_bias=rel_bias, norm_f_g=norm_f_g)
    y_prompt = _trunk(x_prompt, mod[:bp], w)
    y_sample = _trunk(x_sample, mod[bp:bp + bs], w)
    return (y_prompt, y_sample)
```

```python
import functools
import math

import jax
import jax.numpy as jnp
import numpy as np
from jax import lax
from jax.experimental import pallas as pl
from jax.experimental.pallas import tpu as pltpu

F32 = jnp.float32
BF16 = jnp.bfloat16
I32 = jnp.int32
U32 = jnp.uint32

D_MODEL = 1024
HEAD_DIM = 64
A_HEADS = 8
A_KV_HEADS = 2
B_HEADS = 8
B_KV_HEADS = 2
A_WIDTH = A_HEADS * HEAD_DIM
B_WIDTH = B_HEADS * HEAD_DIM
KV_WIDTH = A_KV_HEADS * HEAD_DIM
Q_BLOCK = 128
WINDOW = 128
GRID_W = 64
ROPE_THETA = 10000.0
AXIS_DIM = HEAD_DIM // 2
N_BUCKETS = 32
MAX_DISTANCE = 128
N_EXPERTS = 16
CAPACITY_FACTOR = 2
D_FF = 2048
N_MOD = 6
EPS = 1e-6
NEG_INF = -1e30
IN_COLS = A_WIDTH + 2 * KV_WIDTH + B_WIDTH + 2 * KV_WIDTH + 2 * D_MODEL

LANES = 128
PACKED = D_MODEL // 2
VMEM_LIMIT = 56 * 1024 * 1024

_QA, _KA, _VA = 0, A_WIDTH, A_WIDTH + KV_WIDTH
_QB = A_WIDTH + 2 * KV_WIDTH
_KB, _VB = _QB + B_WIDTH, _QB + B_WIDTH + KV_WIDTH
_GA = _QB + B_WIDTH + 2 * KV_WIDTH
_GB = _GA + D_MODEL


def _params(sem, vmem=VMEM_LIMIT):
    return pltpu.CompilerParams(dimension_semantics=sem, vmem_limit_bytes=vmem)


def _dot(a, b):
    return jnp.dot(a, b, preferred_element_type=F32)


def _mod_kernel(c_ref, w_ref, b_ref, o_ref):
    c = c_ref[...]
    s = c * jax.nn.sigmoid(c)
    o_ref[...] = _dot(s.astype(BF16), w_ref[...].astype(BF16)) + b_ref[...]


def _modulation(c, w_ada, b_ada):
    bp, d = c.shape
    n = w_ada.shape[1]
    tn = 1536
    return pl.pallas_call(
        _mod_kernel,
        grid=(n // tn,),
        in_specs=[pl.BlockSpec((bp, d), lambda j: (0, 0)),
                  pl.BlockSpec((d, tn), lambda j: (0, j)),
                  pl.BlockSpec((1, tn), lambda j: (0, j))],
        out_specs=pl.BlockSpec((bp, tn), lambda j: (0, j)),
        out_shape=jax.ShapeDtypeStruct((bp, n), F32),
        compiler_params=_params(("arbitrary",)),
        name="modulation",
    )(c, w_ada, b_ada.reshape(1, n))


def _swap16(x):
    n = x.shape[-1]
    left = pltpu.roll(x, n - 16, axis=1)
    right = pltpu.roll(x, 16, axis=1)
    lane = lax.broadcasted_iota(I32, x.shape, 1)
    return jnp.where((lane % 32) < 16, left, right)


def _inproj_kernel(x_ref, sc_ref, sh_ref, g1_ref, w_ref, aq_ref, bq_ref, ak_ref, bk_ref, hm_ref,
                   qa_ref, kat_ref, va_ref, qb_ref, kbt_ref, vb_ref, sga_ref, sgb_ref):
    x = x_ref[0]
    ms = jnp.mean(x * x, axis=-1, keepdims=True)
    xn = x * lax.rsqrt(ms + EPS) * g1_ref[...]
    h = (xn * (1.0 + sc_ref[0]) + sh_ref[0]).astype(BF16)

    def proj(lo, width):
        return _dot(h, w_ref[:, lo:lo + width])

    def head_rms(q):
        pieces = []
        for j in range(q.shape[1] // LANES):
            qq = q[:, j * LANES:(j + 1) * LANES]
            pieces.append(_dot((qq * qq).astype(BF16), hm_ref[...]))
        msq = pieces[0] if len(pieces) == 1 else jnp.concatenate(pieces, axis=1)
        return q * lax.rsqrt(msq + EPS)

    def rope(qh, a_ref, b_ref):
        reps = qh.shape[1] // LANES
        a = a_ref[...]
        b = b_ref[...]
        if reps > 1:
            a = jnp.concatenate([a] * reps, axis=1)
            b = jnp.concatenate([b] * reps, axis=1)
        return qh * a + _swap16(qh) * b

    qa = rope(head_rms(proj(_QA, A_WIDTH)), aq_ref, bq_ref)
    qa_ref[0] = qa.astype(BF16)
    ka = rope(head_rms(proj(_KA, KV_WIDTH)), ak_ref, bk_ref)
    kat_ref[0] = ka.T.astype(BF16)
    va_ref[0] = proj(_VA, KV_WIDTH).astype(BF16)
    qb_ref[0] = (proj(_QB, B_WIDTH) * (HEAD_DIM ** -0.5)).astype(BF16)
    kbt_ref[0] = proj(_KB, KV_WIDTH).T.astype(BF16)
    vb_ref[0] = proj(_VB, KV_WIDTH).astype(BF16)
    sga_ref[0] = jax.nn.sigmoid(proj(_GA, D_MODEL)).astype(BF16)
    sgb_ref[0] = jax.nn.sigmoid(proj(_GB, D_MODEL)).astype(BF16)


def _rope_tables(s, gain, scale):
    pos = jnp.arange(s, dtype=I32)
    row = (pos // GRID_W).astype(F32)
    col = (pos % GRID_W).astype(F32)
    inv_freq = ROPE_THETA ** (-jnp.arange(0, AXIS_DIM, 2, dtype=F32) / AXIS_DIM)
    ang_r = row[:, None] * inv_freq
    ang_c = col[:, None] * inv_freq
    cos64 = jnp.concatenate([jnp.cos(ang_r)] * 2 + [jnp.cos(ang_c)] * 2, axis=1)
    sin64 = jnp.concatenate([-jnp.sin(ang_r), jnp.sin(ang_r), -jnp.sin(ang_c), jnp.sin(ang_c)], axis=1)
    g = gain.astype(F32)
    gp = jnp.concatenate([g[16:32], g[0:16], g[48:64], g[32:48]])
    a = cos64 * g[None, :] * scale
    b = sin64 * gp[None, :] * scale
    return jnp.concatenate([a, a], axis=1), jnp.concatenate([b, b], axis=1)


def _in_projection(x, sc1, sh1, g1, w_in, q_gain, k_gain, tm):
    b, s, d = x.shape
    aq, bq = _rope_tables(s, q_gain, HEAD_DIM ** -0.5 * math.log2(math.e))
    ak, bk = _rope_tables(s, k_gain, 1.0)
    head_mean = jnp.asarray(np.kron(np.eye(2), np.full((HEAD_DIM, HEAD_DIM), 1.0 / HEAD_DIM)), BF16)
    tok = lambda w: pl.BlockSpec((1, tm, w), lambda i, t: (i, t, 0))
    tr = pl.BlockSpec((1, KV_WIDTH, tm), lambda i, t: (i, 0, t))
    vec = pl.BlockSpec((1, 1, d), lambda i, t: (i, 0, 0))
    tab = pl.BlockSpec((tm, LANES), lambda i, t: (t, 0))
    sd = jax.ShapeDtypeStruct
    return pl.pallas_call(
        _inproj_kernel,
        grid=(b, s // tm),
        in_specs=[tok(d), vec, vec,
                  pl.BlockSpec((1, d), lambda i, t: (0, 0)),
                  pl.BlockSpec((d, IN_COLS), lambda i, t: (0, 0)),
                  tab, tab, tab, tab,
                  pl.BlockSpec((LANES, LANES), lambda i, t: (0, 0))],
        out_specs=[tok(A_WIDTH), tr, tok(KV_WIDTH), tok(B_WIDTH), tr, tok(KV_WIDTH), tok(d), tok(d)],
        out_shape=[sd((b, s, A_WIDTH), BF16), sd((b, KV_WIDTH, s), BF16), sd((b, s, KV_WIDTH), BF16),
                   sd((b, s, B_WIDTH), BF16), sd((b, KV_WIDTH, s), BF16), sd((b, s, KV_WIDTH), BF16),
                   sd((b, s, d), BF16), sd((b, s, d), BF16)],
        compiler_params=_params(("arbitrary", "arbitrary")),
        name="in_projection",
    )(x, sc1, sh1, g1.reshape(1, d), w_in, aq, bq, ak, bk, head_mean)


def _half_lane_variants(v, k_is_one):
    lane = lax.broadcasted_iota(I32, v.shape, 1)
    vr = pltpu.roll(v, HEAD_DIM, axis=1)
    own_lo = jnp.where(k_is_one, vr, v)
    own_hi = jnp.where(k_is_one, v, vr)
    lo = jnp.where(lane < HEAD_DIM, own_lo, 0.0)
    hi = jnp.where(lane >= HEAD_DIM, own_hi, 0.0)
    return lo, hi


_SUM_LANE_LO = LANES - 1
_SUM_LANE_HI = 0


def _gattn_kernel(q_ref, kt_ref, v_ref, o_ref, vv_ref, s_ref, qs_ref, os_ref, *, tu):
    k = pl.program_id(1)
    qi = pl.program_id(2)
    tq = q_ref.shape[1]
    group = A_HEADS // A_KV_HEADS
    nu = group * (tq // tu)

    @pl.when(qi == 0)
    def _():
        lo, hi = _half_lane_variants(v_ref[0].astype(F32), k == 1)
        lane = lax.broadcasted_iota(I32, lo.shape, 1)
        vv_ref[0] = jnp.where(lane == _SUM_LANE_LO, 1.0, lo).astype(BF16)
        vv_ref[1] = jnp.where(lane == _SUM_LANE_HI, 1.0, hi).astype(BF16)

    for u in range(nu):
        r, g = divmod(u, group)
        qs_ref[u] = q_ref[0, r * tu:(r + 1) * tu, g * HEAD_DIM:(g + 1) * HEAD_DIM]

    def unit(u, carry):
        sc = _dot(qs_ref[u], kt_ref[0])
        s_ref[...] = sc
        m = jnp.max(sc, axis=-1, keepdims=True)
        p = jnp.exp2(s_ref[...] - m)
        os_ref[u] = _dot(p.astype(BF16), vv_ref[u % 2])
        return carry

    lax.fori_loop(0, nu, unit, 0)

    lane_o = lax.broadcasted_iota(I32, (tu, LANES), 1)
    for r in range(tq // tu):
        for j in range(group // 2):
            even = os_ref[r * group + 2 * j]
            odd = os_ref[r * group + 2 * j + 1]
            o = (jnp.where(lane_o < HEAD_DIM, even, 0.0) / even[:, _SUM_LANE_LO:_SUM_LANE_LO + 1]
                 + jnp.where(lane_o >= HEAD_DIM, odd, 0.0) / odd[:, _SUM_LANE_HI:_SUM_LANE_HI + 1])
            o_ref[0, r * tu:(r + 1) * tu, j * LANES:(j + 1) * LANES] = o.astype(o_ref.dtype)


def _global_attention(qa, kat, va, tq, tu):
    b, s, _ = qa.shape
    gw = A_WIDTH // A_KV_HEADS
    tq = min(tq, s)
    tu = min(tu, tq)
    return pl.pallas_call(
        functools.partial(_gattn_kernel, tu=tu),
        grid=(b, A_KV_HEADS, s // tq),
        in_specs=[pl.BlockSpec((1, tq, gw), lambda i, k, t: (i, t, k)),
                  pl.BlockSpec((1, HEAD_DIM, s), lambda i, k, t: (i, k, 0)),
                  pl.BlockSpec((1, s, KV_WIDTH), lambda i, k, t: (i, 0, 0))],
        out_specs=pl.BlockSpec((1, tq, gw), lambda i, k, t: (i, t, k)),
        out_shape=jax.ShapeDtypeStruct((b, s, A_WIDTH), BF16),
        scratch_shapes=[pltpu.VMEM((2, s, KV_WIDTH), BF16), pltpu.VMEM((tu, s), F32),
                        pltpu.VMEM((gw // HEAD_DIM * (tq // tu), tu, HEAD_DIM), BF16),
                        pltpu.VMEM((gw // HEAD_DIM * (tq // tu), tu, LANES), F32)],
        compiler_params=_params(("arbitrary", "arbitrary", "arbitrary")),
        name="global_attention",
    )(qa, kat, va)


def _t5_bucket_np(rel):
    half = N_BUCKETS // 2
    max_exact = half // 2
    base = (rel > 0).astype(np.int32) * half
    n = np.abs(rel)
    large = max_exact + (np.log(np.maximum(n, 1).astype(np.float32) / max_exact)
                         / math.log(MAX_DISTANCE / max_exact) * (half - max_exact)).astype(np.int32)
    large = np.minimum(large, half - 1)
    return base + np.where(n < max_exact, n, large)


def _stack_order(k):
    group = B_HEADS // B_KV_HEADS
    return [k * group + g for g in (0, 2, 1, 3)]


def _wattn_kernel(q_ref, ktp_ref, ktc_ref, ktn_ref, vp_ref, vc_ref, vn_ref, bias_ref, sink_ref, o_ref):
    t = pl.program_id(1)
    nt = pl.num_programs(1)
    span = Q_BLOCK + 2 * WINDOW
    nqb = q_ref.shape[1] // Q_BLOCK
    kt = jnp.concatenate([ktp_ref[0], ktc_ref[0], ktn_ref[0]], axis=1)
    v = jnp.concatenate([vp_ref[0], vc_ref[0], vn_ref[0]], axis=0).astype(F32)
    col = lax.broadcasted_iota(I32, (1, span), 1)
    first_ok = (col >= WINDOW) | (t > 0)
    last_ok = (col < WINDOW + Q_BLOCK) | (t < nt - 1)
    half_rows = 2 * Q_BLOCK
    for k in range(B_KV_HEADS):
        lo, hi = _half_lane_variants(v, k == 1)
        vlo, vhi = lo.astype(BF16), hi.astype(BF16)
        sink = sink_ref[k][:, 0:1]
        for jb in range(nqb):
            keys = slice(jb * Q_BLOCK, jb * Q_BLOCK + span)
            rows = slice(jb * Q_BLOCK, (jb + 1) * Q_BLOCK)
            q4 = jnp.concatenate([q_ref[0, rows, h * HEAD_DIM:(h + 1) * HEAD_DIM] for h in _stack_order(k)],
                                 axis=0)
            logits = _dot(q4, kt[k * HEAD_DIM:(k + 1) * HEAD_DIM, keys]) + bias_ref[k]
            if jb == 0:
                logits = jnp.where(first_ok, logits, NEG_INF)
            if jb == nqb - 1:
                logits = jnp.where(last_ok, logits, NEG_INF)
            m = jnp.maximum(jnp.max(logits, axis=-1, keepdims=True), sink)
            p = jnp.exp(logits - m)
            den = jnp.sum(p, axis=-1, keepdims=True) + jnp.exp(sink - m)
            pb = p.astype(BF16)
            o_even = _dot(pb[:half_rows], vlo[keys]) / den[:half_rows]
            o_odd = _dot(pb[half_rows:], vhi[keys]) / den[half_rows:]
            for j in range(2):
                pair = o_even[j * Q_BLOCK:(j + 1) * Q_BLOCK] + o_odd[j * Q_BLOCK:(j + 1) * Q_BLOCK]
                pj = 2 * k + j
                o_ref[0, rows, pj * LANES:(pj + 1) * LANES] = pair.astype(o_ref.dtype)


def _window_attention(qb, kbt, vb, rel_bias, sink, tq):
    b, s, _ = qb.shape
    tq = min(tq, s)
    nb = s // Q_BLOCK
    per = tq // Q_BLOCK
    span = Q_BLOCK + 2 * WINDOW
    rel = np.arange(span)[None, :] - WINDOW - np.arange(Q_BLOCK)[:, None]
    band = np.abs(rel) <= WINDOW
    bias = rel_bias.astype(F32)[jnp.asarray(_t5_bucket_np(rel))]
    bias = jnp.where(jnp.asarray(band)[None], bias.transpose(2, 0, 1), NEG_INF)
    order = np.array([_stack_order(k) for k in range(B_KV_HEADS)])
    bias4 = bias[order].reshape(B_KV_HEADS, 4 * Q_BLOCK, span)
    sink4 = jnp.broadcast_to(sink.astype(F32)[order][:, :, None, None],
                             (B_KV_HEADS, 4, Q_BLOCK, LANES)).reshape(B_KV_HEADS, 4 * Q_BLOCK, LANES)
    prev = lambda i, t: jnp.maximum(t * per - 1, 0)
    nxt = lambda i, t: jnp.minimum((t + 1) * per, nb - 1)
    kt_edge = lambda f: pl.BlockSpec((1, KV_WIDTH, Q_BLOCK), lambda i, t: (i, 0, f(i, t)))
    v_edge = lambda f: pl.BlockSpec((1, Q_BLOCK, KV_WIDTH), lambda i, t: (i, f(i, t), 0))
    return pl.pallas_call(
        _wattn_kernel,
        grid=(b, s // tq),
        in_specs=[pl.BlockSpec((1, tq, B_WIDTH), lambda i, t: (i, t, 0)),
                  kt_edge(prev), pl.BlockSpec((1, KV_WIDTH, tq), lambda i, t: (i, 0, t)), kt_edge(nxt),
                  v_edge(prev), pl.BlockSpec((1, tq, KV_WIDTH), lambda i, t: (i, t, 0)), v_edge(nxt),
                  pl.BlockSpec((B_KV_HEADS, 4 * Q_BLOCK, span), lambda i, t: (0, 0, 0)),
                  pl.BlockSpec((B_KV_HEADS, 4 * Q_BLOCK, LANES), lambda i, t: (0, 0, 0))],
        out_specs=pl.BlockSpec((1, tq, B_WIDTH), lambda i, t: (i, t, 0)),
        out_shape=jax.ShapeDtypeStruct((b, s, B_WIDTH), BF16),
        compiler_params=_params(("arbitrary", "arbitrary")),
        name="window_attention",
    )(qb, kbt, kbt, kbt, vb, vb, vb, bias4, sink4)


def _pack_bf16_pairs(x):
    k = x.shape[1] // 2
    hi = pltpu.bitcast(x[:, :k].astype(BF16).astype(F32), U32)
    lo = pltpu.bitcast(x[:, k:].astype(BF16).astype(F32), U32)
    return hi | (lo >> 16)


def _unpack_bf16_pairs(w):
    hi = pltpu.bitcast(w & jnp.uint32(0xFFFF0000), F32)
    lo = pltpu.bitcast(w << 16, F32)
    return jnp.concatenate([hi, lo], axis=1).astype(BF16)


def _outproj_kernel(oa_ref, ob_ref, sga_ref, sgb_ref, x_ref, gt1_ref, sc2_ref, sh2_ref, g2_ref,
                    wa_ref, wb_ref, wo_ref, wr_ref, x1_ref, h2p_ref, afft_ref):
    a = _dot(oa_ref[0], wa_ref[...])
    b = _dot(ob_ref[0], wb_ref[...])
    merged = sga_ref[0].astype(F32) * a + sgb_ref[0].astype(F32) * b
    y = _dot(merged.astype(BF16), wo_ref[...])
    x1 = x_ref[0] + gt1_ref[0] * y
    x1_ref[0] = x1
    ms = jnp.mean(x1 * x1, axis=-1, keepdims=True)
    h2 = (x1 * lax.rsqrt(ms + EPS) * g2_ref[...]) * (1.0 + sc2_ref[0]) + sh2_ref[0]
    h2p_ref[...] = _pack_bf16_pairs(h2).reshape(h2p_ref.shape)
    logits = _dot(h2.astype(BF16), wr_ref[...])
    lane = lax.broadcasted_iota(I32, logits.shape, 1)
    logits = jnp.where(lane < N_EXPERTS, logits, -jnp.inf)
    m = jnp.max(logits, axis=-1, keepdims=True)
    e = jnp.exp(logits - m)
    aff = e / jnp.sum(e, axis=-1, keepdims=True)
    afft_ref[...] = aff.T[:N_EXPERTS, :]


def _out_projection(oa, ob, sga, sgb, x, gt1, sc2, sh2, g2, wa, wb, wo, wr, tm):
    b, s, d = x.shape
    nt = s // tm
    n = b * s
    tok = lambda w: pl.BlockSpec((1, tm, w), lambda i, t: (i, t, 0))
    vec = pl.BlockSpec((1, 1, d), lambda i, t: (i, 0, 0))
    full = lambda r, c: pl.BlockSpec((r, c), lambda i, t: (0, 0))
    sd = jax.ShapeDtypeStruct
    return pl.pallas_call(
        _outproj_kernel,
        grid=(b, nt),
        in_specs=[tok(A_WIDTH), tok(B_WIDTH), tok(d), tok(d), tok(d), vec, vec, vec, full(1, d),
                  full(A_WIDTH, d), full(B_WIDTH, d), full(d, d), full(d, LANES)],
        out_specs=[tok(d),
                   pl.BlockSpec((tm, 1, PACKED), lambda i, t: (i * nt + t, 0, 0)),
                   pl.BlockSpec((N_EXPERTS, tm), lambda i, t: (0, i * nt + t))],
        out_shape=[sd((b, s, d), F32), sd((n, 1, PACKED), U32), sd((N_EXPERTS, n), F32)],
        compiler_params=_params(("arbitrary", "arbitrary")),
        name="out_projection",
    )(oa, ob, sga, sgb, x, gt1, sc2, sh2, g2.reshape(1, d), wa, wb, wo, wr)


def _threshold_kernel(aff_ref, thr_ref, cut_ref, *, cap):
    bits = pltpu.bitcast(aff_ref[...], I32)
    n = bits.shape[1]

    def value_step(i, lo):
        cand = lo | jnp.left_shift(jnp.int32(1), 30 - i)
        cnt = jnp.sum((bits >= cand).astype(F32), axis=1, keepdims=True)
        return jnp.where(cnt >= cap, cand, lo)

    thr = lax.fori_loop(0, 31, value_step, jnp.zeros((N_EXPERTS, 1), I32))
    need = cap - jnp.sum((bits > thr).astype(F32), axis=1, keepdims=True)
    eq = bits == thr
    tpos = lax.broadcasted_iota(I32, (1, n), 1)
    nbits = max(n.bit_length(), 1)

    def index_step(i, cut):
        cand = cut + jnp.left_shift(jnp.int32(1), nbits - 1 - i)
        below = jnp.sum(jnp.where(eq & (tpos < cand), 1.0, 0.0), axis=1, keepdims=True)
        return jnp.where((below <= need) & (cand <= n), cand, cut)

    cut = lax.fori_loop(0, nbits, index_step, jnp.zeros((N_EXPERTS, 1), I32))
    thr_ref[...] = jnp.broadcast_to(thr, thr_ref.shape)
    cut_ref[...] = jnp.broadcast_to(cut, cut_ref.shape)


def _capacity_threshold(afft, cap):
    e, n = afft.shape
    out = pl.BlockSpec((e, LANES), lambda i: (0, 0))
    return pl.pallas_call(
        functools.partial(_threshold_kernel, cap=cap),
        grid=(1,),
        in_specs=[pl.BlockSpec((e, n), lambda i: (0, 0))],
        out_specs=[out, out],
        out_shape=[jax.ShapeDtypeStruct((e, LANES), I32)] * 2,
        compiler_params=_params(("arbitrary",)),
        name="capacity_threshold",
    )(afft)


_AUX_ROWS = 8


def _prefix_kernel(aff_ref, thr_ref, cut_ref, u_ref, ls_ref, posm_ref, slot_ref, tokrow_ref, offs_ref,
                   run_ref, *, tb, ch):
    i = pl.program_id(0)

    @pl.when(i == 0)
    def _():
        run_ref[...] = jnp.zeros_like(run_ref)

    bits = pltpu.bitcast(aff_ref[...], I32)
    tpos = i * tb + lax.broadcasted_iota(I32, (1, tb), 1)
    thr = thr_ref[:, 0:1]
    sel = (bits > thr) | ((bits == thr) & (tpos < cut_ref[:, 0:1]))
    s = jnp.where(sel, 1.0, 0.0)
    for j in range(tb // ch):
        sj = s[:, j * ch:(j + 1) * ch]
        cntj = jnp.sum(sj, axis=0, keepdims=True)
        x = jnp.concatenate([sj, cntj, jnp.zeros((_AUX_ROWS - 1, ch), F32)], axis=0).astype(BF16)
        run = run_ref[:, 0:1]
        incl = _dot(x, u_ref[...]) + run
        posm_ref[:, j * ch:(j + 1) * ch] = jnp.where(sj > 0.0, incl[:N_EXPERTS] - sj, -1.0)
        tok_end = incl[N_EXPERTS:N_EXPERTS + 1]
        tok_off = tok_end - cntj
        slot_ref[:, j * ch:(j + 1) * ch] = tok_off + _dot(ls_ref[...], sj.astype(BF16))
        tokrow_ref[:, j * ch:(j + 1) * ch] = jnp.concatenate(
            [tok_off, tok_end, jnp.zeros((6, ch), F32)], axis=0)
        offs_ref[j] = jnp.broadcast_to(run[:N_EXPERTS], (N_EXPERTS, LANES))
        run_ref[...] = jnp.broadcast_to(incl[:, ch - 1:ch], run_ref.shape)


def _routing_prefix(afft, thr, cut, tb, ch):
    e, n = afft.shape
    upper = jnp.asarray(np.triu(np.ones((ch, ch), np.float32)), BF16)
    lower_strict = jnp.asarray(np.tril(np.ones((e, e), np.float32), -1), BF16)
    rows = lambda r: pl.BlockSpec((r, tb), lambda i: (0, i))
    const = lambda r, c: pl.BlockSpec((r, c), lambda i: (0, 0))
    sd = jax.ShapeDtypeStruct
    return pl.pallas_call(
        functools.partial(_prefix_kernel, tb=tb, ch=ch),
        grid=(n // tb,),
        in_specs=[rows(e), const(e, LANES), const(e, LANES), const(ch, ch), const(e, e)],
        out_specs=[rows(e), rows(e), rows(8), pl.BlockSpec((tb // ch, e, LANES), lambda i: (i, 0, 0))],
        out_shape=[sd((e, n), F32), sd((e, n), F32), sd((8, n), F32), sd((n // ch, e, LANES), F32)],
        scratch_shapes=[pltpu.VMEM((e + _AUX_ROWS, LANES), F32)],
        compiler_params=_params(("arbitrary",)),
        name="routing_prefix",
    )(afft, thr, cut, upper, lower_strict)


def _compact_kernel(win_ref, aff_ref, posm_ref, slot_ref, list_ref, *, tb, ch, nc):
    i = pl.program_id(0)

    @pl.when(i == 0)
    def _():
        list_ref[...] = jnp.zeros_like(list_ref)

    rank = lax.broadcasted_iota(I32, (2 * ch, ch), 0).astype(F32)
    tok_local = lax.broadcasted_iota(I32, (1, ch), 1).astype(F32)

    def expert_body(e, carry):
        for j in range(tb // ch):
            c = i * (tb // ch) + j
            w0 = win_ref[e * nc + c]
            lanes = slice(j * ch, (j + 1) * ch)
            rel = posm_ref[pl.ds(e, 1), lanes] - (w0 * ch).astype(F32)
            onehot_t = jnp.where(rank == rel, 1.0, 0.0).astype(BF16)
            slot = slot_ref[pl.ds(e, 1), lanes].astype(I32)
            g = aff_ref[pl.ds(e, 1), lanes]
            g_hi = g.astype(BF16).astype(F32)
            g_mid = (g - g_hi).astype(BF16).astype(F32)
            g_lo = g - g_hi - g_mid
            vals = jnp.concatenate(
                [tok_local, jnp.full((1, ch), c, I32).astype(F32),
                 (slot & 255).astype(F32), ((slot >> 8) & 255).astype(F32), (slot >> 16).astype(F32),
                 g_hi, g_mid, g_lo], axis=0).astype(BF16)
            out = lax.dot_general(vals, onehot_t, (((1,), (1,)), ((), ())), preferred_element_type=F32)
            list_ref[e, w0] += out[:, :ch]
            list_ref[e, w0 + 1] += out[:, ch:]
        return carry

    lax.fori_loop(0, N_EXPERTS, expert_body, 0)


def _routing_compact(win, afft, posm, slot, cap, tb, ch):
    e, n = afft.shape
    nc = n // ch
    nwin = cap // ch
    rows = pl.BlockSpec((e, tb), lambda i, w: (0, i))
    grid_spec = pltpu.PrefetchScalarGridSpec(
        num_scalar_prefetch=1,
        grid=(n // tb,),
        in_specs=[rows, rows, rows],
        out_specs=pl.BlockSpec((e, nwin + 2, 8, ch), lambda i, w: (0, 0, 0, 0)),
    )
    return pl.pallas_call(
        functools.partial(_compact_kernel, tb=tb, ch=ch, nc=nc),
        grid_spec=grid_spec,
        out_shape=jax.ShapeDtypeStruct((e, nwin + 2, 8, ch), F32),
        compiler_params=_params(("arbitrary",)),
        name="routing_compact",
    )(win, afft, posm, slot)


def _routing(afft, cap, tm, tt, blk, tb=2048, ch=256):
    e, n = afft.shape
    assert n // ch <= 256 and tm == ch and e * cap < (1 << 24)
    tb = min(tb, n)
    thr, cut = _capacity_threshold(afft, cap)
    posm, slot, tokrow, offs = _routing_prefix(afft, thr, cut, tb, ch)
    win = (offs[:, :, 0].astype(I32) // ch).T.reshape(-1)
    lists = _routing_compact(win, afft, posm, slot, cap, tb, ch)[:, :cap // ch]
    nsteps = e * cap // tm
    idx = (lists[:, :, 1] * ch + lists[:, :, 0]).astype(I32).reshape(nsteps, tm)
    dst = (lists[:, :, 2] + lists[:, :, 3] * 256.0 + lists[:, :, 4] * 65536.0).astype(I32).reshape(nsteps, tm)
    gate = ((lists[:, :, 5] + lists[:, :, 6]) + lists[:, :, 7]).reshape(nsteps, 1, tm)
    meta = jnp.concatenate([idx, dst], axis=1)
    gate8 = jnp.broadcast_to(gate, (nsteps, 8, tm))

    npairs = e * cap
    ntile, nblk = n // tt, npairs // blk
    start = tokrow[0, ::tt].astype(I32)
    end = jnp.concatenate([start[1:], jnp.full((1,), npairs, I32)])
    b0 = jnp.minimum(start // blk, nblk - 1)
    b1 = jnp.where(end > start, (end - 1) // blk, b0)
    nb = b1 - b0 + 1
    wend = jnp.cumsum(nb)
    woff = wend - nb
    w = jnp.arange(nblk + ntile, dtype=I32)
    wt = jnp.minimum(jnp.sum((w[:, None] >= wend[None, :]).astype(I32), axis=1), ntile - 1)
    valid = (w < wend[-1]).astype(I32)
    wblk = jnp.where(valid == 1, b0[wt] + (w - woff[wt]), b1[ntile - 1]).astype(I32)
    return meta, gate8, tokrow, wt, wblk, valid


def _moe_kernel(meta_hbm, gate_ref, h2_hbm, wg_ref, wu_ref, wd_ref, z_hbm,
                meta_smem, xbuf, x2d, zbuf, sem_m, sem_g, sem_s, *, tm, nsteps):
    nt = pl.num_programs(1)
    s = pl.program_id(0) * nt + pl.program_id(1)
    slot = s % 2

    def meta_copy(step, mslot):
        return pltpu.make_async_copy(meta_hbm.at[step], meta_smem.at[mslot], sem_m)

    def issue_gather(mslot, bslot):
        for r in range(tm):
            tok = meta_smem[mslot, r]
            pltpu.make_async_copy(h2_hbm.at[tok], xbuf.at[bslot, r], sem_g.at[bslot]).start()

    def wait_gather(bslot):
        pltpu.make_async_copy(h2_hbm.at[pl.ds(0, tm)], xbuf.at[bslot], sem_g.at[bslot]).wait()

    def issue_scatter(mslot, bslot):
        for r in range(tm):
            dst = meta_smem[mslot, tm + r]
            pltpu.make_async_copy(zbuf.at[bslot, r], z_hbm.at[dst], sem_s.at[bslot]).start()

    def wait_scatter(bslot):
        pltpu.make_async_copy(zbuf.at[bslot], z_hbm.at[pl.ds(0, tm)], sem_s.at[bslot]).wait()

    @pl.when(s == 0)
    def _():
        c0 = meta_copy(0, 0)
        c0.start()
        c0.wait()
        if nsteps > 1:
            c1 = meta_copy(1, 1)
            c1.start()
            c1.wait()
        issue_gather(0, 0)

    @pl.when(s + 2 < nsteps)
    def _():
        meta_copy(s + 2, (s + 2) % 3).start()

    @pl.when(s + 1 < nsteps)
    def _():
        issue_gather((s + 1) % 3, 1 - slot)

    wait_gather(slot)
    x2d[...] = xbuf[slot].reshape(tm, PACKED)
    xe = _unpack_bf16_pairs(x2d[...])
    gate = _dot(xe, wg_ref[...])
    up = _dot(xe, wu_ref[...])
    hid = (gate * jax.nn.sigmoid(gate) * up).astype(BF16)
    ye = _dot(hid, wd_ref[...])
    z = ye * gate_ref[...].T[:, 0:1]

    @pl.when(s >= 2)
    def _():
        wait_scatter(slot)

    zbuf[slot] = _pack_bf16_pairs(z).reshape(tm, 1, PACKED)
    issue_scatter(s % 3, slot)

    @pl.when(s + 2 < nsteps)
    def _():
        meta_copy(s + 2, (s + 2) % 3).wait()

    @pl.when(s == nsteps - 1)
    def _():
        if nsteps > 1:
            wait_scatter(1 - slot)
        wait_scatter(slot)


def _expert_mlp(meta, gate8, h2p, wg, wu, wd, tm):
    nsteps = meta.shape[0]
    nt = nsteps // N_EXPERTS
    d, f = wg.shape[1], wg.shape[2]
    step = lambda e, i: (e * nt + i, 0, 0)
    return pl.pallas_call(
        functools.partial(_moe_kernel, tm=tm, nsteps=nsteps),
        grid=(N_EXPERTS, nt),
        in_specs=[pl.BlockSpec(memory_space=pl.ANY),
                  pl.BlockSpec((None, 8, tm), step),
                  pl.BlockSpec(memory_space=pl.ANY),
                  pl.BlockSpec((None, d, f), lambda e, i: (e, 0, 0)),
                  pl.BlockSpec((None, d, f), lambda e, i: (e, 0, 0)),
                  pl.BlockSpec((None, f, d), lambda e, i: (e, 0, 0))],
        out_specs=pl.BlockSpec(memory_space=pl.ANY),
        out_shape=jax.ShapeDtypeStruct((nsteps * tm, 1, PACKED), U32),
        scratch_shapes=[pltpu.SMEM((3, 2 * tm), I32),
                        pltpu.VMEM((2, tm, 1, PACKED), U32),
                        pltpu.VMEM((tm, PACKED), U32),
                        pltpu.VMEM((2, tm, 1, PACKED), U32),
                        pltpu.SemaphoreType.DMA,
                        pltpu.SemaphoreType.DMA((2,)),
                        pltpu.SemaphoreType.DMA((2,))],
        compiler_params=_params(("arbitrary", "arbitrary")),
        name="expert_mlp",
    )(meta, gate8, h2p, wg, wu, wd)


def _combine_kernel(wtile_ref, wblk_ref, wvalid_ref, z_ref, tokrow_ref, x1_ref, gt2_ref, gf_ref,
                    y_ref, acc_ref, z2d, *, tt):
    w = pl.program_id(0)
    nw = pl.num_programs(0)
    tile = wtile_ref[w]
    first = jnp.logical_or(w == 0, wtile_ref[jnp.maximum(w - 1, 0)] != tile)
    last = jnp.logical_or(w == nw - 1, wtile_ref[jnp.minimum(w + 1, nw - 1)] != tile)

    @pl.when(first)
    def _():
        acc_ref[...] = jnp.zeros_like(acc_ref)

    @pl.when(wvalid_ref[w] == 1)
    def _():
        blk = z2d.shape[0]
        z2d[...] = z_ref[...].reshape(z2d.shape)
        zrows = _unpack_bf16_pairs(z2d[...])
        pair = (wblk_ref[w] * blk + lax.broadcasted_iota(I32, (blk, tt), 0)).astype(F32)
        owned = (pair >= tokrow_ref[0:1, :]) & (pair < tokrow_ref[1:2, :])
        onehot = jnp.where(owned, 1.0, 0.0).T.astype(BF16)
        acc_ref[...] += _dot(onehot, zrows)

    @pl.when(last)
    def _():
        x2 = x1_ref[...] + gt2_ref[0] * acc_ref[...]
        ms = jnp.mean(x2 * x2, axis=-1, keepdims=True)
        y_ref[...] = x2 * lax.rsqrt(ms + EPS) * gf_ref[...]


def _combine(wtile, wblk, wvalid, z, tokrow, x1, gt2, gf, s, tt, blk):
    n, d = x1.shape
    nw = wtile.shape[0]
    grid_spec = pltpu.PrefetchScalarGridSpec(
        num_scalar_prefetch=3,
        grid=(nw,),
        in_specs=[pl.BlockSpec((blk, 1, PACKED), lambda w, wt, wb, wv: (wb[w], 0, 0)),
                  pl.BlockSpec((8, tt), lambda w, wt, wb, wv: (0, wt[w])),
                  pl.BlockSpec((tt, d), lambda w, wt, wb, wv: (wt[w], 0)),
                  pl.BlockSpec((1, 1, d), lambda w, wt, wb, wv: (wt[w] * tt // s, 0, 0)),
                  pl.BlockSpec((1, d), lambda w, wt, wb, wv: (0, 0))],
        out_specs=pl.BlockSpec((tt, d), lambda w, wt, wb, wv: (wt[w], 0)),
        scratch_shapes=[pltpu.VMEM((tt, d), F32), pltpu.VMEM((blk, PACKED), U32)],
    )
    return pl.pallas_call(
        functools.partial(_combine_kernel, tt=tt),
        grid_spec=grid_spec,
        out_shape=jax.ShapeDtypeStruct((n, d), F32),
        compiler_params=_params(("arbitrary",)),
        name="combine_final_norm",
    )(wtile, wblk, wvalid, z, tokrow, x1, gt2, gf.reshape(1, d))


def _trunk(x, mod, w, tm_proj=512, tq=512, tu=256, tq_win=512, tm_moe=256, tt=512, blk=512):
    b, s, d = x.shape
    n = b * s
    cap = CAPACITY_FACTOR * n // N_EXPERTS
    sh1, sc1, gt1, sh2, sc2, gt2 = [m.reshape(b, 1, d) for m in jnp.split(mod, N_MOD, axis=-1)]

    qa, kat, va, qb, kbt, vb, sga, sgb = _in_projection(
        x, sc1, sh1, w["norm1_g"], w["w_in"], w["q_norm_g"], w["k_norm_g"], tm_proj)
    oa = _global_attention(qa, kat, va, tq, tu)
    ob = _window_attention(qb, kbt, vb, w["rel_bias"], w["sink"], tq_win)
    x1, h2p, afft = _out_projection(oa, ob, sga, sgb, x, gt1, sc2, sh2, w["norm2_g"],
                                    w["w_branch_a"], w["w_branch_b"], w["w_out"], w["w_router"], tm_proj)
    meta, gate8, tokrow, wtile, wblk, wvalid = _routing(afft, cap, tm_moe, tt, blk, ch=tm_moe)
    z = _expert_mlp(meta, gate8, h2p, w["w_e_gate"], w["w_e_up"], w["w_e_down"], tm_moe)
    y = _combine(wtile, wblk, wvalid, z, tokrow, x1.reshape(n, d), gt2, w["norm_f_g"], s, tt, blk)
    return y.reshape(b, s, d)


def kernel(x_prompt, x_sample, c_prompt, c_sample, w_ada, b_ada, norm1_g, w_in, q_norm_g, k_norm_g, sink,
           w_branch_a, w_branch_b, w_out, norm2_g, w_router, w_e_gate, w_e_up, w_e_down, rel_bias, norm_f_g):
    assert w_ada.shape[0] == 1, "single layer"
    bp, bs = c_prompt.shape[0], c_sample.shape[0]
    rows = -(-(bp + bs) // 16) * 16
    c = jnp.concatenate([c_prompt, c_sample, jnp.zeros((rows - bp - bs, D_MODEL), F32)], axis=0)
    mod = _modulation(c, w_ada[0], b_ada[0])

    router = jnp.zeros((D_MODEL, LANES), BF16).at[:, :N_EXPERTS].set(w_router[0].astype(BF16))
    w = dict(norm1_g=norm1_g[0], w_in=w_in[0].astype(BF16), q_norm_g=q_norm_g[0], k_norm_g=k_norm_g[0],
             sink=sink[0], w_branch_a=w_branch_a[0].astype(BF16), w_branch_b=w_branch_b[0].astype(BF16),
             w_out=w_out[0].astype(BF16), norm2_g=norm2_g[0], w_router=router,
             w_e_gate=w_e_gate[0].astype(BF16), w_e_up=w_e_up[0].astype(BF16),
             w_e_down=w_e_down[0].astype(BF16), rel_bias=rel_bias, norm_f_g=norm_f_g)
    y_prompt = _trunk(x_prompt, mod[:bp], w)
    y_sample = _trunk(x_sample, mod[bp:bp + bs], w)
    return (y_prompt, y_sample)
```

```python
import functools
import math

import jax
import jax.numpy as jnp
import numpy as np
from jax import lax
from jax.experimental import pallas as pl
from jax.experimental.pallas import tpu as pltpu

F32 = jnp.float32
BF16 = jnp.bfloat16
I32 = jnp.int32
U32 = jnp.uint32

D_MODEL = 1024
HEAD_DIM = 64
A_HEADS = 8
A_KV_HEADS = 2
B_HEADS = 8
B_KV_HEADS = 2
A_WIDTH = A_HEADS * HEAD_DIM
B_WIDTH = B_HEADS * HEAD_DIM
KV_WIDTH = A_KV_HEADS * HEAD_DIM
Q_BLOCK = 128
WINDOW = 128
GRID_W = 64
ROPE_THETA = 10000.0
AXIS_DIM = HEAD_DIM // 2
N_BUCKETS = 32
MAX_DISTANCE = 128
N_EXPERTS = 16
CAPACITY_FACTOR = 2
D_FF = 2048
N_MOD = 6
EPS = 1e-6
NEG_INF = -1e30
IN_COLS = A_WIDTH + 2 * KV_WIDTH + B_WIDTH + 2 * KV_WIDTH + 2 * D_MODEL

LANES = 128
PACKED = D_MODEL // 2
VMEM_LIMIT = 56 * 1024 * 1024
LOGITS_VMEM_BUDGET = 16 * 1024 * 1024

_QA, _KA, _VA = 0, A_WIDTH, A_WIDTH + KV_WIDTH
_QB = A_WIDTH + 2 * KV_WIDTH
_KB, _VB = _QB + B_WIDTH, _QB + B_WIDTH + KV_WIDTH
_GA = _QB + B_WIDTH + 2 * KV_WIDTH
_GB = _GA + D_MODEL


def _params(sem, vmem=VMEM_LIMIT):
    return pltpu.CompilerParams(dimension_semantics=sem, vmem_limit_bytes=vmem)


def _dot(a, b):
    return jnp.dot(a, b, preferred_element_type=F32)


def _mod_kernel(c_ref, w_ref, b_ref, o_ref):
    c = c_ref[...]
    s = c * jax.nn.sigmoid(c)
    o_ref[...] = _dot(s.astype(BF16), w_ref[...].astype(BF16)) + b_ref[...]


def _modulation(c, w_ada, b_ada):
    bp, d = c.shape
    n = w_ada.shape[1]
    tn = 1536
    return pl.pallas_call(
        _mod_kernel,
        grid=(n // tn,),
        in_specs=[pl.BlockSpec((bp, d), lambda j: (0, 0)),
                  pl.BlockSpec((d, tn), lambda j: (0, j)),
                  pl.BlockSpec((1, tn), lambda j: (0, j))],
        out_specs=pl.BlockSpec((bp, tn), lambda j: (0, j)),
        out_shape=jax.ShapeDtypeStruct((bp, n), F32),
        compiler_params=_params(("arbitrary",)),
        name="modulation",
    )(c, w_ada, b_ada.reshape(1, n))


def _swap16(x):
    n = x.shape[-1]
    left = pltpu.roll(x, n - 16, axis=1)
    right = pltpu.roll(x, 16, axis=1)
    lane = lax.broadcasted_iota(I32, x.shape, 1)
    return jnp.where((lane % 32) < 16, left, right)


def _inproj_kernel(x_ref, sc_ref, sh_ref, g1_ref, w_ref, aq_ref, bq_ref, ak_ref, bk_ref, hm_ref,
                   qa_ref, kat_ref, va_ref, qb_ref, kbt_ref, vb_ref, sga_ref, sgb_ref):
    x = x_ref[0]
    ms = jnp.mean(x * x, axis=-1, keepdims=True)
    xn = x * lax.rsqrt(ms + EPS) * g1_ref[...]
    h = (xn * (1.0 + sc_ref[0]) + sh_ref[0]).astype(BF16)

    def proj(lo, width):
        return _dot(h, w_ref[:, lo:lo + width])

    def head_rms(q):
        pieces = []
        for j in range(q.shape[1] // LANES):
            qq = q[:, j * LANES:(j + 1) * LANES]
            pieces.append(_dot((qq * qq).astype(BF16), hm_ref[...]))
        msq = pieces[0] if len(pieces) == 1 else jnp.concatenate(pieces, axis=1)
        return q * lax.rsqrt(msq + EPS)

    def rope(qh, a_ref, b_ref):
        reps = qh.shape[1] // LANES
        a = a_ref[...]
        b = b_ref[...]
        if reps > 1:
            a = jnp.concatenate([a] * reps, axis=1)
            b = jnp.concatenate([b] * reps, axis=1)
        return qh * a + _swap16(qh) * b

    qa = rope(head_rms(proj(_QA, A_WIDTH)), aq_ref, bq_ref)
    qa_ref[0] = qa.astype(BF16)
    ka = rope(head_rms(proj(_KA, KV_WIDTH)), ak_ref, bk_ref)
    kat_ref[0] = ka.T.astype(BF16)
    va_ref[0] = proj(_VA, KV_WIDTH).astype(BF16)
    qb_ref[0] = (proj(_QB, B_WIDTH) * (HEAD_DIM ** -0.5)).astype(BF16)
    kbt_ref[0] = proj(_KB, KV_WIDTH).T.astype(BF16)
    vb_ref[0] = proj(_VB, KV_WIDTH).astype(BF16)
    sga_ref[0] = jax.nn.sigmoid(proj(_GA, D_MODEL)).astype(BF16)
    sgb_ref[0] = jax.nn.sigmoid(proj(_GB, D_MODEL)).astype(BF16)


def _rope_tables(s, gain, scale):
    pos = jnp.arange(s, dtype=I32)
    row = (pos // GRID_W).astype(F32)
    col = (pos % GRID_W).astype(F32)
    inv_freq = ROPE_THETA ** (-jnp.arange(0, AXIS_DIM, 2, dtype=F32) / AXIS_DIM)
    ang_r = row[:, None] * inv_freq
    ang_c = col[:, None] * inv_freq
    cos64 = jnp.concatenate([jnp.cos(ang_r)] * 2 + [jnp.cos(ang_c)] * 2, axis=1)
    sin64 = jnp.concatenate([-jnp.sin(ang_r), jnp.sin(ang_r), -jnp.sin(ang_c), jnp.sin(ang_c)], axis=1)
    g = gain.astype(F32)
    gp = jnp.concatenate([g[16:32], g[0:16], g[48:64], g[32:48]])
    a = cos64 * g[None, :] * scale
    b = sin64 * gp[None, :] * scale
    return jnp.concatenate([a, a], axis=1), jnp.concatenate([b, b], axis=1)


def _in_projection(x, sc1, sh1, g1, w_in, q_gain, k_gain, tm):
    b, s, d = x.shape
    aq, bq = _rope_tables(s, q_gain, HEAD_DIM ** -0.5 * math.log2(math.e))
    ak, bk = _rope_tables(s, k_gain, 1.0)
    head_mean = jnp.asarray(np.kron(np.eye(2), np.full((HEAD_DIM, HEAD_DIM), 1.0 / HEAD_DIM)), BF16)
    tok = lambda w: pl.BlockSpec((1, tm, w), lambda i, t: (i, t, 0))
    tr = pl.BlockSpec((1, KV_WIDTH, tm), lambda i, t: (i, 0, t))
    vec = pl.BlockSpec((1, 1, d), lambda i, t: (i, 0, 0))
    tab = pl.BlockSpec((tm, LANES), lambda i, t: (t, 0))
    sd = jax.ShapeDtypeStruct
    return pl.pallas_call(
        _inproj_kernel,
        grid=(b, s // tm),
        in_specs=[tok(d), vec, vec,
                  pl.BlockSpec((1, d), lambda i, t: (0, 0)),
                  pl.BlockSpec((d, IN_COLS), lambda i, t: (0, 0)),
                  tab, tab, tab, tab,
                  pl.BlockSpec((LANES, LANES), lambda i, t: (0, 0))],
        out_specs=[tok(A_WIDTH), tr, tok(KV_WIDTH), tok(B_WIDTH), tr, tok(KV_WIDTH), tok(d), tok(d)],
        out_shape=[sd((b, s, A_WIDTH), BF16), sd((b, KV_WIDTH, s), BF16), sd((b, s, KV_WIDTH), BF16),
                   sd((b, s, B_WIDTH), BF16), sd((b, KV_WIDTH, s), BF16), sd((b, s, KV_WIDTH), BF16),
                   sd((b, s, d), BF16), sd((b, s, d), BF16)],
        compiler_params=_params(("arbitrary", "arbitrary")),
        name="in_projection",
    )(x, sc1, sh1, g1.reshape(1, d), w_in, aq, bq, ak, bk, head_mean)


def _half_lane_variants(v, k_is_one):
    lane = lax.broadcasted_iota(I32, v.shape, 1)
    vr = pltpu.roll(v, HEAD_DIM, axis=1)
    own_lo = jnp.where(k_is_one, vr, v)
    own_hi = jnp.where(k_is_one, v, vr)
    lo = jnp.where(lane < HEAD_DIM, own_lo, 0.0)
    hi = jnp.where(lane >= HEAD_DIM, own_hi, 0.0)
    return lo, hi


_SUM_LANE_LO = LANES - 1
_SUM_LANE_HI = 0


def _gattn_kernel(q_ref, kt_ref, v_ref, o_ref, vv_ref, s_ref, qs_ref, os_ref, *, tu):
    k = pl.program_id(1)
    qi = pl.program_id(2)
    tq = q_ref.shape[1]
    group = A_HEADS // A_KV_HEADS
    nu = group * (tq // tu)

    @pl.when(qi == 0)
    def _():
        lo, hi = _half_lane_variants(v_ref[0].astype(F32), k == 1)
        lane = lax.broadcasted_iota(I32, lo.shape, 1)
        vv_ref[0] = jnp.where(lane == _SUM_LANE_LO, 1.0, lo).astype(BF16)
        vv_ref[1] = jnp.where(lane == _SUM_LANE_HI, 1.0, hi).astype(BF16)

    for u in range(nu):
        r, g = divmod(u, group)
        qs_ref[u] = q_ref[0, r * tu:(r + 1) * tu, g * HEAD_DIM:(g + 1) * HEAD_DIM]

    upb = s_ref.shape[0]

    def units(i, carry):
        maxes = []
        for j in range(upb):
            sc = _dot(qs_ref[upb * i + j], kt_ref[0])
            s_ref[j] = sc
            maxes.append(jnp.max(sc, axis=-1, keepdims=True))
        for j in range(upb):
            p = jnp.exp2(s_ref[j] - maxes[j])
            os_ref[upb * i + j] = _dot(p.astype(BF16), vv_ref[j % 2])
        return carry

    lax.fori_loop(0, nu // upb, units, 0)

    lane_o = lax.broadcasted_iota(I32, (tu, LANES), 1)
    for r in range(tq // tu):
        for j in range(group // 2):
            even = os_ref[r * group + 2 * j]
            odd = os_ref[r * group + 2 * j + 1]
            o = (jnp.where(lane_o < HEAD_DIM, even, 0.0) / even[:, _SUM_LANE_LO:_SUM_LANE_LO + 1]
                 + jnp.where(lane_o >= HEAD_DIM, odd, 0.0) / odd[:, _SUM_LANE_HI:_SUM_LANE_HI + 1])
            o_ref[0, r * tu:(r + 1) * tu, j * LANES:(j + 1) * LANES] = o.astype(o_ref.dtype)


def _global_attention(qa, kat, va, tq, tu):
    b, s, _ = qa.shape
    gw = A_WIDTH // A_KV_HEADS
    tq = min(tq, s)
    tu = min(tu, tq)
    nu = gw // HEAD_DIM * (tq // tu)
    upb = max(2, min(nu, LOGITS_VMEM_BUDGET // (tu * s * 4)))
    while nu % upb or upb % 2:
        upb -= 1
    return pl.pallas_call(
        functools.partial(_gattn_kernel, tu=tu),
        grid=(b, A_KV_HEADS, s // tq),
        in_specs=[pl.BlockSpec((1, tq, gw), lambda i, k, t: (i, t, k)),
                  pl.BlockSpec((1, HEAD_DIM, s), lambda i, k, t: (i, k, 0)),
                  pl.BlockSpec((1, s, KV_WIDTH), lambda i, k, t: (i, 0, 0))],
        out_specs=pl.BlockSpec((1, tq, gw), lambda i, k, t: (i, t, k)),
        out_shape=jax.ShapeDtypeStruct((b, s, A_WIDTH), BF16),
        scratch_shapes=[pltpu.VMEM((2, s, KV_WIDTH), BF16), pltpu.VMEM((upb, tu, s), F32),
                        pltpu.VMEM((nu, tu, HEAD_DIM), BF16), pltpu.VMEM((nu, tu, LANES), F32)],
        compiler_params=_params(("arbitrary", "arbitrary", "arbitrary")),
        name="global_attention",
    )(qa, kat, va)


def _t5_bucket_np(rel):
    half = N_BUCKETS // 2
    max_exact = half // 2
    base = (rel > 0).astype(np.int32) * half
    n = np.abs(rel)
    large = max_exact + (np.log(np.maximum(n, 1).astype(np.float32) / max_exact)
                         / math.log(MAX_DISTANCE / max_exact) * (half - max_exact)).astype(np.int32)
    large = np.minimum(large, half - 1)
    return base + np.where(n < max_exact, n, large)


def _stack_order(k):
    group = B_HEADS // B_KV_HEADS
    return [k * group + g for g in (0, 2, 1, 3)]


def _wattn_kernel(q_ref, ktp_ref, ktc_ref, ktn_ref, vp_ref, vc_ref, vn_ref, bias_ref, sink_ref, o_ref):
    t = pl.program_id(1)
    nt = pl.num_programs(1)
    span = Q_BLOCK + 2 * WINDOW
    nqb = q_ref.shape[1] // Q_BLOCK
    kt = jnp.concatenate([ktp_ref[0], ktc_ref[0], ktn_ref[0]], axis=1)
    v = jnp.concatenate([vp_ref[0], vc_ref[0], vn_ref[0]], axis=0).astype(F32)
    col = lax.broadcasted_iota(I32, (1, span), 1)
    first_ok = (col >= WINDOW) | (t > 0)
    last_ok = (col < WINDOW + Q_BLOCK) | (t < nt - 1)
    half_rows = 2 * Q_BLOCK
    values = []
    for k in range(B_KV_HEADS):
        lo, hi = _half_lane_variants(v, k == 1)
        values.append((lo.astype(BF16), hi.astype(BF16)))
    sinks = [sink_ref[k][:, 0:1] for k in range(B_KV_HEADS)]
    items = [(k, jb) for k in range(B_KV_HEADS) for jb in range(nqb)]
    keys = lambda jb: slice(jb * Q_BLOCK, jb * Q_BLOCK + span)
    rows = lambda jb: slice(jb * Q_BLOCK, (jb + 1) * Q_BLOCK)

    logits = []
    for k, jb in items:
        q4 = jnp.concatenate([q_ref[0, rows(jb), h * HEAD_DIM:(h + 1) * HEAD_DIM] for h in _stack_order(k)],
                             axis=0)
        lg = _dot(q4, kt[k * HEAD_DIM:(k + 1) * HEAD_DIM, keys(jb)]) + bias_ref[k]
        if jb == 0:
            lg = jnp.where(first_ok, lg, NEG_INF)
        if jb == nqb - 1:
            lg = jnp.where(last_ok, lg, NEG_INF)
        logits.append(lg)
    maxes = [jnp.maximum(jnp.max(lg, axis=-1, keepdims=True), sinks[k]) for (k, _), lg in zip(items, logits)]
    probs = [jnp.exp(lg - m) for lg, m in zip(logits, maxes)]
    dens = [jnp.sum(p, axis=-1, keepdims=True) + jnp.exp(sinks[k] - m)
            for (k, _), p, m in zip(items, probs, maxes)]
    outs = []
    for (k, jb), p, den in zip(items, probs, dens):
        pb = p.astype(BF16)
        vlo, vhi = values[k]
        o_even = _dot(pb[:half_rows], vlo[keys(jb)]) / den[:half_rows]
        o_odd = _dot(pb[half_rows:], vhi[keys(jb)]) / den[half_rows:]
        outs.append((o_even, o_odd))
    for (k, jb), (o_even, o_odd) in zip(items, outs):
        for j in range(2):
            pair = o_even[j * Q_BLOCK:(j + 1) * Q_BLOCK] + o_odd[j * Q_BLOCK:(j + 1) * Q_BLOCK]
            pj = 2 * k + j
            o_ref[0, rows(jb), pj * LANES:(pj + 1) * LANES] = pair.astype(o_ref.dtype)


def _window_attention(qb, kbt, vb, rel_bias, sink, tq):
    b, s, _ = qb.shape
    tq = min(tq, s)
    nb = s // Q_BLOCK
    per = tq // Q_BLOCK
    span = Q_BLOCK + 2 * WINDOW
    rel = np.arange(span)[None, :] - WINDOW - np.arange(Q_BLOCK)[:, None]
    band = np.abs(rel) <= WINDOW
    bias = rel_bias.astype(F32)[jnp.asarray(_t5_bucket_np(rel))]
    bias = jnp.where(jnp.asarray(band)[None], bias.transpose(2, 0, 1), NEG_INF)
    order = np.array([_stack_order(k) for k in range(B_KV_HEADS)])
    bias4 = bias[order].reshape(B_KV_HEADS, 4 * Q_BLOCK, span)
    sink4 = jnp.broadcast_to(sink.astype(F32)[order][:, :, None, None],
                             (B_KV_HEADS, 4, Q_BLOCK, LANES)).reshape(B_KV_HEADS, 4 * Q_BLOCK, LANES)
    prev = lambda i, t: jnp.maximum(t * per - 1, 0)
    nxt = lambda i, t: jnp.minimum((t + 1) * per, nb - 1)
    kt_edge = lambda f: pl.BlockSpec((1, KV_WIDTH, Q_BLOCK), lambda i, t: (i, 0, f(i, t)))
    v_edge = lambda f: pl.BlockSpec((1, Q_BLOCK, KV_WIDTH), lambda i, t: (i, f(i, t), 0))
    return pl.pallas_call(
        _wattn_kernel,
        grid=(b, s // tq),
        in_specs=[pl.BlockSpec((1, tq, B_WIDTH), lambda i, t: (i, t, 0)),
                  kt_edge(prev), pl.BlockSpec((1, KV_WIDTH, tq), lambda i, t: (i, 0, t)), kt_edge(nxt),
                  v_edge(prev), pl.BlockSpec((1, tq, KV_WIDTH), lambda i, t: (i, t, 0)), v_edge(nxt),
                  pl.BlockSpec((B_KV_HEADS, 4 * Q_BLOCK, span), lambda i, t: (0, 0, 0)),
                  pl.BlockSpec((B_KV_HEADS, 4 * Q_BLOCK, LANES), lambda i, t: (0, 0, 0))],
        out_specs=pl.BlockSpec((1, tq, B_WIDTH), lambda i, t: (i, t, 0)),
        out_shape=jax.ShapeDtypeStruct((b, s, B_WIDTH), BF16),
        compiler_params=_params(("arbitrary", "arbitrary")),
        name="window_attention",
    )(qb, kbt, kbt, kbt, vb, vb, vb, bias4, sink4)


def _pack_bf16_pairs(x):
    k = x.shape[1] // 2
    hi = pltpu.bitcast(x[:, :k].astype(BF16).astype(F32), U32)
    lo = pltpu.bitcast(x[:, k:].astype(BF16).astype(F32), U32)
    return hi | (lo >> 16)


def _unpack_bf16_pairs(w):
    hi = pltpu.bitcast(w & jnp.uint32(0xFFFF0000), F32)
    lo = pltpu.bitcast(w << 16, F32)
    return jnp.concatenate([hi, lo], axis=1).astype(BF16)


def _outproj_kernel(oa_ref, ob_ref, sga_ref, sgb_ref, x_ref, gt1_ref, sc2_ref, sh2_ref, g2_ref,
                    wa_ref, wb_ref, wo_ref, wr_ref, x1_ref, h2p_ref, afft_ref):
    a = _dot(oa_ref[0], wa_ref[...])
    b = _dot(ob_ref[0], wb_ref[...])
    merged = sga_ref[0].astype(F32) * a + sgb_ref[0].astype(F32) * b
    y = _dot(merged.astype(BF16), wo_ref[...])
    x1 = x_ref[0] + gt1_ref[0] * y
    x1_ref[0] = x1
    ms = jnp.mean(x1 * x1, axis=-1, keepdims=True)
    h2 = (x1 * lax.rsqrt(ms + EPS) * g2_ref[...]) * (1.0 + sc2_ref[0]) + sh2_ref[0]
    h2p_ref[...] = _pack_bf16_pairs(h2).reshape(h2p_ref.shape)
    logits = _dot(h2.astype(BF16), wr_ref[...])
    lane = lax.broadcasted_iota(I32, logits.shape, 1)
    logits = jnp.where(lane < N_EXPERTS, logits, -jnp.inf)
    m = jnp.max(logits, axis=-1, keepdims=True)
    e = jnp.exp(logits - m)
    aff = e / jnp.sum(e, axis=-1, keepdims=True)
    afft_ref[...] = aff.T[:N_EXPERTS, :]


def _out_projection(oa, ob, sga, sgb, x, gt1, sc2, sh2, g2, wa, wb, wo, wr, tm):
    b, s, d = x.shape
    nt = s // tm
    n = b * s
    tok = lambda w: pl.BlockSpec((1, tm, w), lambda i, t: (i, t, 0))
    vec = pl.BlockSpec((1, 1, d), lambda i, t: (i, 0, 0))
    full = lambda r, c: pl.BlockSpec((r, c), lambda i, t: (0, 0))
    sd = jax.ShapeDtypeStruct
    return pl.pallas_call(
        _outproj_kernel,
        grid=(b, nt),
        in_specs=[tok(A_WIDTH), tok(B_WIDTH), tok(d), tok(d), tok(d), vec, vec, vec, full(1, d),
                  full(A_WIDTH, d), full(B_WIDTH, d), full(d, d), full(d, LANES)],
        out_specs=[tok(d),
                   pl.BlockSpec((tm, 1, PACKED), lambda i, t: (i * nt + t, 0, 0)),
                   pl.BlockSpec((N_EXPERTS, tm), lambda i, t: (0, i * nt + t))],
        out_shape=[sd((b, s, d), F32), sd((n, 1, PACKED), U32), sd((N_EXPERTS, n), F32)],
        compiler_params=_params(("arbitrary", "arbitrary")),
        name="out_projection",
    )(oa, ob, sga, sgb, x, gt1, sc2, sh2, g2.reshape(1, d), wa, wb, wo, wr)


def _threshold_kernel(aff_ref, thr_ref, cut_ref, *, cap):
    bits = pltpu.bitcast(aff_ref[...], I32)
    n = bits.shape[1]

    def value_step(i, lo):
        cand = lo | jnp.left_shift(jnp.int32(1), 30 - i)
        cnt = jnp.sum((bits >= cand).astype(F32), axis=1, keepdims=True)
        return jnp.where(cnt >= cap, cand, lo)

    thr = lax.fori_loop(0, 31, value_step, jnp.zeros((N_EXPERTS, 1), I32))
    need = cap - jnp.sum((bits > thr).astype(F32), axis=1, keepdims=True)
    eq = bits == thr
    tpos = lax.broadcasted_iota(I32, (1, n), 1)
    nbits = max(n.bit_length(), 1)

    def index_step(i, cut):
        cand = cut + jnp.left_shift(jnp.int32(1), nbits - 1 - i)
        below = jnp.sum(jnp.where(eq & (tpos < cand), 1.0, 0.0), axis=1, keepdims=True)
        return jnp.where((below <= need) & (cand <= n), cand, cut)

    cut = lax.fori_loop(0, nbits, index_step, jnp.zeros((N_EXPERTS, 1), I32))
    thr_ref[...] = jnp.broadcast_to(thr, thr_ref.shape)
    cut_ref[...] = jnp.broadcast_to(cut, cut_ref.shape)


def _capacity_threshold(afft, cap):
    e, n = afft.shape
    out = pl.BlockSpec((e, LANES), lambda i: (0, 0))
    return pl.pallas_call(
        functools.partial(_threshold_kernel, cap=cap),
        grid=(1,),
        in_specs=[pl.BlockSpec((e, n), lambda i: (0, 0))],
        out_specs=[out, out],
        out_shape=[jax.ShapeDtypeStruct((e, LANES), I32)] * 2,
        compiler_params=_params(("arbitrary",)),
        name="capacity_threshold",
    )(afft)


_AUX_ROWS = 8


def _prefix_kernel(aff_ref, thr_ref, cut_ref, u_ref, ls_ref, posm_ref, slot_ref, tokrow_ref, offs_ref,
                   run_ref, *, tb, ch):
    i = pl.program_id(0)

    @pl.when(i == 0)
    def _():
        run_ref[...] = jnp.zeros_like(run_ref)

    bits = pltpu.bitcast(aff_ref[...], I32)
    tpos = i * tb + lax.broadcasted_iota(I32, (1, tb), 1)
    thr = thr_ref[:, 0:1]
    sel = (bits > thr) | ((bits == thr) & (tpos < cut_ref[:, 0:1]))
    s = jnp.where(sel, 1.0, 0.0)
    for j in range(tb // ch):
        sj = s[:, j * ch:(j + 1) * ch]
        cntj = jnp.sum(sj, axis=0, keepdims=True)
        x = jnp.concatenate([sj, cntj, jnp.zeros((_AUX_ROWS - 1, ch), F32)], axis=0).astype(BF16)
        run = run_ref[:, 0:1]
        incl = _dot(x, u_ref[...]) + run
        posm_ref[:, j * ch:(j + 1) * ch] = jnp.where(sj > 0.0, incl[:N_EXPERTS] - sj, -1.0)
        tok_end = incl[N_EXPERTS:N_EXPERTS + 1]
        tok_off = tok_end - cntj
        slot_ref[:, j * ch:(j + 1) * ch] = tok_off + _dot(ls_ref[...], sj.astype(BF16))
        tokrow_ref[:, j * ch:(j + 1) * ch] = jnp.concatenate(
            [tok_off, tok_end, jnp.zeros((6, ch), F32)], axis=0)
        offs_ref[j] = jnp.broadcast_to(run[:N_EXPERTS], (N_EXPERTS, LANES))
        run_ref[...] = jnp.broadcast_to(incl[:, ch - 1:ch], run_ref.shape)


def _routing_prefix(afft, thr, cut, tb, ch):
    e, n = afft.shape
    upper = jnp.asarray(np.triu(np.ones((ch, ch), np.float32)), BF16)
    lower_strict = jnp.asarray(np.tril(np.ones((e, e), np.float32), -1), BF16)
    rows = lambda r: pl.BlockSpec((r, tb), lambda i: (0, i))
    const = lambda r, c: pl.BlockSpec((r, c), lambda i: (0, 0))
    sd = jax.ShapeDtypeStruct
    return pl.pallas_call(
        functools.partial(_prefix_kernel, tb=tb, ch=ch),
        grid=(n // tb,),
        in_specs=[rows(e), const(e, LANES), const(e, LANES), const(ch, ch), const(e, e)],
        out_specs=[rows(e), rows(e), rows(8), pl.BlockSpec((tb // ch, e, LANES), lambda i: (i, 0, 0))],
        out_shape=[sd((e, n), F32), sd((e, n), F32), sd((8, n), F32), sd((n // ch, e, LANES), F32)],
        scratch_shapes=[pltpu.VMEM((e + _AUX_ROWS, LANES), F32)],
        compiler_params=_params(("arbitrary",)),
        name="routing_prefix",
    )(afft, thr, cut, upper, lower_strict)


def _compact_kernel(win_ref, aff_ref, posm_ref, slot_ref, list_ref, *, tb, ch, nc):
    i = pl.program_id(0)

    @pl.when(i == 0)
    def _():
        list_ref[...] = jnp.zeros_like(list_ref)

    rank = lax.broadcasted_iota(I32, (2 * ch, ch), 0).astype(F32)
    tok_local = lax.broadcasted_iota(I32, (1, ch), 1).astype(F32)

    def expert_body(e, carry):
        for j in range(tb // ch):
            c = i * (tb // ch) + j
            w0 = win_ref[e * nc + c]
            lanes = slice(j * ch, (j + 1) * ch)
            rel = posm_ref[pl.ds(e, 1), lanes] - (w0 * ch).astype(F32)
            onehot_t = jnp.where(rank == rel, 1.0, 0.0).astype(BF16)
            slot = slot_ref[pl.ds(e, 1), lanes].astype(I32)
            g = aff_ref[pl.ds(e, 1), lanes]
            g_hi = g.astype(BF16).astype(F32)
            g_mid = (g - g_hi).astype(BF16).astype(F32)
            g_lo = g - g_hi - g_mid
            vals = jnp.concatenate(
                [tok_local, jnp.full((1, ch), c, I32).astype(F32),
                 (slot & 255).astype(F32), ((slot >> 8) & 255).astype(F32), (slot >> 16).astype(F32),
                 g_hi, g_mid, g_lo], axis=0).astype(BF16)
            out = lax.dot_general(vals, onehot_t, (((1,), (1,)), ((), ())), preferred_element_type=F32)
            list_ref[e, w0] += out[:, :ch]
            list_ref[e, w0 + 1] += out[:, ch:]
        return carry

    lax.fori_loop(0, N_EXPERTS, expert_body, 0)


def _routing_compact(win, afft, posm, slot, cap, tb, ch):
    e, n = afft.shape
    nc = n // ch
    nwin = cap // ch
    rows = pl.BlockSpec((e, tb), lambda i, w: (0, i))
    grid_spec = pltpu.PrefetchScalarGridSpec(
        num_scalar_prefetch=1,
        grid=(n // tb,),
        in_specs=[rows, rows, rows],
        out_specs=pl.BlockSpec((e, nwin + 2, 8, ch), lambda i, w: (0, 0, 0, 0)),
    )
    return pl.pallas_call(
        functools.partial(_compact_kernel, tb=tb, ch=ch, nc=nc),
        grid_spec=grid_spec,
        out_shape=jax.ShapeDtypeStruct((e, nwin + 2, 8, ch), F32),
        compiler_params=_params(("arbitrary",)),
        name="routing_compact",
    )(win, afft, posm, slot)


def _routing(afft, cap, tm, tt, blk, tb=2048, ch=256):
    e, n = afft.shape
    assert n // ch <= 256 and tm == ch and e * cap < (1 << 24)
    tb = min(tb, n)
    thr, cut = _capacity_threshold(afft, cap)
    posm, slot, tokrow, offs = _routing_prefix(afft, thr, cut, tb, ch)
    win = (offs[:, :, 0].astype(I32) // ch).T.reshape(-1)
    lists = _routing_compact(win, afft, posm, slot, cap, tb, ch)[:, :cap // ch]
    nsteps = e * cap // tm
    idx = (lists[:, :, 1] * ch + lists[:, :, 0]).astype(I32).reshape(nsteps, tm)
    dst = (lists[:, :, 2] + lists[:, :, 3] * 256.0 + lists[:, :, 4] * 65536.0).astype(I32).reshape(nsteps, tm)
    gate = ((lists[:, :, 5] + lists[:, :, 6]) + lists[:, :, 7]).reshape(nsteps, 1, tm)
    meta = jnp.concatenate([idx, dst], axis=1)
    gate8 = jnp.broadcast_to(gate, (nsteps, 8, tm))

    npairs = e * cap
    ntile, nblk = n // tt, npairs // blk
    start = tokrow[0, ::tt].astype(I32)
    end = jnp.concatenate([start[1:], jnp.full((1,), npairs, I32)])
    b0 = jnp.minimum(start // blk, nblk - 1)
    b1 = jnp.where(end > start, (end - 1) // blk, b0)
    nb = b1 - b0 + 1
    wend = jnp.cumsum(nb)
    woff = wend - nb
    w = jnp.arange(nblk + ntile, dtype=I32)
    wt = jnp.minimum(jnp.sum((w[:, None] >= wend[None, :]).astype(I32), axis=1), ntile - 1)
    valid = (w < wend[-1]).astype(I32)
    wblk = jnp.where(valid == 1, b0[wt] + (w - woff[wt]), b1[ntile - 1]).astype(I32)
    return meta, gate8, tokrow, wt, wblk, valid


def _moe_kernel(meta_hbm, gate_ref, h2_hbm, wg_ref, wu_ref, wd_ref, z_hbm,
                meta_smem, xbuf, x2d, zbuf, sem_m, sem_g, sem_s, *, tm, nsteps):
    nt = pl.num_programs(1)
    s = pl.program_id(0) * nt + pl.program_id(1)
    slot = s % 2

    def meta_copy(step, mslot):
        return pltpu.make_async_copy(meta_hbm.at[step], meta_smem.at[mslot], sem_m)

    def issue_gather(mslot, bslot):
        for r in range(tm):
            tok = meta_smem[mslot, r]
            pltpu.make_async_copy(h2_hbm.at[tok], xbuf.at[bslot, r], sem_g.at[bslot]).start()

    def wait_gather(bslot):
        pltpu.make_async_copy(h2_hbm.at[pl.ds(0, tm)], xbuf.at[bslot], sem_g.at[bslot]).wait()

    def issue_scatter(mslot, bslot):
        for r in range(tm):
            dst = meta_smem[mslot, tm + r]
            pltpu.make_async_copy(zbuf.at[bslot, r], z_hbm.at[dst], sem_s.at[bslot]).start()

    def wait_scatter(bslot):
        pltpu.make_async_copy(zbuf.at[bslot], z_hbm.at[pl.ds(0, tm)], sem_s.at[bslot]).wait()

    @pl.when(s == 0)
    def _():
        c0 = meta_copy(0, 0)
        c0.start()
        c0.wait()
        if nsteps > 1:
            c1 = meta_copy(1, 1)
            c1.start()
            c1.wait()
        issue_gather(0, 0)

    @pl.when(s + 2 < nsteps)
    def _():
        meta_copy(s + 2, (s + 2) % 3).start()

    @pl.when(s + 1 < nsteps)
    def _():
        issue_gather((s + 1) % 3, 1 - slot)

    wait_gather(slot)
    x2d[...] = xbuf[slot].reshape(tm, PACKED)
    xe = _unpack_bf16_pairs(x2d[...])
    gate = _dot(xe, wg_ref[...])
    up = _dot(xe, wu_ref[...])
    hid = (gate * jax.nn.sigmoid(gate) * up).astype(BF16)
    ye = _dot(hid, wd_ref[...])
    z = ye * gate_ref[...].T[:, 0:1]

    @pl.when(s >= 2)
    def _():
        wait_scatter(slot)

    zbuf[slot] = _pack_bf16_pairs(z).reshape(tm, 1, PACKED)
    issue_scatter(s % 3, slot)

    @pl.when(s + 2 < nsteps)
    def _():
        meta_copy(s + 2, (s + 2) % 3).wait()

    @pl.when(s == nsteps - 1)
    def _():
        if nsteps > 1:
            wait_scatter(1 - slot)
        wait_scatter(slot)


def _expert_mlp(meta, gate8, h2p, wg, wu, wd, tm):
    nsteps = meta.shape[0]
    nt = nsteps // N_EXPERTS
    d, f = wg.shape[1], wg.shape[2]
    step = lambda e, i: (e * nt + i, 0, 0)
    return pl.pallas_call(
        functools.partial(_moe_kernel, tm=tm, nsteps=nsteps),
        grid=(N_EXPERTS, nt),
        in_specs=[pl.BlockSpec(memory_space=pl.ANY),
                  pl.BlockSpec((None, 8, tm), step),
                  pl.BlockSpec(memory_space=pl.ANY),
                  pl.BlockSpec((None, d, f), lambda e, i: (e, 0, 0)),
                  pl.BlockSpec((None, d, f), lambda e, i: (e, 0, 0)),
                  pl.BlockSpec((None, f, d), lambda e, i: (e, 0, 0))],
        out_specs=pl.BlockSpec(memory_space=pl.ANY),
        out_shape=jax.ShapeDtypeStruct((nsteps * tm, 1, PACKED), U32),
        scratch_shapes=[pltpu.SMEM((3, 2 * tm), I32),
                        pltpu.VMEM((2, tm, 1, PACKED), U32),
                        pltpu.VMEM((tm, PACKED), U32),
                        pltpu.VMEM((2, tm, 1, PACKED), U32),
                        pltpu.SemaphoreType.DMA,
                        pltpu.SemaphoreType.DMA((2,)),
                        pltpu.SemaphoreType.DMA((2,))],
        compiler_params=_params(("arbitrary", "arbitrary")),
        name="expert_mlp",
    )(meta, gate8, h2p, wg, wu, wd)


def _combine_kernel(wtile_ref, wblk_ref, wvalid_ref, z_ref, tokrow_ref, x1_ref, gt2_ref, gf_ref,
                    y_ref, acc_ref, z2d, *, tt):
    w = pl.program_id(0)
    nw = pl.num_programs(0)
    tile = wtile_ref[w]
    first = jnp.logical_or(w == 0, wtile_ref[jnp.maximum(w - 1, 0)] != tile)
    last = jnp.logical_or(w == nw - 1, wtile_ref[jnp.minimum(w + 1, nw - 1)] != tile)

    @pl.when(first)
    def _():
        acc_ref[...] = jnp.zeros_like(acc_ref)

    @pl.when(wvalid_ref[w] == 1)
    def _():
        blk = z2d.shape[0]
        z2d[...] = z_ref[...].reshape(z2d.shape)
        zrows = _unpack_bf16_pairs(z2d[...])
        pair = (wblk_ref[w] * blk + lax.broadcasted_iota(I32, (blk, tt), 0)).astype(F32)
        owned = (pair >= tokrow_ref[0:1, :]) & (pair < tokrow_ref[1:2, :])
        onehot = jnp.where(owned, 1.0, 0.0).T.astype(BF16)
        acc_ref[...] += _dot(onehot, zrows)

    @pl.when(last)
    def _():
        x2 = x1_ref[...] + gt2_ref[0] * acc_ref[...]
        ms = jnp.mean(x2 * x2, axis=-1, keepdims=True)
        y_ref[...] = x2 * lax.rsqrt(ms + EPS) * gf_ref[...]


def _combine(wtile, wblk, wvalid, z, tokrow, x1, gt2, gf, s, tt, blk):
    n, d = x1.shape
    nw = wtile.shape[0]
    grid_spec = pltpu.PrefetchScalarGridSpec(
        num_scalar_prefetch=3,
        grid=(nw,),
        in_specs=[pl.BlockSpec((blk, 1, PACKED), lambda w, wt, wb, wv: (wb[w], 0, 0)),
                  pl.BlockSpec((8, tt), lambda w, wt, wb, wv: (0, wt[w])),
                  pl.BlockSpec((tt, d), lambda w, wt, wb, wv: (wt[w], 0)),
                  pl.BlockSpec((1, 1, d), lambda w, wt, wb, wv: (wt[w] * tt // s, 0, 0)),
                  pl.BlockSpec((1, d), lambda w, wt, wb, wv: (0, 0))],
        out_specs=pl.BlockSpec((tt, d), lambda w, wt, wb, wv: (wt[w], 0)),
        scratch_shapes=[pltpu.VMEM((tt, d), F32), pltpu.VMEM((blk, PACKED), U32)],
    )
    return pl.pallas_call(
        functools.partial(_combine_kernel, tt=tt),
        grid_spec=grid_spec,
        out_shape=jax.ShapeDtypeStruct((n, d), F32),
        compiler_params=_params(("arbitrary",)),
        name="combine_final_norm",
    )(wtile, wblk, wvalid, z, tokrow, x1, gt2, gf.reshape(1, d))


def _trunk(x, mod, w, tm_proj=512, tq=512, tu=256, tq_win=512, tm_moe=256, tt=512, blk=512):
    b, s, d = x.shape
    n = b * s
    cap = CAPACITY_FACTOR * n // N_EXPERTS
    sh1, sc1, gt1, sh2, sc2, gt2 = [m.reshape(b, 1, d) for m in jnp.split(mod, N_MOD, axis=-1)]

    qa, kat, va, qb, kbt, vb, sga, sgb = _in_projection(
        x, sc1, sh1, w["norm1_g"], w["w_in"], w["q_norm_g"], w["k_norm_g"], tm_proj)
    oa = _global_attention(qa, kat, va, tq, tu)
    ob = _window_attention(qb, kbt, vb, w["rel_bias"], w["sink"], tq_win)
    x1, h2p, afft = _out_projection(oa, ob, sga, sgb, x, gt1, sc2, sh2, w["norm2_g"],
                                    w["w_branch_a"], w["w_branch_b"], w["w_out"], w["w_router"], tm_proj)
    meta, gate8, tokrow, wtile, wblk, wvalid = _routing(afft, cap, tm_moe, tt, blk, ch=tm_moe)
    z = _expert_mlp(meta, gate8, h2p, w["w_e_gate"], w["w_e_up"], w["w_e_down"], tm_moe)
    y = _combine(wtile, wblk, wvalid, z, tokrow, x1.reshape(n, d), gt2, w["norm_f_g"], s, tt, blk)
    return y.reshape(b, s, d)


def kernel(x_prompt, x_sample, c_prompt, c_sample, w_ada, b_ada, norm1_g, w_in, q_norm_g, k_norm_g, sink,
           w_branch_a, w_branch_b, w_out, norm2_g, w_router, w_e_gate, w_e_up, w_e_down, rel_bias, norm_f_g):
    assert w_ada.shape[0] == 1, "single layer"
    bp, bs = c_prompt.shape[0], c_sample.shape[0]
    rows = -(-(bp + bs) // 16) * 16
    c = jnp.concatenate([c_prompt, c_sample, jnp.zeros((rows - bp - bs, D_MODEL), F32)], axis=0)
    mod = _modulation(c, w_ada[0], b_ada[0])

    router = jnp.zeros((D_MODEL, LANES), BF16).at[:, :N_EXPERTS].set(w_router[0].astype(BF16))
    w = dict(norm1_g=norm1_g[0], w_in=w_in[0].astype(BF16), q_norm_g=q_norm_g[0], k_norm_g=k_norm_g[0],
             sink=sink[0], w_branch_a=w_branch_a[0].astype(BF16), w_branch_b=w_branch_b[0].astype(BF16),
             w_out=w_out[0].astype(BF16), norm2_g=norm2_g[0], w_router=router,
             w_e_gate=w_e_gate[0].astype(BF16), w_e_up=w_e_up[0].astype(BF16),
             w_e_down=w_e_down[0].astype(BF16), rel_bias=rel_bias, norm_f_g=norm_f_g)
    y_prompt = _trunk(x_prompt, mod[:bp], w)
    y_sample = _trunk(x_sample, mod[bp:bp + bs], w)
    return (y_prompt, y_sample)
```

```python
import functools
import math

import jax
import jax.numpy as jnp
import numpy as np
from jax import lax
from jax.experimental import pallas as pl
from jax.experimental.pallas import tpu as pltpu

F32 = jnp.float32
BF16 = jnp.bfloat16
I32 = jnp.int32
U32 = jnp.uint32

D_MODEL = 1024
HEAD_DIM = 64
A_HEADS = 8
A_KV_HEADS = 2
B_HEADS = 8
B_KV_HEADS = 2
A_WIDTH = A_HEADS * HEAD_DIM
B_WIDTH = B_HEADS * HEAD_DIM
KV_WIDTH = A_KV_HEADS * HEAD_DIM
Q_BLOCK = 128
WINDOW = 128
GRID_W = 64
ROPE_THETA = 10000.0
AXIS_DIM = HEAD_DIM // 2
N_BUCKETS = 32
MAX_DISTANCE = 128
N_EXPERTS = 16
CAPACITY_FACTOR = 2
D_FF = 2048
N_MOD = 6
EPS = 1e-6
NEG_INF = -1e30
IN_COLS = A_WIDTH + 2 * KV_WIDTH + B_WIDTH + 2 * KV_WIDTH + 2 * D_MODEL

LANES = 128
PACKED = D_MODEL // 2
VMEM_LIMIT = 56 * 1024 * 1024
LOGITS_VMEM_BUDGET = 16 * 1024 * 1024

_QA, _KA, _VA = 0, A_WIDTH, A_WIDTH + KV_WIDTH
_QB = A_WIDTH + 2 * KV_WIDTH
_KB, _VB = _QB + B_WIDTH, _QB + B_WIDTH + KV_WIDTH
_GA = _QB + B_WIDTH + 2 * KV_WIDTH
_GB = _GA + D_MODEL


def _params(sem, vmem=VMEM_LIMIT):
    return pltpu.CompilerParams(dimension_semantics=sem, vmem_limit_bytes=vmem)


def _dot(a, b):
    return jnp.dot(a, b, preferred_element_type=F32)


def _mod_kernel(c_ref, w_ref, b_ref, o_ref):
    c = c_ref[...]
    s = c * jax.nn.sigmoid(c)
    o_ref[...] = _dot(s.astype(BF16), w_ref[...].astype(BF16)) + b_ref[...]


def _modulation(c, w_ada, b_ada):
    bp, d = c.shape
    n = w_ada.shape[1]
    tn = 1536
    return pl.pallas_call(
        _mod_kernel,
        grid=(n // tn,),
        in_specs=[pl.BlockSpec((bp, d), lambda j: (0, 0)),
                  pl.BlockSpec((d, tn), lambda j: (0, j)),
                  pl.BlockSpec((1, tn), lambda j: (0, j))],
        out_specs=pl.BlockSpec((bp, tn), lambda j: (0, j)),
        out_shape=jax.ShapeDtypeStruct((bp, n), F32),
        compiler_params=_params(("arbitrary",)),
        name="modulation",
    )(c, w_ada, b_ada.reshape(1, n))


def _swap16(x):
    n = x.shape[-1]
    left = pltpu.roll(x, n - 16, axis=1)
    right = pltpu.roll(x, 16, axis=1)
    lane = lax.broadcasted_iota(I32, x.shape, 1)
    return jnp.where((lane % 32) < 16, left, right)


def _inproj_kernel(x_ref, sc_ref, sh_ref, g1_ref, w_ref, aq_ref, bq_ref, ak_ref, bk_ref, hm_ref,
                   qa_ref, kat_ref, va_ref, qb_ref, kbt_ref, vb_ref, sga_ref, sgb_ref):
    x = x_ref[0]
    ms = jnp.mean(x * x, axis=-1, keepdims=True)
    xn = x * lax.rsqrt(ms + EPS) * g1_ref[...]
    h = (xn * (1.0 + sc_ref[0]) + sh_ref[0]).astype(BF16)

    def proj(lo, width):
        return _dot(h, w_ref[:, lo:lo + width])

    def head_rms(q):
        pieces = []
        for j in range(q.shape[1] // LANES):
            qq = q[:, j * LANES:(j + 1) * LANES]
            pieces.append(_dot((qq * qq).astype(BF16), hm_ref[...]))
        msq = pieces[0] if len(pieces) == 1 else jnp.concatenate(pieces, axis=1)
        return q * lax.rsqrt(msq + EPS)

    def rope(qh, a_ref, b_ref):
        reps = qh.shape[1] // LANES
        a = a_ref[...]
        b = b_ref[...]
        if reps > 1:
            a = jnp.concatenate([a] * reps, axis=1)
            b = jnp.concatenate([b] * reps, axis=1)
        return qh * a + _swap16(qh) * b

    qa = rope(head_rms(proj(_QA, A_WIDTH)), aq_ref, bq_ref)
    qa_ref[0] = qa.astype(BF16)
    ka = rope(head_rms(proj(_KA, KV_WIDTH)), ak_ref, bk_ref)
    kat_ref[0] = ka.T.astype(BF16)
    va_ref[0] = proj(_VA, KV_WIDTH).astype(BF16)
    qb_ref[0] = (proj(_QB, B_WIDTH) * (HEAD_DIM ** -0.5)).astype(BF16)
    kbt_ref[0] = proj(_KB, KV_WIDTH).T.astype(BF16)
    vb_ref[0] = proj(_VB, KV_WIDTH).astype(BF16)
    sga_ref[0] = jax.nn.sigmoid(proj(_GA, D_MODEL)).astype(BF16)
    sgb_ref[0] = jax.nn.sigmoid(proj(_GB, D_MODEL)).astype(BF16)


def _rope_tables(s, gain, scale):
    pos = jnp.arange(s, dtype=I32)
    row = (pos // GRID_W).astype(F32)
    col = (pos % GRID_W).astype(F32)
    inv_freq = ROPE_THETA ** (-jnp.arange(0, AXIS_DIM, 2, dtype=F32) / AXIS_DIM)
    ang_r = row[:, None] * inv_freq
    ang_c = col[:, None] * inv_freq
    cos64 = jnp.concatenate([jnp.cos(ang_r)] * 2 + [jnp.cos(ang_c)] * 2, axis=1)
    sin64 = jnp.concatenate([-jnp.sin(ang_r), jnp.sin(ang_r), -jnp.sin(ang_c), jnp.sin(ang_c)], axis=1)
    g = gain.astype(F32)
    gp = jnp.concatenate([g[16:32], g[0:16], g[48:64], g[32:48]])
    a = cos64 * g[None, :] * scale
    b = sin64 * gp[None, :] * scale
    return jnp.concatenate([a, a], axis=1), jnp.concatenate([b, b], axis=1)


def _in_projection(x, sc1, sh1, g1, w_in, q_gain, k_gain, tm):
    b, s, d = x.shape
    aq, bq = _rope_tables(s, q_gain, HEAD_DIM ** -0.5 * math.log2(math.e))
    ak, bk = _rope_tables(s, k_gain, 1.0)
    head_mean = jnp.asarray(np.kron(np.eye(2), np.full((HEAD_DIM, HEAD_DIM), 1.0 / HEAD_DIM)), BF16)
    tok = lambda w: pl.BlockSpec((1, tm, w), lambda i, t: (i, t, 0))
    tr = pl.BlockSpec((1, KV_WIDTH, tm), lambda i, t: (i, 0, t))
    vec = pl.BlockSpec((1, 1, d), lambda i, t: (i, 0, 0))
    tab = pl.BlockSpec((tm, LANES), lambda i, t: (t, 0))
    sd = jax.ShapeDtypeStruct
    return pl.pallas_call(
        _inproj_kernel,
        grid=(b, s // tm),
        in_specs=[tok(d), vec, vec,
                  pl.BlockSpec((1, d), lambda i, t: (0, 0)),
                  pl.BlockSpec((d, IN_COLS), lambda i, t: (0, 0)),
                  tab, tab, tab, tab,
                  pl.BlockSpec((LANES, LANES), lambda i, t: (0, 0))],
        out_specs=[tok(A_WIDTH), tr, tok(KV_WIDTH), tok(B_WIDTH), tr, tok(KV_WIDTH), tok(d), tok(d)],
        out_shape=[sd((b, s, A_WIDTH), BF16), sd((b, KV_WIDTH, s), BF16), sd((b, s, KV_WIDTH), BF16),
                   sd((b, s, B_WIDTH), BF16), sd((b, KV_WIDTH, s), BF16), sd((b, s, KV_WIDTH), BF16),
                   sd((b, s, d), BF16), sd((b, s, d), BF16)],
        compiler_params=_params(("arbitrary", "arbitrary")),
        name="in_projection",
    )(x, sc1, sh1, g1.reshape(1, d), w_in, aq, bq, ak, bk, head_mean)


def _half_lane_variants(v, k_is_one):
    lane = lax.broadcasted_iota(I32, v.shape, 1)
    vr = pltpu.roll(v, HEAD_DIM, axis=1)
    own_lo = jnp.where(k_is_one, vr, v)
    own_hi = jnp.where(k_is_one, v, vr)
    lo = jnp.where(lane < HEAD_DIM, own_lo, 0.0)
    hi = jnp.where(lane >= HEAD_DIM, own_hi, 0.0)
    return lo, hi


_SUM_LANE_LO = LANES - 1
_SUM_LANE_HI = 0


def _gattn_kernel(q_ref, kt_ref, v_ref, o_ref, vv_ref, s_ref, qs_ref, os_ref, *, tu):
    k = pl.program_id(1)
    qi = pl.program_id(2)
    tq = q_ref.shape[1]
    group = A_HEADS // A_KV_HEADS
    nu = group * (tq // tu)

    @pl.when(qi == 0)
    def _():
        lo, hi = _half_lane_variants(v_ref[0].astype(F32), k == 1)
        lane = lax.broadcasted_iota(I32, lo.shape, 1)
        vv_ref[0] = jnp.where(lane == _SUM_LANE_LO, 1.0, lo).astype(BF16)
        vv_ref[1] = jnp.where(lane == _SUM_LANE_HI, 1.0, hi).astype(BF16)

    for u in range(nu):
        r, g = divmod(u, group)
        qs_ref[u] = q_ref[0, r * tu:(r + 1) * tu, g * HEAD_DIM:(g + 1) * HEAD_DIM]

    upb = s_ref.shape[0]

    def units(i, carry):
        maxes = []
        for j in range(upb):
            sc = _dot(qs_ref[upb * i + j], kt_ref[0])
            s_ref[j] = sc
            maxes.append(jnp.max(sc, axis=-1, keepdims=True))
        for j in range(upb):
            p = jnp.exp2(s_ref[j] - maxes[j])
            os_ref[upb * i + j] = _dot(p.astype(BF16), vv_ref[j % 2])
        return carry

    lax.fori_loop(0, nu // upb, units, 0)

    lane_o = lax.broadcasted_iota(I32, (tu, LANES), 1)
    for r in range(tq // tu):
        for j in range(group // 2):
            even = os_ref[r * group + 2 * j]
            odd = os_ref[r * group + 2 * j + 1]
            o = (jnp.where(lane_o < HEAD_DIM, even, 0.0) / even[:, _SUM_LANE_LO:_SUM_LANE_LO + 1]
                 + jnp.where(lane_o >= HEAD_DIM, odd, 0.0) / odd[:, _SUM_LANE_HI:_SUM_LANE_HI + 1])
            o_ref[0, r * tu:(r + 1) * tu, j * LANES:(j + 1) * LANES] = o.astype(o_ref.dtype)


def _global_attention(qa, kat, va, tq, tu):
    b, s, _ = qa.shape
    gw = A_WIDTH // A_KV_HEADS
    tq = min(tq, s)
    tu = min(tu, tq)
    nu = gw // HEAD_DIM * (tq // tu)
    upb = max(2, min(nu, LOGITS_VMEM_BUDGET // (tu * s * 4)))
    while nu % upb or upb % 2:
        upb -= 1
    return pl.pallas_call(
        functools.partial(_gattn_kernel, tu=tu),
        grid=(b, A_KV_HEADS, s // tq),
        in_specs=[pl.BlockSpec((1, tq, gw), lambda i, k, t: (i, t, k)),
                  pl.BlockSpec((1, HEAD_DIM, s), lambda i, k, t: (i, k, 0)),
                  pl.BlockSpec((1, s, KV_WIDTH), lambda i, k, t: (i, 0, 0))],
        out_specs=pl.BlockSpec((1, tq, gw), lambda i, k, t: (i, t, k)),
        out_shape=jax.ShapeDtypeStruct((b, s, A_WIDTH), BF16),
        scratch_shapes=[pltpu.VMEM((2, s, KV_WIDTH), BF16), pltpu.VMEM((upb, tu, s), F32),
                        pltpu.VMEM((nu, tu, HEAD_DIM), BF16), pltpu.VMEM((nu, tu, LANES), F32)],
        compiler_params=_params(("arbitrary", "arbitrary", "arbitrary")),
        name="global_attention",
    )(qa, kat, va)


def _t5_bucket_np(rel):
    half = N_BUCKETS // 2
    max_exact = half // 2
    base = (rel > 0).astype(np.int32) * half
    n = np.abs(rel)
    large = max_exact + (np.log(np.maximum(n, 1).astype(np.float32) / max_exact)
                         / math.log(MAX_DISTANCE / max_exact) * (half - max_exact)).astype(np.int32)
    large = np.minimum(large, half - 1)
    return base + np.where(n < max_exact, n, large)


def _stack_order(k):
    group = B_HEADS // B_KV_HEADS
    return [k * group + g for g in (0, 2, 1, 3)]


def _wattn_kernel(q_ref, ktp_ref, ktc_ref, ktn_ref, vp_ref, vc_ref, vn_ref, bias_ref, sink_ref, o_ref):
    t = pl.program_id(1)
    nt = pl.num_programs(1)
    span = Q_BLOCK + 2 * WINDOW
    nqb = q_ref.shape[1] // Q_BLOCK
    kt = jnp.concatenate([ktp_ref[0], ktc_ref[0], ktn_ref[0]], axis=1)
    v = jnp.concatenate([vp_ref[0], vc_ref[0], vn_ref[0]], axis=0).astype(F32)
    col = lax.broadcasted_iota(I32, (1, span), 1)
    first_ok = (col >= WINDOW) | (t > 0)
    last_ok = (col < WINDOW + Q_BLOCK) | (t < nt - 1)
    half_rows = 2 * Q_BLOCK
    values = []
    for k in range(B_KV_HEADS):
        lo, hi = _half_lane_variants(v, k == 1)
        values.append((lo.astype(BF16), hi.astype(BF16)))
    sinks = [sink_ref[k][:, 0:1] for k in range(B_KV_HEADS)]
    items = [(k, jb) for k in range(B_KV_HEADS) for jb in range(nqb)]
    keys = lambda jb: slice(jb * Q_BLOCK, jb * Q_BLOCK + span)
    rows = lambda jb: slice(jb * Q_BLOCK, (jb + 1) * Q_BLOCK)

    logits = []
    for k, jb in items:
        q4 = jnp.concatenate([q_ref[0, rows(jb), h * HEAD_DIM:(h + 1) * HEAD_DIM] for h in _stack_order(k)],
                             axis=0)
        lg = _dot(q4, kt[k * HEAD_DIM:(k + 1) * HEAD_DIM, keys(jb)]) + bias_ref[k]
        if jb == 0:
            lg = jnp.where(first_ok, lg, NEG_INF)
        if jb == nqb - 1:
            lg = jnp.where(last_ok, lg, NEG_INF)
        logits.append(lg)
    maxes = [jnp.maximum(jnp.max(lg, axis=-1, keepdims=True), sinks[k]) for (k, _), lg in zip(items, logits)]
    probs = [jnp.exp(lg - m) for lg, m in zip(logits, maxes)]
    dens = [jnp.sum(p, axis=-1, keepdims=True) + jnp.exp(sinks[k] - m)
            for (k, _), p, m in zip(items, probs, maxes)]
    outs = []
    for (k, jb), p, den in zip(items, probs, dens):
        pb = p.astype(BF16)
        vlo, vhi = values[k]
        o_even = _dot(pb[:half_rows], vlo[keys(jb)]) / den[:half_rows]
        o_odd = _dot(pb[half_rows:], vhi[keys(jb)]) / den[half_rows:]
        outs.append((o_even, o_odd))
    for (k, jb), (o_even, o_odd) in zip(items, outs):
        for j in range(2):
            pair = o_even[j * Q_BLOCK:(j + 1) * Q_BLOCK] + o_odd[j * Q_BLOCK:(j + 1) * Q_BLOCK]
            pj = 2 * k + j
            o_ref[0, rows(jb), pj * LANES:(pj + 1) * LANES] = pair.astype(o_ref.dtype)


def _window_attention(qb, kbt, vb, rel_bias, sink, tq):
    b, s, _ = qb.shape
    tq = min(tq, s)
    nb = s // Q_BLOCK
    per = tq // Q_BLOCK
    span = Q_BLOCK + 2 * WINDOW
    rel = np.arange(span)[None, :] - WINDOW - np.arange(Q_BLOCK)[:, None]
    band = np.abs(rel) <= WINDOW
    bias = rel_bias.astype(F32)[jnp.asarray(_t5_bucket_np(rel))]
    bias = jnp.where(jnp.asarray(band)[None], bias.transpose(2, 0, 1), NEG_INF)
    order = np.array([_stack_order(k) for k in range(B_KV_HEADS)])
    bias4 = bias[order].reshape(B_KV_HEADS, 4 * Q_BLOCK, span)
    sink4 = jnp.broadcast_to(sink.astype(F32)[order][:, :, None, None],
                             (B_KV_HEADS, 4, Q_BLOCK, LANES)).reshape(B_KV_HEADS, 4 * Q_BLOCK, LANES)
    prev = lambda i, t: jnp.maximum(t * per - 1, 0)
    nxt = lambda i, t: jnp.minimum((t + 1) * per, nb - 1)
    kt_edge = lambda f: pl.BlockSpec((1, KV_WIDTH, Q_BLOCK), lambda i, t: (i, 0, f(i, t)))
    v_edge = lambda f: pl.BlockSpec((1, Q_BLOCK, KV_WIDTH), lambda i, t: (i, f(i, t), 0))
    return pl.pallas_call(
        _wattn_kernel,
        grid=(b, s // tq),
        in_specs=[pl.BlockSpec((1, tq, B_WIDTH), lambda i, t: (i, t, 0)),
                  kt_edge(prev), pl.BlockSpec((1, KV_WIDTH, tq), lambda i, t: (i, 0, t)), kt_edge(nxt),
                  v_edge(prev), pl.BlockSpec((1, tq, KV_WIDTH), lambda i, t: (i, t, 0)), v_edge(nxt),
                  pl.BlockSpec((B_KV_HEADS, 4 * Q_BLOCK, span), lambda i, t: (0, 0, 0)),
                  pl.BlockSpec((B_KV_HEADS, 4 * Q_BLOCK, LANES), lambda i, t: (0, 0, 0))],
        out_specs=pl.BlockSpec((1, tq, B_WIDTH), lambda i, t: (i, t, 0)),
        out_shape=jax.ShapeDtypeStruct((b, s, B_WIDTH), BF16),
        compiler_params=_params(("arbitrary", "arbitrary")),
        name="window_attention",
    )(qb, kbt, kbt, kbt, vb, vb, vb, bias4, sink4)


def _pack_bf16_pairs(x):
    k = x.shape[1] // 2
    hi = pltpu.bitcast(x[:, :k].astype(BF16).astype(F32), U32)
    lo = pltpu.bitcast(x[:, k:].astype(BF16).astype(F32), U32)
    return hi | (lo >> 16)


def _unpack_bf16_pairs(w):
    hi = pltpu.bitcast(w & jnp.uint32(0xFFFF0000), F32)
    lo = pltpu.bitcast(w << 16, F32)
    return jnp.concatenate([hi, lo], axis=1).astype(BF16)


def _outproj_kernel(oa_ref, ob_ref, sga_ref, sgb_ref, x_ref, gt1_ref, sc2_ref, sh2_ref, g2_ref,
                    wa_ref, wb_ref, wo_ref, wr_ref, x1_ref, h2p_ref, afft_ref):
    a = _dot(oa_ref[0], wa_ref[...])
    b = _dot(ob_ref[0], wb_ref[...])
    merged = sga_ref[0].astype(F32) * a + sgb_ref[0].astype(F32) * b
    y = _dot(merged.astype(BF16), wo_ref[...])
    x1 = x_ref[0] + gt1_ref[0] * y
    x1_ref[0] = x1
    ms = jnp.mean(x1 * x1, axis=-1, keepdims=True)
    h2 = (x1 * lax.rsqrt(ms + EPS) * g2_ref[...]) * (1.0 + sc2_ref[0]) + sh2_ref[0]
    h2p_ref[...] = _pack_bf16_pairs(h2).reshape(h2p_ref.shape)
    logits = _dot(h2.astype(BF16), wr_ref[...])
    lane = lax.broadcasted_iota(I32, logits.shape, 1)
    logits = jnp.where(lane < N_EXPERTS, logits, -jnp.inf)
    m = jnp.max(logits, axis=-1, keepdims=True)
    e = jnp.exp(logits - m)
    aff = e / jnp.sum(e, axis=-1, keepdims=True)
    afft_ref[...] = aff.T[:N_EXPERTS, :]


def _out_projection(oa, ob, sga, sgb, x, gt1, sc2, sh2, g2, wa, wb, wo, wr, tm):
    b, s, d = x.shape
    nt = s // tm
    n = b * s
    tok = lambda w: pl.BlockSpec((1, tm, w), lambda i, t: (i, t, 0))
    vec = pl.BlockSpec((1, 1, d), lambda i, t: (i, 0, 0))
    full = lambda r, c: pl.BlockSpec((r, c), lambda i, t: (0, 0))
    sd = jax.ShapeDtypeStruct
    return pl.pallas_call(
        _outproj_kernel,
        grid=(b, nt),
        in_specs=[tok(A_WIDTH), tok(B_WIDTH), tok(d), tok(d), tok(d), vec, vec, vec, full(1, d),
                  full(A_WIDTH, d), full(B_WIDTH, d), full(d, d), full(d, LANES)],
        out_specs=[tok(d),
                   pl.BlockSpec((tm, 1, PACKED), lambda i, t: (i * nt + t, 0, 0)),
                   pl.BlockSpec((N_EXPERTS, tm), lambda i, t: (0, i * nt + t))],
        out_shape=[sd((b, s, d), F32), sd((n, 1, PACKED), U32), sd((N_EXPERTS, n), F32)],
        compiler_params=_params(("arbitrary", "arbitrary")),
        name="out_projection",
    )(oa, ob, sga, sgb, x, gt1, sc2, sh2, g2.reshape(1, d), wa, wb, wo, wr)


def _threshold_kernel(aff_ref, thr_ref, cut_ref, *, cap):
    bits = pltpu.bitcast(aff_ref[...], I32)
    n = bits.shape[1]

    def value_step(i, lo):
        cand = lo | jnp.left_shift(jnp.int32(1), 30 - i)
        cnt = jnp.sum((bits >= cand).astype(F32), axis=1, keepdims=True)
        return jnp.where(cnt >= cap, cand, lo)

    thr = lax.fori_loop(0, 31, value_step, jnp.zeros((N_EXPERTS, 1), I32))
    need = cap - jnp.sum((bits > thr).astype(F32), axis=1, keepdims=True)
    eq = bits == thr
    tpos = lax.broadcasted_iota(I32, (1, n), 1)
    nbits = max(n.bit_length(), 1)

    def index_step(i, cut):
        cand = cut + jnp.left_shift(jnp.int32(1), nbits - 1 - i)
        below = jnp.sum(jnp.where(eq & (tpos < cand), 1.0, 0.0), axis=1, keepdims=True)
        return jnp.where((below <= need) & (cand <= n), cand, cut)

    cut = lax.fori_loop(0, nbits, index_step, jnp.zeros((N_EXPERTS, 1), I32))
    thr_ref[...] = jnp.broadcast_to(thr, thr_ref.shape)
    cut_ref[...] = jnp.broadcast_to(cut, cut_ref.shape)


def _capacity_threshold(afft, cap):
    e, n = afft.shape
    out = pl.BlockSpec((e, LANES), lambda i: (0, 0))
    return pl.pallas_call(
        functools.partial(_threshold_kernel, cap=cap),
        grid=(1,),
        in_specs=[pl.BlockSpec((e, n), lambda i: (0, 0))],
        out_specs=[out, out],
        out_shape=[jax.ShapeDtypeStruct((e, LANES), I32)] * 2,
        compiler_params=_params(("arbitrary",)),
        name="capacity_threshold",
    )(afft)


_AUX_ROWS = 8


def _prefix_kernel(aff_ref, thr_ref, cut_ref, u_ref, ls_ref, posm_ref, slot_ref, tokrow_ref, offs_ref,
                   run_ref, *, tb, ch):
    i = pl.program_id(0)

    @pl.when(i == 0)
    def _():
        run_ref[...] = jnp.zeros_like(run_ref)

    bits = pltpu.bitcast(aff_ref[...], I32)
    tpos = i * tb + lax.broadcasted_iota(I32, (1, tb), 1)
    thr = thr_ref[:, 0:1]
    sel = (bits > thr) | ((bits == thr) & (tpos < cut_ref[:, 0:1]))
    s = jnp.where(sel, 1.0, 0.0)
    for j in range(tb // ch):
        sj = s[:, j * ch:(j + 1) * ch]
        cntj = jnp.sum(sj, axis=0, keepdims=True)
        x = jnp.concatenate([sj, cntj, jnp.zeros((_AUX_ROWS - 1, ch), F32)], axis=0).astype(BF16)
        run = run_ref[:, 0:1]
        incl = _dot(x, u_ref[...]) + run
        posm_ref[:, j * ch:(j + 1) * ch] = jnp.where(sj > 0.0, incl[:N_EXPERTS] - sj, -1.0)
        tok_end = incl[N_EXPERTS:N_EXPERTS + 1]
        tok_off = tok_end - cntj
        slot_ref[:, j * ch:(j + 1) * ch] = tok_off + _dot(ls_ref[...], sj.astype(BF16))
        tokrow_ref[:, j * ch:(j + 1) * ch] = jnp.concatenate(
            [tok_off, tok_end, jnp.zeros((6, ch), F32)], axis=0)
        offs_ref[j] = jnp.broadcast_to(run[:N_EXPERTS], (N_EXPERTS, LANES))
        run_ref[...] = jnp.broadcast_to(incl[:, ch - 1:ch], run_ref.shape)


def _routing_prefix(afft, thr, cut, tb, ch):
    e, n = afft.shape
    upper = jnp.asarray(np.triu(np.ones((ch, ch), np.float32)), BF16)
    lower_strict = jnp.asarray(np.tril(np.ones((e, e), np.float32), -1), BF16)
    rows = lambda r: pl.BlockSpec((r, tb), lambda i: (0, i))
    const = lambda r, c: pl.BlockSpec((r, c), lambda i: (0, 0))
    sd = jax.ShapeDtypeStruct
    return pl.pallas_call(
        functools.partial(_prefix_kernel, tb=tb, ch=ch),
        grid=(n // tb,),
        in_specs=[rows(e), const(e, LANES), const(e, LANES), const(ch, ch), const(e, e)],
        out_specs=[rows(e), rows(e), rows(8), pl.BlockSpec((tb // ch, e, LANES), lambda i: (i, 0, 0))],
        out_shape=[sd((e, n), F32), sd((e, n), F32), sd((8, n), F32), sd((n // ch, e, LANES), F32)],
        scratch_shapes=[pltpu.VMEM((e + _AUX_ROWS, LANES), F32)],
        compiler_params=_params(("arbitrary",)),
        name="routing_prefix",
    )(afft, thr, cut, upper, lower_strict)


def _compact_kernel(win_ref, aff_ref, posm_ref, slot_ref, list_ref, *, tb, ch, nc):
    i = pl.program_id(0)

    @pl.when(i == 0)
    def _():
        list_ref[...] = jnp.zeros_like(list_ref)

    rank = lax.broadcasted_iota(I32, (2 * ch, ch), 0).astype(F32)
    tok_local = lax.broadcasted_iota(I32, (1, ch), 1).astype(F32)

    def expert_body(e, carry):
        for j in range(tb // ch):
            c = i * (tb // ch) + j
            w0 = win_ref[e * nc + c]
            lanes = slice(j * ch, (j + 1) * ch)
            rel = posm_ref[pl.ds(e, 1), lanes] - (w0 * ch).astype(F32)
            onehot_t = jnp.where(rank == rel, 1.0, 0.0).astype(BF16)
            slot = slot_ref[pl.ds(e, 1), lanes].astype(I32)
            g = aff_ref[pl.ds(e, 1), lanes]
            g_hi = g.astype(BF16).astype(F32)
            g_mid = (g - g_hi).astype(BF16).astype(F32)
            g_lo = g - g_hi - g_mid
            vals = jnp.concatenate(
                [tok_local, jnp.full((1, ch), c, I32).astype(F32),
                 (slot & 255).astype(F32), ((slot >> 8) & 255).astype(F32), (slot >> 16).astype(F32),
                 g_hi, g_mid, g_lo], axis=0).astype(BF16)
            out = lax.dot_general(vals, onehot_t, (((1,), (1,)), ((), ())), preferred_element_type=F32)
            list_ref[e, w0] += out[:, :ch]
            list_ref[e, w0 + 1] += out[:, ch:]
        return carry

    lax.fori_loop(0, N_EXPERTS, expert_body, 0)


def _routing_compact(win, afft, posm, slot, cap, tb, ch):
    e, n = afft.shape
    nc = n // ch
    nwin = cap // ch
    rows = pl.BlockSpec((e, tb), lambda i, w: (0, i))
    grid_spec = pltpu.PrefetchScalarGridSpec(
        num_scalar_prefetch=1,
        grid=(n // tb,),
        in_specs=[rows, rows, rows],
        out_specs=pl.BlockSpec((e, nwin + 2, 8, ch), lambda i, w: (0, 0, 0, 0)),
    )
    return pl.pallas_call(
        functools.partial(_compact_kernel, tb=tb, ch=ch, nc=nc),
        grid_spec=grid_spec,
        out_shape=jax.ShapeDtypeStruct((e, nwin + 2, 8, ch), F32),
        compiler_params=_params(("arbitrary",)),
        name="routing_compact",
    )(win, afft, posm, slot)


def _routing(afft, cap, tm, tt, blk, tb=2048, ch=256):
    e, n = afft.shape
    assert n // ch <= 256 and cap % ch == 0 and cap % tm == 0 and e * cap < (1 << 24)
    tb = min(tb, n)
    thr, cut = _capacity_threshold(afft, cap)
    posm, slot, tokrow, offs = _routing_prefix(afft, thr, cut, tb, ch)
    win = (offs[:, :, 0].astype(I32) // ch).T.reshape(-1)
    lists = _routing_compact(win, afft, posm, slot, cap, tb, ch)[:, :cap // ch]
    nsteps = e * cap // tm
    idx = (lists[:, :, 1] * ch + lists[:, :, 0]).astype(I32).reshape(nsteps, tm)
    dst = (lists[:, :, 2] + lists[:, :, 3] * 256.0 + lists[:, :, 4] * 65536.0).astype(I32).reshape(nsteps, tm)
    gate = ((lists[:, :, 5] + lists[:, :, 6]) + lists[:, :, 7]).reshape(nsteps, 1, tm)
    meta = jnp.concatenate([idx, dst], axis=1)
    gate8 = jnp.broadcast_to(gate, (nsteps, 8, tm))

    npairs = e * cap
    ntile, nblk = n // tt, npairs // blk
    start = tokrow[0, ::tt].astype(I32)
    end = jnp.concatenate([start[1:], jnp.full((1,), npairs, I32)])
    b0 = jnp.minimum(start // blk, nblk - 1)
    b1 = jnp.where(end > start, (end - 1) // blk, b0)
    nb = b1 - b0 + 1
    wend = jnp.cumsum(nb)
    woff = wend - nb
    w = jnp.arange(nblk + ntile, dtype=I32)
    wt = jnp.minimum(jnp.sum((w[:, None] >= wend[None, :]).astype(I32), axis=1), ntile - 1)
    valid = (w < wend[-1]).astype(I32)
    wblk = jnp.where(valid == 1, b0[wt] + (w - woff[wt]), b1[ntile - 1]).astype(I32)
    return meta, gate8, tokrow, wt, wblk, valid


def _moe_kernel(meta_hbm, gate_ref, h2_hbm, wg_ref, wu_ref, wd_ref, z_hbm,
                meta_smem, xbuf, x2d, zbuf, sem_m, sem_g, sem_s, *, tm, nsteps):
    nt = pl.num_programs(1)
    s = pl.program_id(0) * nt + pl.program_id(1)
    slot = s % 2

    def meta_copy(step, mslot):
        return pltpu.make_async_copy(meta_hbm.at[step], meta_smem.at[mslot], sem_m)

    def issue_gather(mslot, bslot):
        for r in range(tm):
            tok = meta_smem[mslot, r]
            pltpu.make_async_copy(h2_hbm.at[tok], xbuf.at[bslot, r], sem_g.at[bslot]).start()

    def wait_gather(bslot):
        pltpu.make_async_copy(h2_hbm.at[pl.ds(0, tm)], xbuf.at[bslot], sem_g.at[bslot]).wait()

    def issue_scatter(mslot, bslot):
        for r in range(tm):
            dst = meta_smem[mslot, tm + r]
            pltpu.make_async_copy(zbuf.at[bslot, r], z_hbm.at[dst], sem_s.at[bslot]).start()

    def wait_scatter(bslot):
        pltpu.make_async_copy(zbuf.at[bslot], z_hbm.at[pl.ds(0, tm)], sem_s.at[bslot]).wait()

    last = nsteps - 1

    @pl.when(s == 0)
    def _():
        for step, mslot in ((0, 0), (min(1, last), 1)):
            c = meta_copy(step, mslot)
            c.start()
            c.wait()
        for r in range(tm):
            meta_smem[3, tm + r] = nsteps * tm + r
        zbuf[1] = jnp.zeros(zbuf.shape[1:], zbuf.dtype)
        issue_gather(0, 0)

    wait_gather(slot)

    @pl.when(s >= 1)
    def _():
        wait_scatter(slot)

    x2d[...] = xbuf[slot].reshape(tm, PACKED)
    prefetch = meta_copy(jnp.minimum(s + 2, last), (s + 2) % 4)
    prefetch.start()
    issue_gather((s + 1) % 4, 1 - slot)
    issue_scatter((s + 3) % 4, 1 - slot)
    xe = _unpack_bf16_pairs(x2d[...])
    gate = _dot(xe, wg_ref[...])
    up = _dot(xe, wu_ref[...])
    hid = (gate * jax.nn.sigmoid(gate) * up).astype(BF16)
    gcol = gate_ref[...].T[:, 0:1]
    halves = [slice(0, tm // 2), slice(tm // 2, tm)]
    ye = [_dot(hid[h], wd_ref[...]) for h in halves]
    for h, y in zip(halves, ye):
        zbuf[slot, h] = _pack_bf16_pairs(y * gcol[h]).reshape(tm // 2, 1, PACKED)
    prefetch.wait()

    @pl.when(s == last)
    def _():
        issue_scatter(s % 4, slot)
        wait_scatter(slot)
        wait_scatter(1 - slot)
        wait_gather(1 - slot)


def _expert_mlp(meta, gate8, h2p, wg, wu, wd, tm):
    nsteps = meta.shape[0]
    nt = nsteps // N_EXPERTS
    d, f = wg.shape[1], wg.shape[2]
    step = lambda e, i: (e * nt + i, 0, 0)
    return pl.pallas_call(
        functools.partial(_moe_kernel, tm=tm, nsteps=nsteps),
        grid=(N_EXPERTS, nt),
        in_specs=[pl.BlockSpec(memory_space=pl.ANY),
                  pl.BlockSpec((None, 8, tm), step),
                  pl.BlockSpec(memory_space=pl.ANY),
                  pl.BlockSpec((None, d, f), lambda e, i: (e, 0, 0)),
                  pl.BlockSpec((None, d, f), lambda e, i: (e, 0, 0)),
                  pl.BlockSpec((None, f, d), lambda e, i: (e, 0, 0))],
        out_specs=pl.BlockSpec(memory_space=pl.ANY),
        out_shape=jax.ShapeDtypeStruct(((nsteps + 1) * tm, 1, PACKED), U32),
        scratch_shapes=[pltpu.SMEM((4, 2 * tm), I32),
                        pltpu.VMEM((2, tm, 1, PACKED), U32),
                        pltpu.VMEM((tm, PACKED), U32),
                        pltpu.VMEM((2, tm, 1, PACKED), U32),
                        pltpu.SemaphoreType.DMA,
                        pltpu.SemaphoreType.DMA((2,)),
                        pltpu.SemaphoreType.DMA((2,))],
        compiler_params=_params(("arbitrary", "arbitrary")),
        name="expert_mlp",
    )(meta, gate8, h2p, wg, wu, wd)


def _combine_kernel(wtile_ref, wblk_ref, wvalid_ref, z_ref, tokrow_ref, x1_ref, gt2_ref, gf_ref,
                    y_ref, acc_ref, z2d, *, tt):
    w = pl.program_id(0)
    nw = pl.num_programs(0)
    tile = wtile_ref[w]
    first = jnp.logical_or(w == 0, wtile_ref[jnp.maximum(w - 1, 0)] != tile)
    last = jnp.logical_or(w == nw - 1, wtile_ref[jnp.minimum(w + 1, nw - 1)] != tile)

    @pl.when(first)
    def _():
        acc_ref[...] = jnp.zeros_like(acc_ref)

    @pl.when(wvalid_ref[w] == 1)
    def _():
        blk = z2d.shape[0]
        z2d[...] = z_ref[...].reshape(z2d.shape)
        zrows = _unpack_bf16_pairs(z2d[...])
        pair = (wblk_ref[w] * blk + lax.broadcasted_iota(I32, (blk, tt), 0)).astype(F32)
        owned = (pair >= tokrow_ref[0:1, :]) & (pair < tokrow_ref[1:2, :])
        onehot = jnp.where(owned, 1.0, 0.0).T.astype(BF16)
        acc_ref[...] += _dot(onehot, zrows)

    @pl.when(last)
    def _():
        x2 = x1_ref[...] + gt2_ref[0] * acc_ref[...]
        ms = jnp.mean(x2 * x2, axis=-1, keepdims=True)
        y_ref[...] = x2 * lax.rsqrt(ms + EPS) * gf_ref[...]


def _combine(wtile, wblk, wvalid, z, tokrow, x1, gt2, gf, s, tt, blk):
    n, d = x1.shape
    nw = wtile.shape[0]
    grid_spec = pltpu.PrefetchScalarGridSpec(
        num_scalar_prefetch=3,
        grid=(nw,),
        in_specs=[pl.BlockSpec((blk, 1, PACKED), lambda w, wt, wb, wv: (wb[w], 0, 0)),
                  pl.BlockSpec((8, tt), lambda w, wt, wb, wv: (0, wt[w])),
                  pl.BlockSpec((tt, d), lambda w, wt, wb, wv: (wt[w], 0)),
                  pl.BlockSpec((1, 1, d), lambda w, wt, wb, wv: (wt[w] * tt // s, 0, 0)),
                  pl.BlockSpec((1, d), lambda w, wt, wb, wv: (0, 0))],
        out_specs=pl.BlockSpec((tt, d), lambda w, wt, wb, wv: (wt[w], 0)),
        scratch_shapes=[pltpu.VMEM((tt, d), F32), pltpu.VMEM((blk, PACKED), U32)],
    )
    return pl.pallas_call(
        functools.partial(_combine_kernel, tt=tt),
        grid_spec=grid_spec,
        out_shape=jax.ShapeDtypeStruct((n, d), F32),
        compiler_params=_params(("arbitrary",)),
        name="combine_final_norm",
    )(wtile, wblk, wvalid, z, tokrow, x1, gt2, gf.reshape(1, d))


def _trunk(x, mod, w, tm_proj=512, tq=512, tu=256, tq_win=512, tm_moe=512, route_ch=256, tt=512, blk=512):
    b, s, d = x.shape
    n = b * s
    cap = CAPACITY_FACTOR * n // N_EXPERTS
    sh1, sc1, gt1, sh2, sc2, gt2 = [m.reshape(b, 1, d) for m in jnp.split(mod, N_MOD, axis=-1)]

    qa, kat, va, qb, kbt, vb, sga, sgb = _in_projection(
        x, sc1, sh1, w["norm1_g"], w["w_in"], w["q_norm_g"], w["k_norm_g"], tm_proj)
    oa = _global_attention(qa, kat, va, tq, tu)
    ob = _window_attention(qb, kbt, vb, w["rel_bias"], w["sink"], tq_win)
    x1, h2p, afft = _out_projection(oa, ob, sga, sgb, x, gt1, sc2, sh2, w["norm2_g"],
                                    w["w_branch_a"], w["w_branch_b"], w["w_out"], w["w_router"], tm_proj)
    meta, gate8, tokrow, wtile, wblk, wvalid = _routing(afft, cap, tm_moe, tt, blk, ch=route_ch)
    z = _expert_mlp(meta, gate8, h2p, w["w_e_gate"], w["w_e_up"], w["w_e_down"], tm_moe)
    y = _combine(wtile, wblk, wvalid, z, tokrow, x1.reshape(n, d), gt2, w["norm_f_g"], s, tt, blk)
    return y.reshape(b, s, d)


def kernel(x_prompt, x_sample, c_prompt, c_sample, w_ada, b_ada, norm1_g, w_in, q_norm_g, k_norm_g, sink,
           w_branch_a, w_branch_b, w_out, norm2_g, w_router, w_e_gate, w_e_up, w_e_down, rel_bias, norm_f_g):
    assert w_ada.shape[0] == 1, "single layer"
    bp, bs = c_prompt.shape[0], c_sample.shape[0]
    rows = -(-(bp + bs) // 16) * 16
    c = jnp.concatenate([c_prompt, c_sample, jnp.zeros((rows - bp - bs, D_MODEL), F32)], axis=0)
    mod = _modulation(c, w_ada[0], b_ada[0])

    router = jnp.zeros((D_MODEL, LANES), BF16).at[:, :N_EXPERTS].set(w_router[0].astype(BF16))
    w = dict(norm1_g=norm1_g[0], w_in=w_in[0].astype(BF16), q_norm_g=q_norm_g[0], k_norm_g=k_norm_g[0],
             sink=sink[0], w_branch_a=w_branch_a[0].astype(BF16), w_branch_b=w_branch_b[0].astype(BF16),
             w_out=w_out[0].astype(BF16), norm2_g=norm2_g[0], w_router=router,
             w_e_gate=w_e_gate[0].astype(BF16), w_e_up=w_e_up[0].astype(BF16),
             w_e_down=w_e_down[0].astype(BF16), rel_bias=rel_bias, norm_f_g=norm_f_g)
    y_prompt = _trunk(x_prompt, mod[:bp], w)
    y_sample = _trunk(x_sample, mod[bp:bp + bs], w)
    return (y_prompt, y_sample)
```

```python
import functools
import math

import jax
import jax.numpy as jnp
import numpy as np
from jax import lax
from jax.experimental import pallas as pl
from jax.experimental.pallas import tpu as pltpu

F32 = jnp.float32
BF16 = jnp.bfloat16
I32 = jnp.int32
U32 = jnp.uint32

D_MODEL = 1024
HEAD_DIM = 64
A_HEADS = 8
A_KV_HEADS = 2
B_HEADS = 8
B_KV_HEADS = 2
A_WIDTH = A_HEADS * HEAD_DIM
B_WIDTH = B_HEADS * HEAD_DIM
KV_WIDTH = A_KV_HEADS * HEAD_DIM
Q_BLOCK = 128
WINDOW = 128
GRID_W = 64
ROPE_THETA = 10000.0
AXIS_DIM = HEAD_DIM // 2
N_BUCKETS = 32
MAX_DISTANCE = 128
N_EXPERTS = 16
CAPACITY_FACTOR = 2
D_FF = 2048
N_MOD = 6
EPS = 1e-6
NEG_INF = -1e30
IN_COLS = A_WIDTH + 2 * KV_WIDTH + B_WIDTH + 2 * KV_WIDTH + 2 * D_MODEL

LANES = 128
PACKED = D_MODEL // 2
VMEM_LIMIT = 56 * 1024 * 1024
LOGITS_VMEM_BUDGET = 16 * 1024 * 1024

_QA, _KA, _VA = 0, A_WIDTH, A_WIDTH + KV_WIDTH
_QB = A_WIDTH + 2 * KV_WIDTH
_KB, _VB = _QB + B_WIDTH, _QB + B_WIDTH + KV_WIDTH
_GA = _QB + B_WIDTH + 2 * KV_WIDTH
_GB = _GA + D_MODEL


def _params(sem, vmem=VMEM_LIMIT):
    return pltpu.CompilerParams(dimension_semantics=sem, vmem_limit_bytes=vmem)


def _dot(a, b):
    return jnp.dot(a, b, preferred_element_type=F32)


def _mod_kernel(c_ref, w_ref, b_ref, o_ref):
    c = c_ref[...]
    s = c * jax.nn.sigmoid(c)
    o_ref[...] = _dot(s.astype(BF16), w_ref[...].astype(BF16)) + b_ref[...]


def _modulation(c, w_ada, b_ada):
    bp, d = c.shape
    n = w_ada.shape[1]
    tn = 1536
    return pl.pallas_call(
        _mod_kernel,
        grid=(n // tn,),
        in_specs=[pl.BlockSpec((bp, d), lambda j: (0, 0)),
                  pl.BlockSpec((d, tn), lambda j: (0, j)),
                  pl.BlockSpec((1, tn), lambda j: (0, j))],
        out_specs=pl.BlockSpec((bp, tn), lambda j: (0, j)),
        out_shape=jax.ShapeDtypeStruct((bp, n), F32),
        compiler_params=_params(("arbitrary",)),
        name="modulation",
    )(c, w_ada, b_ada.reshape(1, n))


def _swap16(x):
    n = x.shape[-1]
    left = pltpu.roll(x, n - 16, axis=1)
    right = pltpu.roll(x, 16, axis=1)
    lane = lax.broadcasted_iota(I32, x.shape, 1)
    return jnp.where((lane % 32) < 16, left, right)


def _inproj_kernel(x_ref, sc_ref, sh_ref, g1_ref, w_ref, aq_ref, bq_ref, ak_ref, bk_ref, hm_ref,
                   qa_ref, kat_ref, va_ref, qb_ref, kbt_ref, vb_ref, sga_ref, sgb_ref):
    tm = x_ref.shape[1]
    halves = [slice(0, tm // 2), slice(tm // 2, tm)]
    hs = []
    for r in halves:
        x = x_ref[0, r]
        ms = jnp.mean(x * x, axis=-1, keepdims=True)
        xn = x * lax.rsqrt(ms + EPS) * g1_ref[...]
        hs.append((xn * (1.0 + sc_ref[0]) + sh_ref[0]).astype(BF16))

    def proj(h, lo, width):
        return _dot(h, w_ref[:, lo:lo + width])

    def head_rms(q):
        pieces = []
        for j in range(q.shape[1] // LANES):
            qq = q[:, j * LANES:(j + 1) * LANES]
            pieces.append(_dot((qq * qq).astype(BF16), hm_ref[...]))
        msq = pieces[0] if len(pieces) == 1 else jnp.concatenate(pieces, axis=1)
        return q * lax.rsqrt(msq + EPS)

    def rope(qh, a, b):
        reps = qh.shape[1] // LANES
        if reps > 1:
            a = jnp.concatenate([a] * reps, axis=1)
            b = jnp.concatenate([b] * reps, axis=1)
        return qh * a + _swap16(qh) * b

    for r, h in zip(halves, hs):
        sga_ref[0, r] = jax.nn.sigmoid(proj(h, _GA, D_MODEL)).astype(BF16)
    for r, h in zip(halves, hs):
        qa_ref[0, r] = rope(head_rms(proj(h, _QA, A_WIDTH)), aq_ref[r], bq_ref[r]).astype(BF16)
    for r, h in zip(halves, hs):
        kat_ref[0, :, r] = rope(head_rms(proj(h, _KA, KV_WIDTH)), ak_ref[r], bk_ref[r]).T.astype(BF16)
    for r, h in zip(halves, hs):
        va_ref[0, r] = proj(h, _VA, KV_WIDTH).astype(BF16)
    for r, h in zip(halves, hs):
        qb_ref[0, r] = (proj(h, _QB, B_WIDTH) * (HEAD_DIM ** -0.5)).astype(BF16)
    for r, h in zip(halves, hs):
        kbt_ref[0, :, r] = proj(h, _KB, KV_WIDTH).T.astype(BF16)
    for r, h in zip(halves, hs):
        vb_ref[0, r] = proj(h, _VB, KV_WIDTH).astype(BF16)
    for r, h in zip(halves, hs):
        sgb_ref[0, r] = jax.nn.sigmoid(proj(h, _GB, D_MODEL)).astype(BF16)


def _rope_tables(s, gain, scale):
    pos = jnp.arange(s, dtype=I32)
    row = (pos // GRID_W).astype(F32)
    col = (pos % GRID_W).astype(F32)
    inv_freq = ROPE_THETA ** (-jnp.arange(0, AXIS_DIM, 2, dtype=F32) / AXIS_DIM)
    ang_r = row[:, None] * inv_freq
    ang_c = col[:, None] * inv_freq
    cos64 = jnp.concatenate([jnp.cos(ang_r)] * 2 + [jnp.cos(ang_c)] * 2, axis=1)
    sin64 = jnp.concatenate([-jnp.sin(ang_r), jnp.sin(ang_r), -jnp.sin(ang_c), jnp.sin(ang_c)], axis=1)
    g = gain.astype(F32)
    gp = jnp.concatenate([g[16:32], g[0:16], g[48:64], g[32:48]])
    a = cos64 * g[None, :] * scale
    b = sin64 * gp[None, :] * scale
    return jnp.concatenate([a, a], axis=1), jnp.concatenate([b, b], axis=1)


def _in_projection(x, sc1, sh1, g1, w_in, q_gain, k_gain, tm):
    b, s, d = x.shape
    aq, bq = _rope_tables(s, q_gain, HEAD_DIM ** -0.5 * math.log2(math.e))
    ak, bk = _rope_tables(s, k_gain, 1.0)
    head_mean = jnp.asarray(np.kron(np.eye(2), np.full((HEAD_DIM, HEAD_DIM), 1.0 / HEAD_DIM)), BF16)
    tok = lambda w: pl.BlockSpec((1, tm, w), lambda i, t: (i, t, 0))
    tr = pl.BlockSpec((1, KV_WIDTH, tm), lambda i, t: (i, 0, t))
    vec = pl.BlockSpec((1, 1, d), lambda i, t: (i, 0, 0))
    tab = pl.BlockSpec((tm, LANES), lambda i, t: (t, 0))
    sd = jax.ShapeDtypeStruct
    return pl.pallas_call(
        _inproj_kernel,
        grid=(b, s // tm),
        in_specs=[tok(d), vec, vec,
                  pl.BlockSpec((1, d), lambda i, t: (0, 0)),
                  pl.BlockSpec((d, IN_COLS), lambda i, t: (0, 0)),
                  tab, tab, tab, tab,
                  pl.BlockSpec((LANES, LANES), lambda i, t: (0, 0))],
        out_specs=[tok(A_WIDTH), tr, tok(KV_WIDTH), tok(B_WIDTH), tr, tok(KV_WIDTH), tok(d), tok(d)],
        out_shape=[sd((b, s, A_WIDTH), BF16), sd((b, KV_WIDTH, s), BF16), sd((b, s, KV_WIDTH), BF16),
                   sd((b, s, B_WIDTH), BF16), sd((b, KV_WIDTH, s), BF16), sd((b, s, KV_WIDTH), BF16),
                   sd((b, s, d), BF16), sd((b, s, d), BF16)],
        compiler_params=_params(("arbitrary", "arbitrary")),
        name="in_projection",
    )(x, sc1, sh1, g1.reshape(1, d), w_in, aq, bq, ak, bk, head_mean)


def _half_lane_variants(v, k_is_one):
    lane = lax.broadcasted_iota(I32, v.shape, 1)
    vr = pltpu.roll(v, HEAD_DIM, axis=1)
    own_lo = jnp.where(k_is_one, vr, v)
    own_hi = jnp.where(k_is_one, v, vr)
    lo = jnp.where(lane < HEAD_DIM, own_lo, 0.0)
    hi = jnp.where(lane >= HEAD_DIM, own_hi, 0.0)
    return lo, hi


_SUM_LANE_LO = LANES - 1
_SUM_LANE_HI = 0


def _gattn_kernel(q_ref, kt_ref, v_ref, o_ref, vv_ref, s_ref, qs_ref, os_ref, *, tu):
    k = pl.program_id(1)
    qi = pl.program_id(2)
    tq = q_ref.shape[1]
    group = A_HEADS // A_KV_HEADS
    nu = group * (tq // tu)

    @pl.when(qi == 0)
    def _():
        lo, hi = _half_lane_variants(v_ref[0].astype(F32), k == 1)
        lane = lax.broadcasted_iota(I32, lo.shape, 1)
        vv_ref[0] = jnp.where(lane == _SUM_LANE_LO, 1.0, lo).astype(BF16)
        vv_ref[1] = jnp.where(lane == _SUM_LANE_HI, 1.0, hi).astype(BF16)

    for u in range(nu):
        r, g = divmod(u, group)
        qs_ref[u] = q_ref[0, r * tu:(r + 1) * tu, g * HEAD_DIM:(g + 1) * HEAD_DIM]

    upb = s_ref.shape[0]

    def units(i, carry):
        maxes = []
        for j in range(upb):
            sc = _dot(qs_ref[upb * i + j], kt_ref[0])
            s_ref[j] = sc
            maxes.append(jnp.max(sc, axis=-1, keepdims=True))
        for j in range(upb):
            p = jnp.exp2(s_ref[j] - maxes[j])
            os_ref[upb * i + j] = _dot(p.astype(BF16), vv_ref[j % 2])
        return carry

    lax.fori_loop(0, nu // upb, units, 0)

    lane_o = lax.broadcasted_iota(I32, (tu, LANES), 1)
    for r in range(tq // tu):
        for j in range(group // 2):
            even = os_ref[r * group + 2 * j]
            odd = os_ref[r * group + 2 * j + 1]
            o = (jnp.where(lane_o < HEAD_DIM, even, 0.0) / even[:, _SUM_LANE_LO:_SUM_LANE_LO + 1]
                 + jnp.where(lane_o >= HEAD_DIM, odd, 0.0) / odd[:, _SUM_LANE_HI:_SUM_LANE_HI + 1])
            o_ref[0, r * tu:(r + 1) * tu, j * LANES:(j + 1) * LANES] = o.astype(o_ref.dtype)


def _global_attention(qa, kat, va, tq, tu):
    b, s, _ = qa.shape
    gw = A_WIDTH // A_KV_HEADS
    tq = min(tq, s)
    tu = min(tu, tq)
    nu = gw // HEAD_DIM * (tq // tu)
    upb = max(2, min(nu, LOGITS_VMEM_BUDGET // (tu * s * 4)))
    while nu % upb or upb % 2:
        upb -= 1
    return pl.pallas_call(
        functools.partial(_gattn_kernel, tu=tu),
        grid=(b, A_KV_HEADS, s // tq),
        in_specs=[pl.BlockSpec((1, tq, gw), lambda i, k, t: (i, t, k)),
                  pl.BlockSpec((1, HEAD_DIM, s), lambda i, k, t: (i, k, 0)),
                  pl.BlockSpec((1, s, KV_WIDTH), lambda i, k, t: (i, 0, 0))],
        out_specs=pl.BlockSpec((1, tq, gw), lambda i, k, t: (i, t, k)),
        out_shape=jax.ShapeDtypeStruct((b, s, A_WIDTH), BF16),
        scratch_shapes=[pltpu.VMEM((2, s, KV_WIDTH), BF16), pltpu.VMEM((upb, tu, s), F32),
                        pltpu.VMEM((nu, tu, HEAD_DIM), BF16), pltpu.VMEM((nu, tu, LANES), F32)],
        compiler_params=_params(("arbitrary", "arbitrary", "arbitrary")),
        name="global_attention",
    )(qa, kat, va)


def _t5_bucket_np(rel):
    half = N_BUCKETS // 2
    max_exact = half // 2
    base = (rel > 0).astype(np.int32) * half
    n = np.abs(rel)
    large = max_exact + (np.log(np.maximum(n, 1).astype(np.float32) / max_exact)
                         / math.log(MAX_DISTANCE / max_exact) * (half - max_exact)).astype(np.int32)
    large = np.minimum(large, half - 1)
    return base + np.where(n < max_exact, n, large)


def _stack_order(k):
    group = B_HEADS // B_KV_HEADS
    return [k * group + g for g in (0, 2, 1, 3)]


def _wattn_kernel(q_ref, ktp_ref, ktc_ref, ktn_ref, vp_ref, vc_ref, vn_ref, bias_ref, sink_ref, o_ref):
    t = pl.program_id(1)
    nt = pl.num_programs(1)
    span = Q_BLOCK + 2 * WINDOW
    nqb = q_ref.shape[1] // Q_BLOCK
    kt = jnp.concatenate([ktp_ref[0], ktc_ref[0], ktn_ref[0]], axis=1)
    v = jnp.concatenate([vp_ref[0], vc_ref[0], vn_ref[0]], axis=0).astype(F32)
    col = lax.broadcasted_iota(I32, (1, span), 1)
    first_ok = (col >= WINDOW) | (t > 0)
    last_ok = (col < WINDOW + Q_BLOCK) | (t < nt - 1)
    half_rows = 2 * Q_BLOCK
    values = []
    for k in range(B_KV_HEADS):
        lo, hi = _half_lane_variants(v, k == 1)
        values.append((lo.astype(BF16), hi.astype(BF16)))
    sinks = [sink_ref[k][:, 0:1] for k in range(B_KV_HEADS)]
    items = [(k, jb) for k in range(B_KV_HEADS) for jb in range(nqb)]
    keys = lambda jb: slice(jb * Q_BLOCK, jb * Q_BLOCK + span)
    rows = lambda jb: slice(jb * Q_BLOCK, (jb + 1) * Q_BLOCK)

    logits = []
    for k, jb in items:
        q4 = jnp.concatenate([q_ref[0, rows(jb), h * HEAD_DIM:(h + 1) * HEAD_DIM] for h in _stack_order(k)],
                             axis=0)
        lg = _dot(q4, kt[k * HEAD_DIM:(k + 1) * HEAD_DIM, keys(jb)]) + bias_ref[k]
        if jb == 0:
            lg = jnp.where(first_ok, lg, NEG_INF)
        if jb == nqb - 1:
            lg = jnp.where(last_ok, lg, NEG_INF)
        logits.append(lg)
    maxes = [jnp.maximum(jnp.max(lg, axis=-1, keepdims=True), sinks[k]) for (k, _), lg in zip(items, logits)]
    probs = [jnp.exp(lg - m) for lg, m in zip(logits, maxes)]
    dens = [jnp.sum(p, axis=-1, keepdims=True) + jnp.exp(sinks[k] - m)
            for (k, _), p, m in zip(items, probs, maxes)]
    outs = []
    for (k, jb), p, den in zip(items, probs, dens):
        pb = p.astype(BF16)
        vlo, vhi = values[k]
        o_even = _dot(pb[:half_rows], vlo[keys(jb)]) / den[:half_rows]
        o_odd = _dot(pb[half_rows:], vhi[keys(jb)]) / den[half_rows:]
        outs.append((o_even, o_odd))
    for (k, jb), (o_even, o_odd) in zip(items, outs):
        for j in range(2):
            pair = o_even[j * Q_BLOCK:(j + 1) * Q_BLOCK] + o_odd[j * Q_BLOCK:(j + 1) * Q_BLOCK]
            pj = 2 * k + j
            o_ref[0, rows(jb), pj * LANES:(pj + 1) * LANES] = pair.astype(o_ref.dtype)


def _window_attention(qb, kbt, vb, rel_bias, sink, tq):
    b, s, _ = qb.shape
    tq = min(tq, s)
    nb = s // Q_BLOCK
    per = tq // Q_BLOCK
    span = Q_BLOCK + 2 * WINDOW
    rel = np.arange(span)[None, :] - WINDOW - np.arange(Q_BLOCK)[:, None]
    band = np.abs(rel) <= WINDOW
    onehot = np.eye(N_BUCKETS, dtype=np.float32)[:, _t5_bucket_np(rel).reshape(-1)]
    bias = jnp.dot(rel_bias.astype(F32).T, jnp.asarray(onehot), precision=lax.Precision.HIGHEST)
    bias = jnp.where(jnp.asarray(band)[None], bias.reshape(B_HEADS, Q_BLOCK, span), NEG_INF)
    order = np.array([_stack_order(k) for k in range(B_KV_HEADS)])
    bias4 = bias[order].reshape(B_KV_HEADS, 4 * Q_BLOCK, span)
    sink4 = jnp.broadcast_to(sink.astype(F32)[order][:, :, None, None],
                             (B_KV_HEADS, 4, Q_BLOCK, LANES)).reshape(B_KV_HEADS, 4 * Q_BLOCK, LANES)
    prev = lambda i, t: jnp.maximum(t * per - 1, 0)
    nxt = lambda i, t: jnp.minimum((t + 1) * per, nb - 1)
    kt_edge = lambda f: pl.BlockSpec((1, KV_WIDTH, Q_BLOCK), lambda i, t: (i, 0, f(i, t)))
    v_edge = lambda f: pl.BlockSpec((1, Q_BLOCK, KV_WIDTH), lambda i, t: (i, f(i, t), 0))
    return pl.pallas_call(
        _wattn_kernel,
        grid=(b, s // tq),
        in_specs=[pl.BlockSpec((1, tq, B_WIDTH), lambda i, t: (i, t, 0)),
                  kt_edge(prev), pl.BlockSpec((1, KV_WIDTH, tq), lambda i, t: (i, 0, t)), kt_edge(nxt),
                  v_edge(prev), pl.BlockSpec((1, tq, KV_WIDTH), lambda i, t: (i, t, 0)), v_edge(nxt),
                  pl.BlockSpec((B_KV_HEADS, 4 * Q_BLOCK, span), lambda i, t: (0, 0, 0)),
                  pl.BlockSpec((B_KV_HEADS, 4 * Q_BLOCK, LANES), lambda i, t: (0, 0, 0))],
        out_specs=pl.BlockSpec((1, tq, B_WIDTH), lambda i, t: (i, t, 0)),
        out_shape=jax.ShapeDtypeStruct((b, s, B_WIDTH), BF16),
        compiler_params=_params(("arbitrary", "arbitrary")),
        name="window_attention",
    )(qb, kbt, kbt, kbt, vb, vb, vb, bias4, sink4)


def _pack_bf16_pairs(x):
    k = x.shape[1] // 2
    hi = pltpu.bitcast(x[:, :k].astype(BF16).astype(F32), U32)
    lo = pltpu.bitcast(x[:, k:].astype(BF16).astype(F32), U32)
    return hi | (lo >> 16)


def _unpack_bf16_pairs(w):
    hi = pltpu.bitcast(w & jnp.uint32(0xFFFF0000), F32)
    lo = pltpu.bitcast(w << 16, F32)
    return jnp.concatenate([hi, lo], axis=1).astype(BF16)


def _outproj_kernel(oa_ref, ob_ref, sga_ref, sgb_ref, x_ref, gt1_ref, sc2_ref, sh2_ref, g2_ref,
                    wa_ref, wb_ref, wo_ref, wr_ref, x1_ref, h2p_ref, afft_ref):
    tm = x_ref.shape[1]
    halves = [slice(0, tm // 2), slice(tm // 2, tm)]
    merged = []
    for r in halves:
        a = _dot(oa_ref[0, r], wa_ref[...])
        b = _dot(ob_ref[0, r], wb_ref[...])
        merged.append((sga_ref[0, r].astype(F32) * a + sgb_ref[0, r].astype(F32) * b).astype(BF16))
    h2s = []
    for r, mg in zip(halves, merged):
        x1 = x_ref[0, r] + gt1_ref[0] * _dot(mg, wo_ref[...])
        x1_ref[0, r] = x1
        ms = jnp.mean(x1 * x1, axis=-1, keepdims=True)
        h2s.append((x1 * lax.rsqrt(ms + EPS) * g2_ref[...]) * (1.0 + sc2_ref[0]) + sh2_ref[0])
    for r, h2 in zip(halves, h2s):
        h2p_ref[r] = _pack_bf16_pairs(h2).reshape(tm // 2, 1, PACKED)
        logits = _dot(h2.astype(BF16), wr_ref[...])
        lane = lax.broadcasted_iota(I32, logits.shape, 1)
        logits = jnp.where(lane < N_EXPERTS, logits, -jnp.inf)
        m = jnp.max(logits, axis=-1, keepdims=True)
        e = jnp.exp(logits - m)
        aff = e / jnp.sum(e, axis=-1, keepdims=True)
        afft_ref[:, r] = aff.T[:N_EXPERTS, :]


def _out_projection(oa, ob, sga, sgb, x, gt1, sc2, sh2, g2, wa, wb, wo, wr, tm):
    b, s, d = x.shape
    nt = s // tm
    n = b * s
    tok = lambda w: pl.BlockSpec((1, tm, w), lambda i, t: (i, t, 0))
    vec = pl.BlockSpec((1, 1, d), lambda i, t: (i, 0, 0))
    full = lambda r, c: pl.BlockSpec((r, c), lambda i, t: (0, 0))
    sd = jax.ShapeDtypeStruct
    return pl.pallas_call(
        _outproj_kernel,
        grid=(b, nt),
        in_specs=[tok(A_WIDTH), tok(B_WIDTH), tok(d), tok(d), tok(d), vec, vec, vec, full(1, d),
                  full(A_WIDTH, d), full(B_WIDTH, d), full(d, d), full(d, LANES)],
        out_specs=[tok(d),
                   pl.BlockSpec((tm, 1, PACKED), lambda i, t: (i * nt + t, 0, 0)),
                   pl.BlockSpec((N_EXPERTS, tm), lambda i, t: (0, i * nt + t))],
        out_shape=[sd((b, s, d), F32), sd((n, 1, PACKED), U32), sd((N_EXPERTS, n), F32)],
        compiler_params=_params(("arbitrary", "arbitrary")),
        name="out_projection",
    )(oa, ob, sga, sgb, x, gt1, sc2, sh2, g2.reshape(1, d), wa, wb, wo, wr)


def _threshold_kernel(aff_ref, thr_ref, cut_ref, *, cap):
    bits = pltpu.bitcast(aff_ref[...], I32)
    n = bits.shape[1]

    def value_step(i, lo):
        cand = lo | jnp.left_shift(jnp.int32(1), 30 - i)
        cnt = jnp.sum((bits >= cand).astype(F32), axis=1, keepdims=True)
        return jnp.where(cnt >= cap, cand, lo)

    thr = lax.fori_loop(0, 31, value_step, jnp.zeros((N_EXPERTS, 1), I32))
    need = cap - jnp.sum((bits > thr).astype(F32), axis=1, keepdims=True)
    eq = bits == thr
    tpos = lax.broadcasted_iota(I32, (1, n), 1)
    nbits = max(n.bit_length(), 1)

    def index_step(i, cut):
        cand = cut + jnp.left_shift(jnp.int32(1), nbits - 1 - i)
        below = jnp.sum(jnp.where(eq & (tpos < cand), 1.0, 0.0), axis=1, keepdims=True)
        return jnp.where((below <= need) & (cand <= n), cand, cut)

    cut = lax.fori_loop(0, nbits, index_step, jnp.zeros((N_EXPERTS, 1), I32))
    thr_ref[...] = jnp.broadcast_to(thr, thr_ref.shape)
    cut_ref[...] = jnp.broadcast_to(cut, cut_ref.shape)


def _capacity_threshold(afft, cap):
    e, n = afft.shape
    out = pl.BlockSpec((e, LANES), lambda i: (0, 0))
    return pl.pallas_call(
        functools.partial(_threshold_kernel, cap=cap),
        grid=(1,),
        in_specs=[pl.BlockSpec((e, n), lambda i: (0, 0))],
        out_specs=[out, out],
        out_shape=[jax.ShapeDtypeStruct((e, LANES), I32)] * 2,
        compiler_params=_params(("arbitrary",)),
        name="capacity_threshold",
    )(afft)


_AUX_ROWS = 8


def _prefix_kernel(aff_ref, thr_ref, cut_ref, u_ref, ls_ref, posm_ref, slot_ref, tokrow_ref, offs_ref,
                   run_ref, *, tb, ch):
    i = pl.program_id(0)

    @pl.when(i == 0)
    def _():
        run_ref[...] = jnp.zeros_like(run_ref)

    bits = pltpu.bitcast(aff_ref[...], I32)
    tpos = i * tb + lax.broadcasted_iota(I32, (1, tb), 1)
    thr = thr_ref[:, 0:1]
    sel = (bits > thr) | ((bits == thr) & (tpos < cut_ref[:, 0:1]))
    s = jnp.where(sel, 1.0, 0.0)
    for j in range(tb // ch):
        sj = s[:, j * ch:(j + 1) * ch]
        cntj = jnp.sum(sj, axis=0, keepdims=True)
        x = jnp.concatenate([sj, cntj, jnp.zeros((_AUX_ROWS - 1, ch), F32)], axis=0).astype(BF16)
        run = run_ref[:, 0:1]
        incl = _dot(x, u_ref[...]) + run
        posm_ref[:, j * ch:(j + 1) * ch] = jnp.where(sj > 0.0, incl[:N_EXPERTS] - sj, -1.0)
        tok_end = incl[N_EXPERTS:N_EXPERTS + 1]
        tok_off = tok_end - cntj
        slot_ref[:, j * ch:(j + 1) * ch] = tok_off + _dot(ls_ref[...], sj.astype(BF16))
        tokrow_ref[:, j * ch:(j + 1) * ch] = jnp.concatenate(
            [tok_off, tok_end, jnp.zeros((6, ch), F32)], axis=0)
        offs_ref[j] = jnp.broadcast_to(run[:N_EXPERTS], (N_EXPERTS, LANES))
        run_ref[...] = jnp.broadcast_to(incl[:, ch - 1:ch], run_ref.shape)


def _routing_prefix(afft, thr, cut, tb, ch):
    e, n = afft.shape
    upper = jnp.asarray(np.triu(np.ones((ch, ch), np.float32)), BF16)
    lower_strict = jnp.asarray(np.tril(np.ones((e, e), np.float32), -1), BF16)
    rows = lambda r: pl.BlockSpec((r, tb), lambda i: (0, i))
    const = lambda r, c: pl.BlockSpec((r, c), lambda i: (0, 0))
    sd = jax.ShapeDtypeStruct
    return pl.pallas_call(
        functools.partial(_prefix_kernel, tb=tb, ch=ch),
        grid=(n // tb,),
        in_specs=[rows(e), const(e, LANES), const(e, LANES), const(ch, ch), const(e, e)],
        out_specs=[rows(e), rows(e), rows(8), pl.BlockSpec((tb // ch, e, LANES), lambda i: (i, 0, 0))],
        out_shape=[sd((e, n), F32), sd((e, n), F32), sd((8, n), F32), sd((n // ch, e, LANES), F32)],
        scratch_shapes=[pltpu.VMEM((e + _AUX_ROWS, LANES), F32)],
        compiler_params=_params(("arbitrary",)),
        name="routing_prefix",
    )(afft, thr, cut, upper, lower_strict)


def _compact_kernel(win_ref, aff_ref, posm_ref, slot_ref, list_ref, *, tb, ch, nc):
    i = pl.program_id(0)

    @pl.when(i == 0)
    def _():
        list_ref[...] = jnp.zeros_like(list_ref)

    rank = lax.broadcasted_iota(I32, (2 * ch, ch), 0).astype(F32)
    tok_local = lax.broadcasted_iota(I32, (1, ch), 1).astype(F32)

    def expert_body(e, carry):
        for j in range(tb // ch):
            c = i * (tb // ch) + j
            w0 = win_ref[e * nc + c]
            lanes = slice(j * ch, (j + 1) * ch)
            rel = posm_ref[pl.ds(e, 1), lanes] - (w0 * ch).astype(F32)
            onehot_t = jnp.where(rank == rel, 1.0, 0.0).astype(BF16)
            slot = slot_ref[pl.ds(e, 1), lanes].astype(I32)
            g = aff_ref[pl.ds(e, 1), lanes]
            g_hi = g.astype(BF16).astype(F32)
            g_mid = (g - g_hi).astype(BF16).astype(F32)
            g_lo = g - g_hi - g_mid
            vals = jnp.concatenate(
                [tok_local, jnp.full((1, ch), c, I32).astype(F32),
                 (slot & 255).astype(F32), ((slot >> 8) & 255).astype(F32), (slot >> 16).astype(F32),
                 g_hi, g_mid, g_lo], axis=0).astype(BF16)
            out = lax.dot_general(vals, onehot_t, (((1,), (1,)), ((), ())), preferred_element_type=F32)
            list_ref[e, w0] += out[:, :ch]
            list_ref[e, w0 + 1] += out[:, ch:]
        return carry

    lax.fori_loop(0, N_EXPERTS, expert_body, 0)


def _routing_compact(win, afft, posm, slot, cap, tb, ch):
    e, n = afft.shape
    nc = n // ch
    nwin = cap // ch
    rows = pl.BlockSpec((e, tb), lambda i, w: (0, i))
    grid_spec = pltpu.PrefetchScalarGridSpec(
        num_scalar_prefetch=1,
        grid=(n // tb,),
        in_specs=[rows, rows, rows],
        out_specs=pl.BlockSpec((e, nwin + 2, 8, ch), lambda i, w: (0, 0, 0, 0)),
    )
    return pl.pallas_call(
        functools.partial(_compact_kernel, tb=tb, ch=ch, nc=nc),
        grid_spec=grid_spec,
        out_shape=jax.ShapeDtypeStruct((e, nwin + 2, 8, ch), F32),
        compiler_params=_params(("arbitrary",)),
        name="routing_compact",
    )(win, afft, posm, slot)


def _routing(afft, cap, tm, tt, blk, tb=2048, ch=256):
    e, n = afft.shape
    assert n // ch <= 256 and cap % ch == 0 and cap % tm == 0 and e * cap < (1 << 24)
    tb = min(tb, n)
    thr, cut = _capacity_threshold(afft, cap)
    posm, slot, tokrow, offs = _routing_prefix(afft, thr, cut, tb, ch)
    win = (offs[:, :, 0].astype(I32) // ch).T.reshape(-1)
    lists = _routing_compact(win, afft, posm, slot, cap, tb, ch)[:, :cap // ch]
    nsteps = e * cap // tm
    idx = (lists[:, :, 1] * ch + lists[:, :, 0]).astype(I32).reshape(nsteps, tm)
    dst = (lists[:, :, 2] + lists[:, :, 3] * 256.0 + lists[:, :, 4] * 65536.0).astype(I32).reshape(nsteps, tm)
    gate = ((lists[:, :, 5] + lists[:, :, 6]) + lists[:, :, 7]).reshape(nsteps, 1, tm)
    meta = jnp.concatenate([idx, dst], axis=1)
    gate8 = jnp.broadcast_to(gate, (nsteps, 8, tm))

    npairs = e * cap
    ntile, nblk = n // tt, npairs // blk
    start = tokrow[0, ::tt].astype(I32)
    end = jnp.concatenate([start[1:], jnp.full((1,), npairs, I32)])
    b0 = jnp.minimum(start // blk, nblk - 1)
    b1 = jnp.where(end > start, (end - 1) // blk, b0)
    nb = b1 - b0 + 1
    wend = jnp.cumsum(nb)
    woff = wend - nb
    w = jnp.arange(nblk + ntile, dtype=I32)
    wt = jnp.minimum(jnp.sum((w[:, None] >= wend[None, :]).astype(I32), axis=1), ntile - 1)
    valid = (w < wend[-1]).astype(I32)
    wblk = jnp.where(valid == 1, b0[wt] + (w - woff[wt]), b1[ntile - 1]).astype(I32)
    return meta, gate8, tokrow, wt, wblk, valid


def _moe_kernel(meta_hbm, gate_ref, h2_hbm, wg_ref, wu_ref, wd_ref, z_hbm,
                meta_smem, xbuf, x2d, zbuf, sem_m, sem_g, sem_s, *, tm, nsteps):
    nt = pl.num_programs(1)
    s = pl.program_id(0) * nt + pl.program_id(1)
    slot = s % 2

    def meta_copy(step, mslot):
        return pltpu.make_async_copy(meta_hbm.at[step], meta_smem.at[mslot], sem_m)

    def issue_gather(mslot, bslot):
        for r in range(tm):
            tok = meta_smem[mslot, r]
            pltpu.make_async_copy(h2_hbm.at[tok], xbuf.at[bslot, r], sem_g.at[bslot]).start(priority=r % 2)

    def wait_gather(bslot):
        pltpu.make_async_copy(h2_hbm.at[pl.ds(0, tm)], xbuf.at[bslot], sem_g.at[bslot]).wait()

    def issue_scatter(mslot, bslot):
        for r in range(tm):
            dst = meta_smem[mslot, tm + r]
            pltpu.make_async_copy(zbuf.at[bslot, r], z_hbm.at[dst], sem_s.at[bslot]).start(priority=r % 2)

    def wait_scatter(bslot):
        pltpu.make_async_copy(zbuf.at[bslot], z_hbm.at[pl.ds(0, tm)], sem_s.at[bslot]).wait()

    last = nsteps - 1

    @pl.when(s == 0)
    def _():
        for step, mslot in ((0, 0), (min(1, last), 1)):
            c = meta_copy(step, mslot)
            c.start()
            c.wait()
        for r in range(tm):
            meta_smem[3, tm + r] = nsteps * tm + r
        zbuf[1] = jnp.zeros(zbuf.shape[1:], zbuf.dtype)
        issue_gather(0, 0)

    wait_gather(slot)

    @pl.when(s >= 1)
    def _():
        wait_scatter(slot)

    x2d[...] = xbuf[slot].reshape(tm, PACKED)
    prefetch = meta_copy(jnp.minimum(s + 2, last), (s + 2) % 4)
    prefetch.start()
    issue_gather((s + 1) % 4, 1 - slot)
    issue_scatter((s + 3) % 4, 1 - slot)
    xe = _unpack_bf16_pairs(x2d[...])
    gate = _dot(xe, wg_ref[...])
    up = _dot(xe, wu_ref[...])
    hid = (gate * jax.nn.sigmoid(gate) * up).astype(BF16)
    gcol = gate_ref[...].T[:, 0:1]
    halves = [slice(0, tm // 2), slice(tm // 2, tm)]
    ye = [_dot(hid[h], wd_ref[...]) for h in halves]
    for h, y in zip(halves, ye):
        zbuf[slot, h] = _pack_bf16_pairs(y * gcol[h]).reshape(tm // 2, 1, PACKED)
    prefetch.wait()

    @pl.when(s == last)
    def _():
        issue_scatter(s % 4, slot)
        wait_scatter(slot)
        wait_scatter(1 - slot)
        wait_gather(1 - slot)


def _expert_mlp(meta, gate8, h2p, wg, wu, wd, tm):
    nsteps = meta.shape[0]
    nt = nsteps // N_EXPERTS
    d, f = wg.shape[1], wg.shape[2]
    step = lambda e, i: (e * nt + i, 0, 0)
    return pl.pallas_call(
        functools.partial(_moe_kernel, tm=tm, nsteps=nsteps),
        grid=(N_EXPERTS, nt),
        in_specs=[pl.BlockSpec(memory_space=pl.ANY),
                  pl.BlockSpec((None, 8, tm), step),
                  pl.BlockSpec(memory_space=pl.ANY),
                  pl.BlockSpec((None, d, f), lambda e, i: (e, 0, 0)),
                  pl.BlockSpec((None, d, f), lambda e, i: (e, 0, 0)),
                  pl.BlockSpec((None, f, d), lambda e, i: (e, 0, 0))],
        out_specs=pl.BlockSpec(memory_space=pl.ANY),
        out_shape=jax.ShapeDtypeStruct(((nsteps + 1) * tm, 1, PACKED), U32),
        scratch_shapes=[pltpu.SMEM((4, 2 * tm), I32),
                        pltpu.VMEM((2, tm, 1, PACKED), U32),
                        pltpu.VMEM((tm, PACKED), U32),
                        pltpu.VMEM((2, tm, 1, PACKED), U32),
                        pltpu.SemaphoreType.DMA,
                        pltpu.SemaphoreType.DMA((2,)),
                        pltpu.SemaphoreType.DMA((2,))],
        compiler_params=_params(("arbitrary", "arbitrary")),
        name="expert_mlp",
    )(meta, gate8, h2p, wg, wu, wd)


def _combine_kernel(wtile_ref, wblk_ref, wvalid_ref, z_ref, tokrow_ref, x1_ref, gt2_ref, gf_ref,
                    y_ref, acc_ref, z2d, *, tt):
    w = pl.program_id(0)
    nw = pl.num_programs(0)
    tile = wtile_ref[w]
    first = jnp.logical_or(w == 0, wtile_ref[jnp.maximum(w - 1, 0)] != tile)
    last = jnp.logical_or(w == nw - 1, wtile_ref[jnp.minimum(w + 1, nw - 1)] != tile)

    @pl.when(first)
    def _():
        acc_ref[...] = jnp.zeros_like(acc_ref)

    @pl.when(wvalid_ref[w] == 1)
    def _():
        blk = z2d.shape[0]
        z2d[...] = z_ref[...].reshape(z2d.shape)
        zrows = _unpack_bf16_pairs(z2d[...])
        pair = (wblk_ref[w] * blk + lax.broadcasted_iota(I32, (blk, tt), 0)).astype(F32)
        owned = (pair >= tokrow_ref[0:1, :]) & (pair < tokrow_ref[1:2, :])
        onehot = jnp.where(owned, 1.0, 0.0).T.astype(BF16)
        acc_ref[...] += _dot(onehot, zrows)

    @pl.when(last)
    def _():
        x2 = x1_ref[...] + gt2_ref[0] * acc_ref[...]
        ms = jnp.mean(x2 * x2, axis=-1, keepdims=True)
        y_ref[...] = x2 * lax.rsqrt(ms + EPS) * gf_ref[...]


def _combine(wtile, wblk, wvalid, z, tokrow, x1, gt2, gf, s, tt, blk):
    n, d = x1.shape
    nw = wtile.shape[0]
    grid_spec = pltpu.PrefetchScalarGridSpec(
        num_scalar_prefetch=3,
        grid=(nw,),
        in_specs=[pl.BlockSpec((blk, 1, PACKED), lambda w, wt, wb, wv: (wb[w], 0, 0)),
                  pl.BlockSpec((8, tt), lambda w, wt, wb, wv: (0, wt[w])),
                  pl.BlockSpec((tt, d), lambda w, wt, wb, wv: (wt[w], 0)),
                  pl.BlockSpec((1, 1, d), lambda w, wt, wb, wv: (wt[w] * tt // s, 0, 0)),
                  pl.BlockSpec((1, d), lambda w, wt, wb, wv: (0, 0))],
        out_specs=pl.BlockSpec((tt, d), lambda w, wt, wb, wv: (wt[w], 0)),
        scratch_shapes=[pltpu.VMEM((tt, d), F32), pltpu.VMEM((blk, PACKED), U32)],
    )
    return pl.pallas_call(
        functools.partial(_combine_kernel, tt=tt),
        grid_spec=grid_spec,
        out_shape=jax.ShapeDtypeStruct((n, d), F32),
        compiler_params=_params(("arbitrary",)),
        name="combine_final_norm",
    )(wtile, wblk, wvalid, z, tokrow, x1, gt2, gf.reshape(1, d))


def _trunk(x, mod, w, tm_proj=512, tq=512, tu=256, tq_win=512, tm_moe=512, route_ch=256, tt=512, blk=512):
    b, s, d = x.shape
    n = b * s
    cap = CAPACITY_FACTOR * n // N_EXPERTS
    sh1, sc1, gt1, sh2, sc2, gt2 = [m.reshape(b, 1, d) for m in jnp.split(mod, N_MOD, axis=-1)]

    qa, kat, va, qb, kbt, vb, sga, sgb = _in_projection(
        x, sc1, sh1, w["norm1_g"], w["w_in"], w["q_norm_g"], w["k_norm_g"], tm_proj)
    oa = _global_attention(qa, kat, va, tq, tu)
    ob = _window_attention(qb, kbt, vb, w["rel_bias"], w["sink"], tq_win)
    x1, h2p, afft = _out_projection(oa, ob, sga, sgb, x, gt1, sc2, sh2, w["norm2_g"],
                                    w["w_branch_a"], w["w_branch_b"], w["w_out"], w["w_router"], tm_proj)
    meta, gate8, tokrow, wtile, wblk, wvalid = _routing(afft, cap, tm_moe, tt, blk, ch=route_ch)
    z = _expert_mlp(meta, gate8, h2p, w["w_e_gate"], w["w_e_up"], w["w_e_down"], tm_moe)
    y = _combine(wtile, wblk, wvalid, z, tokrow, x1.reshape(n, d), gt2, w["norm_f_g"], s, tt, blk)
    return y.reshape(b, s, d)


def kernel(x_prompt, x_sample, c_prompt, c_sample, w_ada, b_ada, norm1_g, w_in, q_norm_g, k_norm_g, sink,
           w_branch_a, w_branch_b, w_out, norm2_g, w_router, w_e_gate, w_e_up, w_e_down, rel_bias, norm_f_g):
    assert w_ada.shape[0] == 1, "single layer"
    bp, bs = c_prompt.shape[0], c_sample.shape[0]
    rows = -(-(bp + bs) // 16) * 16
    c = jnp.concatenate([c_prompt, c_sample, jnp.zeros((rows - bp - bs, D_MODEL), F32)], axis=0)
    mod = _modulation(c, w_ada[0], b_ada[0])

    router = jnp.zeros((D_MODEL, LANES), BF16).at[:, :N_EXPERTS].set(w_router[0].astype(BF16))
    w = dict(norm1_g=norm1_g[0], w_in=w_in[0].astype(BF16), q_norm_g=q_norm_g[0], k_norm_g=k_norm_g[0],
             sink=sink[0], w_branch_a=w_branch_a[0].astype(BF16), w_branch_b=w_branch_b[0].astype(BF16),
             w_out=w_out[0].astype(BF16), norm2_g=norm2_g[0], w_router=router,
             w_e_gate=w_e_gate[0].astype(BF16), w_e_up=w_e_up[0].astype(BF16),
             w_e_down=w_e_down[0].astype(BF16), rel_bias=rel_bias, norm_f_g=norm_f_g)
    y_prompt = _trunk(x_prompt, mod[:bp], w)
    y_sample = _trunk(x_sample, mod[bp:bp + bs], w)
    return (y_prompt, y_sample)
```

```python
import functools
import math

import jax
import jax.numpy as jnp
import numpy as np
from jax import lax
from jax.experimental import pallas as pl
from jax.experimental.pallas import tpu as pltpu

F32 = jnp.float32
BF16 = jnp.bfloat16
I32 = jnp.int32
U32 = jnp.uint32

D_MODEL = 1024
HEAD_DIM = 64
A_HEADS = 8
A_KV_HEADS = 2
B_HEADS = 8
B_KV_HEADS = 2
A_WIDTH = A_HEADS * HEAD_DIM
B_WIDTH = B_HEADS * HEAD_DIM
KV_WIDTH = A_KV_HEADS * HEAD_DIM
Q_BLOCK = 128
WINDOW = 128
GRID_W = 64
ROPE_THETA = 10000.0
AXIS_DIM = HEAD_DIM // 2
N_BUCKETS = 32
MAX_DISTANCE = 128
N_EXPERTS = 16
CAPACITY_FACTOR = 2
D_FF = 2048
N_MOD = 6
EPS = 1e-6
NEG_INF = -1e30
IN_COLS = A_WIDTH + 2 * KV_WIDTH + B_WIDTH + 2 * KV_WIDTH + 2 * D_MODEL

LANES = 128
PACKED = D_MODEL // 2
VMEM_LIMIT = 56 * 1024 * 1024
LOGITS_VMEM_BUDGET = 16 * 1024 * 1024

_QA, _KA, _VA = 0, A_WIDTH, A_WIDTH + KV_WIDTH
_QB = A_WIDTH + 2 * KV_WIDTH
_KB, _VB = _QB + B_WIDTH, _QB + B_WIDTH + KV_WIDTH
_GA = _QB + B_WIDTH + 2 * KV_WIDTH
_GB = _GA + D_MODEL


def _params(sem, vmem=VMEM_LIMIT):
    return pltpu.CompilerParams(dimension_semantics=sem, vmem_limit_bytes=vmem)


def _dot(a, b):
    return jnp.dot(a, b, preferred_element_type=F32)


def _mod_kernel(c_ref, w_ref, b_ref, o_ref):
    c = c_ref[...]
    s = c * jax.nn.sigmoid(c)
    o_ref[...] = _dot(s.astype(BF16), w_ref[...].astype(BF16)) + b_ref[...]


def _modulation(c, w_ada, b_ada):
    bp, d = c.shape
    n = w_ada.shape[1]
    tn = 1536
    return pl.pallas_call(
        _mod_kernel,
        grid=(n // tn,),
        in_specs=[pl.BlockSpec((bp, d), lambda j: (0, 0)),
                  pl.BlockSpec((d, tn), lambda j: (0, j)),
                  pl.BlockSpec((1, tn), lambda j: (0, j))],
        out_specs=pl.BlockSpec((bp, tn), lambda j: (0, j)),
        out_shape=jax.ShapeDtypeStruct((bp, n), F32),
        compiler_params=_params(("arbitrary",)),
        name="modulation",
    )(c, w_ada, b_ada.reshape(1, n))


def _swap16(x):
    n = x.shape[-1]
    left = pltpu.roll(x, n - 16, axis=1)
    right = pltpu.roll(x, 16, axis=1)
    lane = lax.broadcasted_iota(I32, x.shape, 1)
    return jnp.where((lane % 32) < 16, left, right)


def _inproj_kernel(x_ref, sc_ref, sh_ref, g1_ref, w_ref, aq_ref, bq_ref, ak_ref, bk_ref, hm_ref,
                   qa_ref, kat_ref, va_ref, qb_ref, kbt_ref, vb_ref, sga_ref, sgb_ref):
    tm = x_ref.shape[1]
    halves = [slice(0, tm // 2), slice(tm // 2, tm)]
    hs = []
    for r in halves:
        x = x_ref[0, r]
        ms = jnp.mean(x * x, axis=-1, keepdims=True)
        xn = x * lax.rsqrt(ms + EPS) * g1_ref[...]
        hs.append((xn * (1.0 + sc_ref[0]) + sh_ref[0]).astype(BF16))

    def proj(h, lo, width):
        return _dot(h, w_ref[:, lo:lo + width])

    def head_rms(q):
        pieces = []
        for j in range(q.shape[1] // LANES):
            qq = q[:, j * LANES:(j + 1) * LANES]
            pieces.append(_dot((qq * qq).astype(BF16), hm_ref[...]))
        msq = pieces[0] if len(pieces) == 1 else jnp.concatenate(pieces, axis=1)
        return q * lax.rsqrt(msq + EPS)

    def rope(qh, a, b):
        reps = qh.shape[1] // LANES
        if reps > 1:
            a = jnp.concatenate([a] * reps, axis=1)
            b = jnp.concatenate([b] * reps, axis=1)
        return qh * a + _swap16(qh) * b

    for r, h in zip(halves, hs):
        sga_ref[0, r] = jax.nn.sigmoid(proj(h, _GA, D_MODEL)).astype(BF16)
    for r, h in zip(halves, hs):
        qa_ref[0, r] = rope(head_rms(proj(h, _QA, A_WIDTH)), aq_ref[r], bq_ref[r]).astype(BF16)
    for r, h in zip(halves, hs):
        kat_ref[0, :, r] = rope(head_rms(proj(h, _KA, KV_WIDTH)), ak_ref[r], bk_ref[r]).T.astype(BF16)
    for r, h in zip(halves, hs):
        va_ref[0, r] = proj(h, _VA, KV_WIDTH).astype(BF16)
    for r, h in zip(halves, hs):
        qb_ref[0, r] = (proj(h, _QB, B_WIDTH) * (HEAD_DIM ** -0.5)).astype(BF16)
    for r, h in zip(halves, hs):
        kbt_ref[0, :, r] = proj(h, _KB, KV_WIDTH).T.astype(BF16)
    for r, h in zip(halves, hs):
        vb_ref[0, r] = proj(h, _VB, KV_WIDTH).astype(BF16)
    for r, h in zip(halves, hs):
        sgb_ref[0, r] = jax.nn.sigmoid(proj(h, _GB, D_MODEL)).astype(BF16)


def _rope_tables(s, gain, scale):
    pos = jnp.arange(s, dtype=I32)
    row = (pos // GRID_W).astype(F32)
    col = (pos % GRID_W).astype(F32)
    inv_freq = ROPE_THETA ** (-jnp.arange(0, AXIS_DIM, 2, dtype=F32) / AXIS_DIM)
    ang_r = row[:, None] * inv_freq
    ang_c = col[:, None] * inv_freq
    cos64 = jnp.concatenate([jnp.cos(ang_r)] * 2 + [jnp.cos(ang_c)] * 2, axis=1)
    sin64 = jnp.concatenate([-jnp.sin(ang_r), jnp.sin(ang_r), -jnp.sin(ang_c), jnp.sin(ang_c)], axis=1)
    g = gain.astype(F32)
    gp = jnp.concatenate([g[16:32], g[0:16], g[48:64], g[32:48]])
    a = cos64 * g[None, :] * scale
    b = sin64 * gp[None, :] * scale
    return jnp.concatenate([a, a], axis=1), jnp.concatenate([b, b], axis=1)


def _in_projection(x, sc1, sh1, g1, w_in, q_gain, k_gain, tm):
    b, s, d = x.shape
    aq, bq = _rope_tables(s, q_gain, HEAD_DIM ** -0.5 * math.log2(math.e))
    ak, bk = _rope_tables(s, k_gain, 1.0)
    head_mean = jnp.asarray(np.kron(np.eye(2), np.full((HEAD_DIM, HEAD_DIM), 1.0 / HEAD_DIM)), BF16)
    tok = lambda w: pl.BlockSpec((1, tm, w), lambda i, t: (i, t, 0))
    tr = pl.BlockSpec((1, KV_WIDTH, tm), lambda i, t: (i, 0, t))
    vec = pl.BlockSpec((1, 1, d), lambda i, t: (i, 0, 0))
    tab = pl.BlockSpec((tm, LANES), lambda i, t: (t, 0))
    sd = jax.ShapeDtypeStruct
    return pl.pallas_call(
        _inproj_kernel,
        grid=(b, s // tm),
        in_specs=[tok(d), vec, vec,
                  pl.BlockSpec((1, d), lambda i, t: (0, 0)),
                  pl.BlockSpec((d, IN_COLS), lambda i, t: (0, 0)),
                  tab, tab, tab, tab,
                  pl.BlockSpec((LANES, LANES), lambda i, t: (0, 0))],
        out_specs=[tok(A_WIDTH), tr, tok(KV_WIDTH), tok(B_WIDTH), tr, tok(KV_WIDTH), tok(d), tok(d)],
        out_shape=[sd((b, s, A_WIDTH), BF16), sd((b, KV_WIDTH, s), BF16), sd((b, s, KV_WIDTH), BF16),
                   sd((b, s, B_WIDTH), BF16), sd((b, KV_WIDTH, s), BF16), sd((b, s, KV_WIDTH), BF16),
                   sd((b, s, d), BF16), sd((b, s, d), BF16)],
        compiler_params=_params(("arbitrary", "arbitrary")),
        name="in_projection",
    )(x, sc1, sh1, g1.reshape(1, d), w_in, aq, bq, ak, bk, head_mean)


def _half_lane_variants(v, k_is_one):
    lane = lax.broadcasted_iota(I32, v.shape, 1)
    vr = pltpu.roll(v, HEAD_DIM, axis=1)
    own_lo = jnp.where(k_is_one, vr, v)
    own_hi = jnp.where(k_is_one, v, vr)
    lo = jnp.where(lane < HEAD_DIM, own_lo, 0.0)
    hi = jnp.where(lane >= HEAD_DIM, own_hi, 0.0)
    return lo, hi


_SUM_LANE_LO = LANES - 1
_SUM_LANE_HI = 0


def _gattn_kernel(q_ref, kt_ref, v_ref, o_ref, vv_ref, s_ref, qs_ref, os_ref, *, tu):
    k = pl.program_id(1)
    qi = pl.program_id(2)
    tq = q_ref.shape[1]
    group = A_HEADS // A_KV_HEADS
    nu = group * (tq // tu)

    @pl.when(qi == 0)
    def _():
        lo, hi = _half_lane_variants(v_ref[0].astype(F32), k == 1)
        lane = lax.broadcasted_iota(I32, lo.shape, 1)
        vv_ref[0] = jnp.where(lane == _SUM_LANE_LO, 1.0, lo).astype(BF16)
        vv_ref[1] = jnp.where(lane == _SUM_LANE_HI, 1.0, hi).astype(BF16)

    for u in range(nu):
        r, g = divmod(u, group)
        qs_ref[u] = q_ref[0, r * tu:(r + 1) * tu, g * HEAD_DIM:(g + 1) * HEAD_DIM]

    upb = s_ref.shape[0]

    def units(i, carry):
        maxes = []
        for j in range(upb):
            sc = _dot(qs_ref[upb * i + j], kt_ref[0])
            s_ref[j] = sc
            maxes.append(jnp.max(sc, axis=-1, keepdims=True))
        for j in range(upb):
            p = jnp.exp2(s_ref[j] - maxes[j])
            os_ref[upb * i + j] = _dot(p.astype(BF16), vv_ref[j % 2])
        return carry

    lax.fori_loop(0, nu // upb, units, 0)

    lane_o = lax.broadcasted_iota(I32, (tu, LANES), 1)
    for r in range(tq // tu):
        for j in range(group // 2):
            even = os_ref[r * group + 2 * j]
            odd = os_ref[r * group + 2 * j + 1]
            o = (jnp.where(lane_o < HEAD_DIM, even, 0.0) / even[:, _SUM_LANE_LO:_SUM_LANE_LO + 1]
                 + jnp.where(lane_o >= HEAD_DIM, odd, 0.0) / odd[:, _SUM_LANE_HI:_SUM_LANE_HI + 1])
            o_ref[0, r * tu:(r + 1) * tu, j * LANES:(j + 1) * LANES] = o.astype(o_ref.dtype)


def _global_attention(qa, kat, va, tq, tu):
    b, s, _ = qa.shape
    gw = A_WIDTH // A_KV_HEADS
    tq = min(tq, s)
    tu = min(tu, tq)
    nu = gw // HEAD_DIM * (tq // tu)
    upb = max(2, min(nu, LOGITS_VMEM_BUDGET // (tu * s * 4)))
    while nu % upb or upb % 2:
        upb -= 1
    return pl.pallas_call(
        functools.partial(_gattn_kernel, tu=tu),
        grid=(b, A_KV_HEADS, s // tq),
        in_specs=[pl.BlockSpec((1, tq, gw), lambda i, k, t: (i, t, k)),
                  pl.BlockSpec((1, HEAD_DIM, s), lambda i, k, t: (i, k, 0)),
                  pl.BlockSpec((1, s, KV_WIDTH), lambda i, k, t: (i, 0, 0))],
        out_specs=pl.BlockSpec((1, tq, gw), lambda i, k, t: (i, t, k)),
        out_shape=jax.ShapeDtypeStruct((b, s, A_WIDTH), BF16),
        scratch_shapes=[pltpu.VMEM((2, s, KV_WIDTH), BF16), pltpu.VMEM((upb, tu, s), F32),
                        pltpu.VMEM((nu, tu, HEAD_DIM), BF16), pltpu.VMEM((nu, tu, LANES), F32)],
        compiler_params=_params(("arbitrary", "arbitrary", "arbitrary")),
        name="global_attention",
    )(qa, kat, va)


def _t5_bucket_np(rel):
    half = N_BUCKETS // 2
    max_exact = half // 2
    base = (rel > 0).astype(np.int32) * half
    n = np.abs(rel)
    large = max_exact + (np.log(np.maximum(n, 1).astype(np.float32) / max_exact)
                         / math.log(MAX_DISTANCE / max_exact) * (half - max_exact)).astype(np.int32)
    large = np.minimum(large, half - 1)
    return base + np.where(n < max_exact, n, large)


def _stack_order(k):
    group = B_HEADS // B_KV_HEADS
    return [k * group + g for g in (0, 2, 1, 3)]


def _wattn_kernel(q_ref, ktp_ref, ktc_ref, ktn_ref, vp_ref, vc_ref, vn_ref, bias_ref, sink_ref, o_ref):
    t = pl.program_id(1)
    nt = pl.num_programs(1)
    span = Q_BLOCK + 2 * WINDOW
    nqb = q_ref.shape[1] // Q_BLOCK
    kt = jnp.concatenate([ktp_ref[0], ktc_ref[0], ktn_ref[0]], axis=1)
    v = jnp.concatenate([vp_ref[0], vc_ref[0], vn_ref[0]], axis=0).astype(F32)
    col = lax.broadcasted_iota(I32, (1, span), 1)
    first_ok = (col >= WINDOW) | (t > 0)
    last_ok = (col < WINDOW + Q_BLOCK) | (t < nt - 1)
    half_rows = 2 * Q_BLOCK
    values = []
    for k in range(B_KV_HEADS):
        lo, hi = _half_lane_variants(v, k == 1)
        values.append((lo.astype(BF16), hi.astype(BF16)))
    sinks = [sink_ref[k][:, 0:1] for k in range(B_KV_HEADS)]
    items = [(k, jb) for k in range(B_KV_HEADS) for jb in range(nqb)]
    keys = lambda jb: slice(jb * Q_BLOCK, jb * Q_BLOCK + span)
    rows = lambda jb: slice(jb * Q_BLOCK, (jb + 1) * Q_BLOCK)

    logits = []
    for k, jb in items:
        q4 = jnp.concatenate([q_ref[0, rows(jb), h * HEAD_DIM:(h + 1) * HEAD_DIM] for h in _stack_order(k)],
                             axis=0)
        lg = _dot(q4, kt[k * HEAD_DIM:(k + 1) * HEAD_DIM, keys(jb)]) + bias_ref[k]
        if jb == 0:
            lg = jnp.where(first_ok, lg, NEG_INF)
        if jb == nqb - 1:
            lg = jnp.where(last_ok, lg, NEG_INF)
        logits.append(lg)
    maxes = [jnp.maximum(jnp.max(lg, axis=-1, keepdims=True), sinks[k]) for (k, _), lg in zip(items, logits)]
    probs = [jnp.exp(lg - m) for lg, m in zip(logits, maxes)]
    dens = [jnp.sum(p, axis=-1, keepdims=True) + jnp.exp(sinks[k] - m)
            for (k, _), p, m in zip(items, probs, maxes)]
    outs = []
    for (k, jb), p, den in zip(items, probs, dens):
        pb = p.astype(BF16)
        vlo, vhi = values[k]
        o_even = _dot(pb[:half_rows], vlo[keys(jb)]) / den[:half_rows]
        o_odd = _dot(pb[half_rows:], vhi[keys(jb)]) / den[half_rows:]
        outs.append((o_even, o_odd))
    for (k, jb), (o_even, o_odd) in zip(items, outs):
        for j in range(2):
            pair = o_even[j * Q_BLOCK:(j + 1) * Q_BLOCK] + o_odd[j * Q_BLOCK:(j + 1) * Q_BLOCK]
            pj = 2 * k + j
            o_ref[0, rows(jb), pj * LANES:(pj + 1) * LANES] = pair.astype(o_ref.dtype)


def _window_attention(qb, kbt, vb, rel_bias, sink, tq):
    b, s, _ = qb.shape
    tq = min(tq, s)
    nb = s // Q_BLOCK
    per = tq // Q_BLOCK
    span = Q_BLOCK + 2 * WINDOW
    rel = np.arange(span)[None, :] - WINDOW - np.arange(Q_BLOCK)[:, None]
    band = np.abs(rel) <= WINDOW
    onehot = np.eye(N_BUCKETS, dtype=np.float32)[:, _t5_bucket_np(rel).reshape(-1)]
    bias = jnp.dot(rel_bias.astype(F32).T, jnp.asarray(onehot), precision=lax.Precision.HIGHEST)
    bias = jnp.where(jnp.asarray(band)[None], bias.reshape(B_HEADS, Q_BLOCK, span), NEG_INF)
    order = np.array([_stack_order(k) for k in range(B_KV_HEADS)])
    bias4 = bias[order].reshape(B_KV_HEADS, 4 * Q_BLOCK, span)
    sink4 = jnp.broadcast_to(sink.astype(F32)[order][:, :, None, None],
                             (B_KV_HEADS, 4, Q_BLOCK, LANES)).reshape(B_KV_HEADS, 4 * Q_BLOCK, LANES)
    prev = lambda i, t: jnp.maximum(t * per - 1, 0)
    nxt = lambda i, t: jnp.minimum((t + 1) * per, nb - 1)
    kt_edge = lambda f: pl.BlockSpec((1, KV_WIDTH, Q_BLOCK), lambda i, t: (i, 0, f(i, t)))
    v_edge = lambda f: pl.BlockSpec((1, Q_BLOCK, KV_WIDTH), lambda i, t: (i, f(i, t), 0))
    return pl.pallas_call(
        _wattn_kernel,
        grid=(b, s // tq),
        in_specs=[pl.BlockSpec((1, tq, B_WIDTH), lambda i, t: (i, t, 0)),
                  kt_edge(prev), pl.BlockSpec((1, KV_WIDTH, tq), lambda i, t: (i, 0, t)), kt_edge(nxt),
                  v_edge(prev), pl.BlockSpec((1, tq, KV_WIDTH), lambda i, t: (i, t, 0)), v_edge(nxt),
                  pl.BlockSpec((B_KV_HEADS, 4 * Q_BLOCK, span), lambda i, t: (0, 0, 0)),
                  pl.BlockSpec((B_KV_HEADS, 4 * Q_BLOCK, LANES), lambda i, t: (0, 0, 0))],
        out_specs=pl.BlockSpec((1, tq, B_WIDTH), lambda i, t: (i, t, 0)),
        out_shape=jax.ShapeDtypeStruct((b, s, B_WIDTH), BF16),
        compiler_params=_params(("arbitrary", "arbitrary")),
        name="window_attention",
    )(qb, kbt, kbt, kbt, vb, vb, vb, bias4, sink4)


def _pack_bf16_pairs(x):
    k = x.shape[1] // 2
    hi = pltpu.bitcast(x[:, :k].astype(BF16).astype(F32), U32)
    lo = pltpu.bitcast(x[:, k:].astype(BF16).astype(F32), U32)
    return hi | (lo >> 16)


def _unpack_bf16_pairs(w):
    hi = pltpu.bitcast(w & jnp.uint32(0xFFFF0000), F32)
    lo = pltpu.bitcast(w << 16, F32)
    return jnp.concatenate([hi, lo], axis=1).astype(BF16)


def _outproj_kernel(oa_ref, ob_ref, sga_ref, sgb_ref, x_ref, gt1_ref, sc2_ref, sh2_ref, g2_ref,
                    wa_ref, wb_ref, wo_ref, wr_ref, x1_ref, h2p_ref, afft_ref):
    tm = x_ref.shape[1]
    halves = [slice(0, tm // 2), slice(tm // 2, tm)]
    merged = []
    for r in halves:
        a = _dot(oa_ref[0, r], wa_ref[...])
        b = _dot(ob_ref[0, r], wb_ref[...])
        merged.append((sga_ref[0, r].astype(F32) * a + sgb_ref[0, r].astype(F32) * b).astype(BF16))
    h2s = []
    for r, mg in zip(halves, merged):
        x1 = x_ref[0, r] + gt1_ref[0] * _dot(mg, wo_ref[...])
        x1_ref[0, r] = x1
        ms = jnp.mean(x1 * x1, axis=-1, keepdims=True)
        h2s.append((x1 * lax.rsqrt(ms + EPS) * g2_ref[...]) * (1.0 + sc2_ref[0]) + sh2_ref[0])
    for r, h2 in zip(halves, h2s):
        h2p_ref[r] = _pack_bf16_pairs(h2).reshape(tm // 2, 1, PACKED)
        logits = _dot(h2.astype(BF16), wr_ref[...])
        lane = lax.broadcasted_iota(I32, logits.shape, 1)
        logits = jnp.where(lane < N_EXPERTS, logits, -jnp.inf)
        m = jnp.max(logits, axis=-1, keepdims=True)
        e = jnp.exp(logits - m)
        aff = e / jnp.sum(e, axis=-1, keepdims=True)
        afft_ref[:, r] = aff.T[:N_EXPERTS, :]


def _out_projection(oa, ob, sga, sgb, x, gt1, sc2, sh2, g2, wa, wb, wo, wr, tm):
    b, s, d = x.shape
    nt = s // tm
    n = b * s
    tok = lambda w: pl.BlockSpec((1, tm, w), lambda i, t: (i, t, 0))
    vec = pl.BlockSpec((1, 1, d), lambda i, t: (i, 0, 0))
    full = lambda r, c: pl.BlockSpec((r, c), lambda i, t: (0, 0))
    sd = jax.ShapeDtypeStruct
    return pl.pallas_call(
        _outproj_kernel,
        grid=(b, nt),
        in_specs=[tok(A_WIDTH), tok(B_WIDTH), tok(d), tok(d), tok(d), vec, vec, vec, full(1, d),
                  full(A_WIDTH, d), full(B_WIDTH, d), full(d, d), full(d, LANES)],
        out_specs=[tok(d),
                   pl.BlockSpec((tm, 1, PACKED), lambda i, t: (i * nt + t, 0, 0)),
                   pl.BlockSpec((N_EXPERTS, tm), lambda i, t: (0, i * nt + t))],
        out_shape=[sd((b, s, d), F32), sd((n, 1, PACKED), U32), sd((N_EXPERTS, n), F32)],
        compiler_params=_params(("arbitrary", "arbitrary")),
        name="out_projection",
    )(oa, ob, sga, sgb, x, gt1, sc2, sh2, g2.reshape(1, d), wa, wb, wo, wr)


def _threshold_kernel(aff_ref, thr_ref, cut_ref, *, cap):
    bits = pltpu.bitcast(aff_ref[...], I32)
    n = bits.shape[1]

    def value_step(i, lo):
        cand = lo | jnp.left_shift(jnp.int32(1), 30 - i)
        cnt = jnp.sum((bits >= cand).astype(F32), axis=1, keepdims=True)
        return jnp.where(cnt >= cap, cand, lo)

    thr = lax.fori_loop(0, 31, value_step, jnp.zeros((N_EXPERTS, 1), I32))
    need = cap - jnp.sum((bits > thr).astype(F32), axis=1, keepdims=True)
    eq = bits == thr
    tpos = lax.broadcasted_iota(I32, (1, n), 1)
    nbits = max(n.bit_length(), 1)

    def index_step(i, cut):
        cand = cut + jnp.left_shift(jnp.int32(1), nbits - 1 - i)
        below = jnp.sum(jnp.where(eq & (tpos < cand), 1.0, 0.0), axis=1, keepdims=True)
        return jnp.where((below <= need) & (cand <= n), cand, cut)

    cut = lax.fori_loop(0, nbits, index_step, jnp.zeros((N_EXPERTS, 1), I32))
    thr_ref[...] = jnp.broadcast_to(thr, thr_ref.shape)
    cut_ref[...] = jnp.broadcast_to(cut, cut_ref.shape)


def _capacity_threshold(afft, cap):
    e, n = afft.shape
    out = pl.BlockSpec((e, LANES), lambda i: (0, 0))
    return pl.pallas_call(
        functools.partial(_threshold_kernel, cap=cap),
        grid=(1,),
        in_specs=[pl.BlockSpec((e, n), lambda i: (0, 0))],
        out_specs=[out, out],
        out_shape=[jax.ShapeDtypeStruct((e, LANES), I32)] * 2,
        compiler_params=_params(("arbitrary",)),
        name="capacity_threshold",
    )(afft)


_AUX_ROWS = 8


def _prefix_kernel(aff_ref, thr_ref, cut_ref, u_ref, ls_ref, posm_ref, slot_ref, tokrow_ref, offs_ref,
                   run_ref, *, tb, ch):
    i = pl.program_id(0)

    @pl.when(i == 0)
    def _():
        run_ref[...] = jnp.zeros_like(run_ref)

    bits = pltpu.bitcast(aff_ref[...], I32)
    tpos = i * tb + lax.broadcasted_iota(I32, (1, tb), 1)
    thr = thr_ref[:, 0:1]
    sel = (bits > thr) | ((bits == thr) & (tpos < cut_ref[:, 0:1]))
    s = jnp.where(sel, 1.0, 0.0)
    for j in range(tb // ch):
        sj = s[:, j * ch:(j + 1) * ch]
        cntj = jnp.sum(sj, axis=0, keepdims=True)
        x = jnp.concatenate([sj, cntj, jnp.zeros((_AUX_ROWS - 1, ch), F32)], axis=0).astype(BF16)
        run = run_ref[:, 0:1]
        incl = _dot(x, u_ref[...]) + run
        posm_ref[:, j * ch:(j + 1) * ch] = jnp.where(sj > 0.0, incl[:N_EXPERTS] - sj, -1.0)
        tok_end = incl[N_EXPERTS:N_EXPERTS + 1]
        tok_off = tok_end - cntj
        slot_ref[:, j * ch:(j + 1) * ch] = tok_off + _dot(ls_ref[...], sj.astype(BF16))
        tokrow_ref[:, j * ch:(j + 1) * ch] = jnp.concatenate(
            [tok_off, tok_end, jnp.zeros((6, ch), F32)], axis=0)
        offs_ref[j] = jnp.broadcast_to(run[:N_EXPERTS], (N_EXPERTS, LANES))
        run_ref[...] = jnp.broadcast_to(incl[:, ch - 1:ch], run_ref.shape)


def _routing_prefix(afft, thr, cut, tb, ch):
    e, n = afft.shape
    upper = jnp.asarray(np.triu(np.ones((ch, ch), np.float32)), BF16)
    lower_strict = jnp.asarray(np.tril(np.ones((e, e), np.float32), -1), BF16)
    rows = lambda r: pl.BlockSpec((r, tb), lambda i: (0, i))
    const = lambda r, c: pl.BlockSpec((r, c), lambda i: (0, 0))
    sd = jax.ShapeDtypeStruct
    return pl.pallas_call(
        functools.partial(_prefix_kernel, tb=tb, ch=ch),
        grid=(n // tb,),
        in_specs=[rows(e), const(e, LANES), const(e, LANES), const(ch, ch), const(e, e)],
        out_specs=[rows(e), rows(e), rows(8), pl.BlockSpec((tb // ch, e, LANES), lambda i: (i, 0, 0))],
        out_shape=[sd((e, n), F32), sd((e, n), F32), sd((8, n), F32), sd((n // ch, e, LANES), F32)],
        scratch_shapes=[pltpu.VMEM((e + _AUX_ROWS, LANES), F32)],
        compiler_params=_params(("arbitrary",)),
        name="routing_prefix",
    )(afft, thr, cut, upper, lower_strict)


def _compact_kernel(win_ref, aff_ref, posm_ref, slot_ref, list_ref, *, tb, ch, nc):
    i = pl.program_id(0)

    @pl.when(i == 0)
    def _():
        list_ref[...] = jnp.zeros_like(list_ref)

    rank = lax.broadcasted_iota(I32, (2 * ch, ch), 0).astype(F32)
    tok_local = lax.broadcasted_iota(I32, (1, ch), 1).astype(F32)

    def expert_body(e, carry):
        for j in range(tb // ch):
            c = i * (tb // ch) + j
            w0 = win_ref[e * nc + c]
            lanes = slice(j * ch, (j + 1) * ch)
            rel = posm_ref[pl.ds(e, 1), lanes] - (w0 * ch).astype(F32)
            onehot_t = jnp.where(rank == rel, 1.0, 0.0).astype(BF16)
            slot = slot_ref[pl.ds(e, 1), lanes].astype(I32)
            g = aff_ref[pl.ds(e, 1), lanes]
            g_hi = g.astype(BF16).astype(F32)
            g_mid = (g - g_hi).astype(BF16).astype(F32)
            g_lo = g - g_hi - g_mid
            vals = jnp.concatenate(
                [tok_local, jnp.full((1, ch), c, I32).astype(F32),
                 (slot & 255).astype(F32), ((slot >> 8) & 255).astype(F32), (slot >> 16).astype(F32),
                 g_hi, g_mid, g_lo], axis=0).astype(BF16)
            out = lax.dot_general(vals, onehot_t, (((1,), (1,)), ((), ())), preferred_element_type=F32)
            list_ref[e, w0] += out[:, :ch]
            list_ref[e, w0 + 1] += out[:, ch:]
        return carry

    lax.fori_loop(0, N_EXPERTS, expert_body, 0)


def _routing_compact(win, afft, posm, slot, cap, tb, ch):
    e, n = afft.shape
    nc = n // ch
    nwin = cap // ch
    rows = pl.BlockSpec((e, tb), lambda i, w: (0, i))
    grid_spec = pltpu.PrefetchScalarGridSpec(
        num_scalar_prefetch=1,
        grid=(n // tb,),
        in_specs=[rows, rows, rows],
        out_specs=pl.BlockSpec((e, nwin + 2, 8, ch), lambda i, w: (0, 0, 0, 0)),
    )
    return pl.pallas_call(
        functools.partial(_compact_kernel, tb=tb, ch=ch, nc=nc),
        grid_spec=grid_spec,
        out_shape=jax.ShapeDtypeStruct((e, nwin + 2, 8, ch), F32),
        compiler_params=_params(("arbitrary",)),
        name="routing_compact",
    )(win, afft, posm, slot)


def _routing(afft, cap, tm, tt, blk, tb=2048, ch=256):
    e, n = afft.shape
    assert n // ch <= 256 and cap % ch == 0 and cap % tm == 0 and e * cap < (1 << 24)
    tb = min(tb, n)
    thr, cut = _capacity_threshold(afft, cap)
    posm, slot, tokrow, offs = _routing_prefix(afft, thr, cut, tb, ch)
    win = (offs[:, :, 0].astype(I32) // ch).T.reshape(-1)
    lists = _routing_compact(win, afft, posm, slot, cap, tb, ch)[:, :cap // ch]
    nsteps = e * cap // tm
    idx = (lists[:, :, 1] * ch + lists[:, :, 0]).astype(I32).reshape(nsteps, tm)
    dst = (lists[:, :, 2] + lists[:, :, 3] * 256.0 + lists[:, :, 4] * 65536.0).astype(I32).reshape(nsteps, tm)
    gate = ((lists[:, :, 5] + lists[:, :, 6]) + lists[:, :, 7]).reshape(nsteps, 1, tm)
    meta = jnp.concatenate([idx, dst], axis=1)
    gate8 = jnp.broadcast_to(gate, (nsteps, 8, tm))

    npairs = e * cap
    ntile, nblk = n // tt, npairs // blk
    start = tokrow[0, ::tt].astype(I32)
    end = jnp.concatenate([start[1:], jnp.full((1,), npairs, I32)])
    b0 = jnp.minimum(start // blk, nblk - 1)
    b1 = jnp.where(end > start, (end - 1) // blk, b0)
    nb = b1 - b0 + 1
    wend = jnp.cumsum(nb)
    woff = wend - nb
    w = jnp.arange(nblk + ntile, dtype=I32)
    wt = jnp.minimum(jnp.sum((w[:, None] >= wend[None, :]).astype(I32), axis=1), ntile - 1)
    valid = (w < wend[-1]).astype(I32)
    wblk = jnp.where(valid == 1, b0[wt] + (w - woff[wt]), b1[ntile - 1]).astype(I32)
    return meta, gate8, tokrow, wt, wblk, valid


def _moe_kernel(meta_hbm, gate_ref, h2_hbm, wg_ref, wu_ref, wd_ref, z_hbm,
                meta_smem, xbuf, x2d, zbuf, sem_m, sem_g, sem_s, *, tm, nsteps):
    nt = pl.num_programs(1)
    s = pl.program_id(0) * nt + pl.program_id(1)
    slot = s % 2

    def meta_copy(step, mslot):
        return pltpu.make_async_copy(meta_hbm.at[step], meta_smem.at[mslot], sem_m)

    def issue_gather(mslot, bslot):
        for r in range(tm):
            tok = meta_smem[mslot, r]
            pltpu.make_async_copy(h2_hbm.at[tok], xbuf.at[bslot, r], sem_g.at[bslot]).start(priority=1)

    def wait_gather(bslot):
        pltpu.make_async_copy(h2_hbm.at[pl.ds(0, tm)], xbuf.at[bslot], sem_g.at[bslot]).wait()

    def issue_scatter(mslot, bslot):
        for r in range(tm):
            dst = meta_smem[mslot, tm + r]
            pltpu.make_async_copy(zbuf.at[bslot, r], z_hbm.at[dst], sem_s.at[bslot]).start(priority=0)

    def wait_scatter(bslot):
        pltpu.make_async_copy(zbuf.at[bslot], z_hbm.at[pl.ds(0, tm)], sem_s.at[bslot]).wait()

    last = nsteps - 1

    @pl.when(s == 0)
    def _():
        for step, mslot in ((0, 0), (min(1, last), 1)):
            c = meta_copy(step, mslot)
            c.start()
            c.wait()
        for r in range(tm):
            meta_smem[3, tm + r] = nsteps * tm + r
        zbuf[1] = jnp.zeros(zbuf.shape[1:], zbuf.dtype)
        issue_gather(0, 0)

    wait_gather(slot)

    @pl.when(s >= 1)
    def _():
        wait_scatter(slot)

    x2d[...] = xbuf[slot].reshape(tm, PACKED)
    prefetch = meta_copy(jnp.minimum(s + 2, last), (s + 2) % 4)
    prefetch.start()
    issue_gather((s + 1) % 4, 1 - slot)
    issue_scatter((s + 3) % 4, 1 - slot)
    xe = _unpack_bf16_pairs(x2d[...])
    gate = _dot(xe, wg_ref[...])
    up = _dot(xe, wu_ref[...])
    hid = (gate * jax.nn.sigmoid(gate) * up).astype(BF16)
    gcol = gate_ref[...].T[:, 0:1]
    halves = [slice(0, tm // 2), slice(tm // 2, tm)]
    ye = [_dot(hid[h], wd_ref[...]) for h in halves]
    for h, y in zip(halves, ye):
        zbuf[slot, h] = _pack_bf16_pairs(y * gcol[h]).reshape(tm // 2, 1, PACKED)
    prefetch.wait()

    @pl.when(s == last)
    def _():
        issue_scatter(s % 4, slot)
        wait_scatter(slot)
        wait_scatter(1 - slot)
        wait_gather(1 - slot)


def _expert_mlp(meta, gate8, h2p, wg, wu, wd, tm):
    nsteps = meta.shape[0]
    nt = nsteps // N_EXPERTS
    d, f = wg.shape[1], wg.shape[2]
    step = lambda e, i: (e * nt + i, 0, 0)
    return pl.pallas_call(
        functools.partial(_moe_kernel, tm=tm, nsteps=nsteps),
        grid=(N_EXPERTS, nt),
        in_specs=[pl.BlockSpec(memory_space=pl.ANY),
                  pl.BlockSpec((None, 8, tm), step),
                  pl.BlockSpec(memory_space=pl.ANY),
                  pl.BlockSpec((None, d, f), lambda e, i: (e, 0, 0)),
                  pl.BlockSpec((None, d, f), lambda e, i: (e, 0, 0)),
                  pl.BlockSpec((None, f, d), lambda e, i: (e, 0, 0))],
        out_specs=pl.BlockSpec(memory_space=pl.ANY),
        out_shape=jax.ShapeDtypeStruct(((nsteps + 1) * tm, 1, PACKED), U32),
        scratch_shapes=[pltpu.SMEM((4, 2 * tm), I32),
                        pltpu.VMEM((2, tm, 1, PACKED), U32),
                        pltpu.VMEM((tm, PACKED), U32),
                        pltpu.VMEM((2, tm, 1, PACKED), U32),
                        pltpu.SemaphoreType.DMA,
                        pltpu.SemaphoreType.DMA((2,)),
                        pltpu.SemaphoreType.DMA((2,))],
        compiler_params=_params(("arbitrary", "arbitrary")),
        name="expert_mlp",
    )(meta, gate8, h2p, wg, wu, wd)


def _combine_kernel(wtile_ref, wblk_ref, wvalid_ref, z_ref, tokrow_ref, x1_ref, gt2_ref, gf_ref,
                    y_ref, acc_ref, z2d, *, tt):
    w = pl.program_id(0)
    nw = pl.num_programs(0)
    tile = wtile_ref[w]
    first = jnp.logical_or(w == 0, wtile_ref[jnp.maximum(w - 1, 0)] != tile)
    last = jnp.logical_or(w == nw - 1, wtile_ref[jnp.minimum(w + 1, nw - 1)] != tile)

    @pl.when(first)
    def _():
        acc_ref[...] = jnp.zeros_like(acc_ref)

    @pl.when(wvalid_ref[w] == 1)
    def _():
        blk = z2d.shape[0]
        z2d[...] = z_ref[...].reshape(z2d.shape)
        zrows = _unpack_bf16_pairs(z2d[...])
        pair = (wblk_ref[w] * blk + lax.broadcasted_iota(I32, (blk, tt), 0)).astype(F32)
        owned = (pair >= tokrow_ref[0:1, :]) & (pair < tokrow_ref[1:2, :])
        onehot = jnp.where(owned, 1.0, 0.0).T.astype(BF16)
        acc_ref[...] += _dot(onehot, zrows)

    @pl.when(last)
    def _():
        x2 = x1_ref[...] + gt2_ref[0] * acc_ref[...]
        ms = jnp.mean(x2 * x2, axis=-1, keepdims=True)
        y_ref[...] = x2 * lax.rsqrt(ms + EPS) * gf_ref[...]


def _combine(wtile, wblk, wvalid, z, tokrow, x1, gt2, gf, s, tt, blk):
    n, d = x1.shape
    nw = wtile.shape[0]
    grid_spec = pltpu.PrefetchScalarGridSpec(
        num_scalar_prefetch=3,
        grid=(nw,),
        in_specs=[pl.BlockSpec((blk, 1, PACKED), lambda w, wt, wb, wv: (wb[w], 0, 0)),
                  pl.BlockSpec((8, tt), lambda w, wt, wb, wv: (0, wt[w])),
                  pl.BlockSpec((tt, d), lambda w, wt, wb, wv: (wt[w], 0)),
                  pl.BlockSpec((1, 1, d), lambda w, wt, wb, wv: (wt[w] * tt // s, 0, 0)),
                  pl.BlockSpec((1, d), lambda w, wt, wb, wv: (0, 0))],
        out_specs=pl.BlockSpec((tt, d), lambda w, wt, wb, wv: (wt[w], 0)),
        scratch_shapes=[pltpu.VMEM((tt, d), F32), pltpu.VMEM((blk, PACKED), U32)],
    )
    return pl.pallas_call(
        functools.partial(_combine_kernel, tt=tt),
        grid_spec=grid_spec,
        out_shape=jax.ShapeDtypeStruct((n, d), F32),
        compiler_params=_params(("arbitrary",)),
        name="combine_final_norm",
    )(wtile, wblk, wvalid, z, tokrow, x1, gt2, gf.reshape(1, d))


def _trunk(x, mod, w, tm_proj=512, tq=512, tu=256, tq_win=1024, tm_moe=512, route_ch=256, tt=512, blk=512):
    b, s, d = x.shape
    n = b * s
    cap = CAPACITY_FACTOR * n // N_EXPERTS
    sh1, sc1, gt1, sh2, sc2, gt2 = [m.reshape(b, 1, d) for m in jnp.split(mod, N_MOD, axis=-1)]

    qa, kat, va, qb, kbt, vb, sga, sgb = _in_projection(
        x, sc1, sh1, w["norm1_g"], w["w_in"], w["q_norm_g"], w["k_norm_g"], tm_proj)
    oa = _global_attention(qa, kat, va, tq, tu)
    ob = _window_attention(qb, kbt, vb, w["rel_bias"], w["sink"], tq_win)
    x1, h2p, afft = _out_projection(oa, ob, sga, sgb, x, gt1, sc2, sh2, w["norm2_g"],
                                    w["w_branch_a"], w["w_branch_b"], w["w_out"], w["w_router"], tm_proj)
    meta, gate8, tokrow, wtile, wblk, wvalid = _routing(afft, cap, tm_moe, tt, blk, ch=route_ch)
    z = _expert_mlp(meta, gate8, h2p, w["w_e_gate"], w["w_e_up"], w["w_e_down"], tm_moe)
    y = _combine(wtile, wblk, wvalid, z, tokrow, x1.reshape(n, d), gt2, w["norm_f_g"], s, tt, blk)
    return y.reshape(b, s, d)


def kernel(x_prompt, x_sample, c_prompt, c_sample, w_ada, b_ada, norm1_g, w_in, q_norm_g, k_norm_g, sink,
           w_branch_a, w_branch_b, w_out, norm2_g, w_router, w_e_gate, w_e_up, w_e_down, rel_bias, norm_f_g):
    assert w_ada.shape[0] == 1, "single layer"
    bp, bs = c_prompt.shape[0], c_sample.shape[0]
    rows = -(-(bp + bs) // 16) * 16
    c = jnp.concatenate([c_prompt, c_sample, jnp.zeros((rows - bp - bs, D_MODEL), F32)], axis=0)
    mod = _modulation(c, w_ada[0], b_ada[0])

    router = jnp.zeros((D_MODEL, LANES), BF16).at[:, :N_EXPERTS].set(w_router[0].astype(BF16))
    w = dict(norm1_g=norm1_g[0], w_in=w_in[0].astype(BF16), q_norm_g=q_norm_g[0], k_norm_g=k_norm_g[0],
             sink=sink[0], w_branch_a=w_branch_a[0].astype(BF16), w_branch_b=w_branch_b[0].astype(BF16),
             w_out=w_out[0].astype(BF16), norm2_g=norm2_g[0], w_router=router,
             w_e_gate=w_e_gate[0].astype(BF16), w_e_up=w_e_up[0].astype(BF16),
             w_e_down=w_e_down[0].astype(BF16), rel_bias=rel_bias, norm_f_g=norm_f_g)
    y_prompt = _trunk(x_prompt, mod[:bp], w)
    y_sample = _trunk(x_sample, mod[bp:bp + bs], w)
    return (y_prompt, y_sample)
```

```python
import functools
import math

import jax
import jax.numpy as jnp
import numpy as np
from jax import lax
from jax.experimental import pallas as pl
from jax.experimental.pallas import tpu as pltpu

F32 = jnp.float32
BF16 = jnp.bfloat16
I32 = jnp.int32
U32 = jnp.uint32

D_MODEL = 1024
HEAD_DIM = 64
A_HEADS = 8
A_KV_HEADS = 2
B_HEADS = 8
B_KV_HEADS = 2
A_WIDTH = A_HEADS * HEAD_DIM
B_WIDTH = B_HEADS * HEAD_DIM
KV_WIDTH = A_KV_HEADS * HEAD_DIM
Q_BLOCK = 128
WINDOW = 128
GRID_W = 64
ROPE_THETA = 10000.0
AXIS_DIM = HEAD_DIM // 2
N_BUCKETS = 32
MAX_DISTANCE = 128
N_EXPERTS = 16
CAPACITY_FACTOR = 2
D_FF = 2048
N_MOD = 6
EPS = 1e-6
NEG_INF = -1e30
IN_COLS = A_WIDTH + 2 * KV_WIDTH + B_WIDTH + 2 * KV_WIDTH + 2 * D_MODEL

LANES = 128
PACKED = D_MODEL // 2
VMEM_LIMIT = 56 * 1024 * 1024
LOGITS_VMEM_BUDGET = 16 * 1024 * 1024

_QA, _KA, _VA = 0, A_WIDTH, A_WIDTH + KV_WIDTH
_QB = A_WIDTH + 2 * KV_WIDTH
_KB, _VB = _QB + B_WIDTH, _QB + B_WIDTH + KV_WIDTH
_GA = _QB + B_WIDTH + 2 * KV_WIDTH
_GB = _GA + D_MODEL


def _params(sem, vmem=VMEM_LIMIT):
    return pltpu.CompilerParams(dimension_semantics=sem, vmem_limit_bytes=vmem)


def _dot(a, b):
    return jnp.dot(a, b, preferred_element_type=F32)


def _mod_kernel(c_ref, w_ref, b_ref, o_ref):
    c = c_ref[...]
    s = c * jax.nn.sigmoid(c)
    o_ref[...] = _dot(s.astype(BF16), w_ref[...].astype(BF16)) + b_ref[...]


def _modulation(c, w_ada, b_ada):
    bp, d = c.shape
    n = w_ada.shape[1]
    tn = 1536
    return pl.pallas_call(
        _mod_kernel,
        grid=(n // tn,),
        in_specs=[pl.BlockSpec((bp, d), lambda j: (0, 0)),
                  pl.BlockSpec((d, tn), lambda j: (0, j)),
                  pl.BlockSpec((1, tn), lambda j: (0, j))],
        out_specs=pl.BlockSpec((bp, tn), lambda j: (0, j)),
        out_shape=jax.ShapeDtypeStruct((bp, n), F32),
        compiler_params=_params(("arbitrary",)),
        name="modulation",
    )(c, w_ada, b_ada.reshape(1, n))


def _swap16(x):
    n = x.shape[-1]
    left = pltpu.roll(x, n - 16, axis=1)
    right = pltpu.roll(x, 16, axis=1)
    lane = lax.broadcasted_iota(I32, x.shape, 1)
    return jnp.where((lane % 32) < 16, left, right)


def _inproj_kernel(x_ref, sc_ref, sh_ref, g1_ref, w_ref, aq_ref, bq_ref, ak_ref, bk_ref, hm_ref,
                   qa_ref, kat_ref, va_ref, qb_ref, kbt_ref, vb_ref, sga_ref, sgb_ref):
    tm = x_ref.shape[1]
    halves = [slice(0, tm // 2), slice(tm // 2, tm)]
    hs = []
    for r in halves:
        x = x_ref[0, r]
        ms = jnp.mean(x * x, axis=-1, keepdims=True)
        xn = x * lax.rsqrt(ms + EPS) * g1_ref[...]
        hs.append((xn * (1.0 + sc_ref[0]) + sh_ref[0]).astype(BF16))

    def proj(h, lo, width):
        return _dot(h, w_ref[:, lo:lo + width])

    def head_rms(q):
        pieces = []
        for j in range(q.shape[1] // LANES):
            qq = q[:, j * LANES:(j + 1) * LANES]
            pieces.append(_dot((qq * qq).astype(BF16), hm_ref[...]))
        msq = pieces[0] if len(pieces) == 1 else jnp.concatenate(pieces, axis=1)
        return q * lax.rsqrt(msq + EPS)

    def rope(qh, a, b):
        reps = qh.shape[1] // LANES
        if reps > 1:
            a = jnp.concatenate([a] * reps, axis=1)
            b = jnp.concatenate([b] * reps, axis=1)
        return qh * a + _swap16(qh) * b

    for r, h in zip(halves, hs):
        sga_ref[0, r] = jax.nn.sigmoid(proj(h, _GA, D_MODEL)).astype(BF16)
    for r, h in zip(halves, hs):
        qa_ref[0, r] = rope(head_rms(proj(h, _QA, A_WIDTH)), aq_ref[r], bq_ref[r]).astype(BF16)
    for r, h in zip(halves, hs):
        kat_ref[0, :, r] = rope(head_rms(proj(h, _KA, KV_WIDTH)), ak_ref[r], bk_ref[r]).T.astype(BF16)
    for r, h in zip(halves, hs):
        va_ref[0, r] = proj(h, _VA, KV_WIDTH).astype(BF16)
    for r, h in zip(halves, hs):
        qb_ref[0, r] = (proj(h, _QB, B_WIDTH) * (HEAD_DIM ** -0.5)).astype(BF16)
    for r, h in zip(halves, hs):
        kbt_ref[0, :, r] = proj(h, _KB, KV_WIDTH).T.astype(BF16)
    for r, h in zip(halves, hs):
        vb_ref[0, r] = proj(h, _VB, KV_WIDTH).astype(BF16)
    for r, h in zip(halves, hs):
        sgb_ref[0, r] = jax.nn.sigmoid(proj(h, _GB, D_MODEL)).astype(BF16)


def _rope_tables(s, gain, scale):
    pos = jnp.arange(s, dtype=I32)
    row = (pos // GRID_W).astype(F32)
    col = (pos % GRID_W).astype(F32)
    inv_freq = ROPE_THETA ** (-jnp.arange(0, AXIS_DIM, 2, dtype=F32) / AXIS_DIM)
    ang_r = row[:, None] * inv_freq
    ang_c = col[:, None] * inv_freq
    cos64 = jnp.concatenate([jnp.cos(ang_r)] * 2 + [jnp.cos(ang_c)] * 2, axis=1)
    sin64 = jnp.concatenate([-jnp.sin(ang_r), jnp.sin(ang_r), -jnp.sin(ang_c), jnp.sin(ang_c)], axis=1)
    g = gain.astype(F32)
    gp = jnp.concatenate([g[16:32], g[0:16], g[48:64], g[32:48]])
    a = cos64 * g[None, :] * scale
    b = sin64 * gp[None, :] * scale
    return jnp.concatenate([a, a], axis=1), jnp.concatenate([b, b], axis=1)


def _in_projection(x, sc1, sh1, g1, w_in, q_gain, k_gain, tm):
    b, s, d = x.shape
    aq, bq = _rope_tables(s, q_gain, HEAD_DIM ** -0.5 * math.log2(math.e))
    ak, bk = _rope_tables(s, k_gain, 1.0)
    head_mean = jnp.asarray(np.kron(np.eye(2), np.full((HEAD_DIM, HEAD_DIM), 1.0 / HEAD_DIM)), BF16)
    tok = lambda w: pl.BlockSpec((1, tm, w), lambda i, t: (i, t, 0))
    tr = pl.BlockSpec((1, KV_WIDTH, tm), lambda i, t: (i, 0, t))
    vec = pl.BlockSpec((1, 1, d), lambda i, t: (i, 0, 0))
    tab = pl.BlockSpec((tm, LANES), lambda i, t: (t, 0))
    sd = jax.ShapeDtypeStruct
    return pl.pallas_call(
        _inproj_kernel,
        grid=(b, s // tm),
        in_specs=[tok(d), vec, vec,
                  pl.BlockSpec((1, d), lambda i, t: (0, 0)),
                  pl.BlockSpec((d, IN_COLS), lambda i, t: (0, 0)),
                  tab, tab, tab, tab,
                  pl.BlockSpec((LANES, LANES), lambda i, t: (0, 0))],
        out_specs=[tok(A_WIDTH), tr, tok(KV_WIDTH), tok(B_WIDTH), tr, tok(KV_WIDTH), tok(d), tok(d)],
        out_shape=[sd((b, s, A_WIDTH), BF16), sd((b, KV_WIDTH, s), BF16), sd((b, s, KV_WIDTH), BF16),
                   sd((b, s, B_WIDTH), BF16), sd((b, KV_WIDTH, s), BF16), sd((b, s, KV_WIDTH), BF16),
                   sd((b, s, d), BF16), sd((b, s, d), BF16)],
        compiler_params=_params(("arbitrary", "arbitrary")),
        name="in_projection",
    )(x, sc1, sh1, g1.reshape(1, d), w_in, aq, bq, ak, bk, head_mean)


def _half_lane_variants(v, k_is_one):
    lane = lax.broadcasted_iota(I32, v.shape, 1)
    vr = pltpu.roll(v, HEAD_DIM, axis=1)
    own_lo = jnp.where(k_is_one, vr, v)
    own_hi = jnp.where(k_is_one, v, vr)
    lo = jnp.where(lane < HEAD_DIM, own_lo, 0.0)
    hi = jnp.where(lane >= HEAD_DIM, own_hi, 0.0)
    return lo, hi


_SUM_LANE_LO = LANES - 1
_SUM_LANE_HI = 0


def _gattn_kernel(q_ref, kt_ref, v_ref, o_ref, vv_ref, s_ref, qs_ref, os_ref, *, tu):
    k = pl.program_id(1)
    qi = pl.program_id(2)
    tq = q_ref.shape[1]
    group = A_HEADS // A_KV_HEADS
    nu = group * (tq // tu)

    @pl.when(qi == 0)
    def _():
        lo, hi = _half_lane_variants(v_ref[0].astype(F32), k == 1)
        lane = lax.broadcasted_iota(I32, lo.shape, 1)
        vv_ref[0] = jnp.where(lane == _SUM_LANE_LO, 1.0, lo).astype(BF16)
        vv_ref[1] = jnp.where(lane == _SUM_LANE_HI, 1.0, hi).astype(BF16)

    for u in range(nu):
        r, g = divmod(u, group)
        qs_ref[u] = q_ref[0, r * tu:(r + 1) * tu, g * HEAD_DIM:(g + 1) * HEAD_DIM]

    upb = s_ref.shape[0]

    def units(i, carry):
        maxes = []
        for j in range(upb):
            sc = _dot(qs_ref[upb * i + j], kt_ref[0])
            s_ref[j] = sc
            maxes.append(jnp.max(sc, axis=-1, keepdims=True))
        for j in range(upb):
            p = jnp.exp2(s_ref[j] - maxes[j])
            os_ref[upb * i + j] = _dot(p.astype(BF16), vv_ref[j % 2])
        return carry

    lax.fori_loop(0, nu // upb, units, 0)

    lane_o = lax.broadcasted_iota(I32, (tu, LANES), 1)
    for r in range(tq // tu):
        for j in range(group // 2):
            even = os_ref[r * group + 2 * j]
            odd = os_ref[r * group + 2 * j + 1]
            o = (jnp.where(lane_o < HEAD_DIM, even, 0.0) / even[:, _SUM_LANE_LO:_SUM_LANE_LO + 1]
                 + jnp.where(lane_o >= HEAD_DIM, odd, 0.0) / odd[:, _SUM_LANE_HI:_SUM_LANE_HI + 1])
            o_ref[0, r * tu:(r + 1) * tu, j * LANES:(j + 1) * LANES] = o.astype(o_ref.dtype)


def _global_attention(qa, kat, va, tq, tu):
    b, s, _ = qa.shape
    gw = A_WIDTH // A_KV_HEADS
    tq = min(tq, s)
    tu = min(tu, tq)
    nu = gw // HEAD_DIM * (tq // tu)
    upb = max(2, min(nu, LOGITS_VMEM_BUDGET // (tu * s * 4)))
    while nu % upb or upb % 2:
        upb -= 1
    return pl.pallas_call(
        functools.partial(_gattn_kernel, tu=tu),
        grid=(b, A_KV_HEADS, s // tq),
        in_specs=[pl.BlockSpec((1, tq, gw), lambda i, k, t: (i, t, k)),
                  pl.BlockSpec((1, HEAD_DIM, s), lambda i, k, t: (i, k, 0)),
                  pl.BlockSpec((1, s, KV_WIDTH), lambda i, k, t: (i, 0, 0))],
        out_specs=pl.BlockSpec((1, tq, gw), lambda i, k, t: (i, t, k)),
        out_shape=jax.ShapeDtypeStruct((b, s, A_WIDTH), BF16),
        scratch_shapes=[pltpu.VMEM((2, s, KV_WIDTH), BF16), pltpu.VMEM((upb, tu, s), F32),
                        pltpu.VMEM((nu, tu, HEAD_DIM), BF16), pltpu.VMEM((nu, tu, LANES), F32)],
        compiler_params=_params(("arbitrary", "arbitrary", "arbitrary")),
        name="global_attention",
    )(qa, kat, va)


def _t5_bucket_np(rel):
    half = N_BUCKETS // 2
    max_exact = half // 2
    base = (rel > 0).astype(np.int32) * half
    n = np.abs(rel)
    large = max_exact + (np.log(np.maximum(n, 1).astype(np.float32) / max_exact)
                         / math.log(MAX_DISTANCE / max_exact) * (half - max_exact)).astype(np.int32)
    large = np.minimum(large, half - 1)
    return base + np.where(n < max_exact, n, large)


def _stack_order(k):
    group = B_HEADS // B_KV_HEADS
    return [k * group + g for g in (0, 2, 1, 3)]


def _wattn_kernel(q_ref, ktp_ref, ktc_ref, ktn_ref, vp_ref, vc_ref, vn_ref, bias_ref, sink_ref, o_ref):
    t = pl.program_id(1)
    nt = pl.num_programs(1)
    span = Q_BLOCK + 2 * WINDOW
    nqb = q_ref.shape[1] // Q_BLOCK
    kt = jnp.concatenate([ktp_ref[0], ktc_ref[0], ktn_ref[0]], axis=1)
    v = jnp.concatenate([vp_ref[0], vc_ref[0], vn_ref[0]], axis=0).astype(F32)
    col = lax.broadcasted_iota(I32, (1, span), 1)
    first_ok = (col >= WINDOW) | (t > 0)
    last_ok = (col < WINDOW + Q_BLOCK) | (t < nt - 1)
    half_rows = 2 * Q_BLOCK
    values = []
    for k in range(B_KV_HEADS):
        lo, hi = _half_lane_variants(v, k == 1)
        values.append((lo.astype(BF16), hi.astype(BF16)))
    sinks = [sink_ref[k][:, 0:1] for k in range(B_KV_HEADS)]
    items = [(k, jb) for k in range(B_KV_HEADS) for jb in range(nqb)]
    keys = lambda jb: slice(jb * Q_BLOCK, jb * Q_BLOCK + span)
    rows = lambda jb: slice(jb * Q_BLOCK, (jb + 1) * Q_BLOCK)

    logits = []
    for k, jb in items:
        q4 = jnp.concatenate([q_ref[0, rows(jb), h * HEAD_DIM:(h + 1) * HEAD_DIM] for h in _stack_order(k)],
                             axis=0)
        lg = _dot(q4, kt[k * HEAD_DIM:(k + 1) * HEAD_DIM, keys(jb)]) + bias_ref[k]
        if jb == 0:
            lg = jnp.where(first_ok, lg, NEG_INF)
        if jb == nqb - 1:
            lg = jnp.where(last_ok, lg, NEG_INF)
        logits.append(lg)
    maxes = [jnp.maximum(jnp.max(lg, axis=-1, keepdims=True), sinks[k]) for (k, _), lg in zip(items, logits)]
    probs = [jnp.exp(lg - m) for lg, m in zip(logits, maxes)]
    dens = [jnp.sum(p, axis=-1, keepdims=True) + jnp.exp(sinks[k] - m)
            for (k, _), p, m in zip(items, probs, maxes)]
    outs = []
    for (k, jb), p, den in zip(items, probs, dens):
        pb = p.astype(BF16)
        vlo, vhi = values[k]
        o_even = _dot(pb[:half_rows], vlo[keys(jb)]) / den[:half_rows]
        o_odd = _dot(pb[half_rows:], vhi[keys(jb)]) / den[half_rows:]
        outs.append((o_even, o_odd))
    for (k, jb), (o_even, o_odd) in zip(items, outs):
        for j in range(2):
            pair = o_even[j * Q_BLOCK:(j + 1) * Q_BLOCK] + o_odd[j * Q_BLOCK:(j + 1) * Q_BLOCK]
            pj = 2 * k + j
            o_ref[0, rows(jb), pj * LANES:(pj + 1) * LANES] = pair.astype(o_ref.dtype)


def _window_attention(qb, kbt, vb, rel_bias, sink, tq):
    b, s, _ = qb.shape
    tq = min(tq, s)
    nb = s // Q_BLOCK
    per = tq // Q_BLOCK
    span = Q_BLOCK + 2 * WINDOW
    rel = np.arange(span)[None, :] - WINDOW - np.arange(Q_BLOCK)[:, None]
    band = np.abs(rel) <= WINDOW
    onehot = np.eye(N_BUCKETS, dtype=np.float32)[:, _t5_bucket_np(rel).reshape(-1)]
    bias = jnp.dot(rel_bias.astype(F32).T, jnp.asarray(onehot), precision=lax.Precision.HIGHEST)
    bias = jnp.where(jnp.asarray(band)[None], bias.reshape(B_HEADS, Q_BLOCK, span), NEG_INF)
    order = np.array([_stack_order(k) for k in range(B_KV_HEADS)])
    bias4 = bias[order].reshape(B_KV_HEADS, 4 * Q_BLOCK, span)
    sink4 = jnp.broadcast_to(sink.astype(F32)[order][:, :, None, None],
                             (B_KV_HEADS, 4, Q_BLOCK, LANES)).reshape(B_KV_HEADS, 4 * Q_BLOCK, LANES)
    prev = lambda i, t: jnp.maximum(t * per - 1, 0)
    nxt = lambda i, t: jnp.minimum((t + 1) * per, nb - 1)
    kt_edge = lambda f: pl.BlockSpec((1, KV_WIDTH, Q_BLOCK), lambda i, t: (i, 0, f(i, t)))
    v_edge = lambda f: pl.BlockSpec((1, Q_BLOCK, KV_WIDTH), lambda i, t: (i, f(i, t), 0))
    return pl.pallas_call(
        _wattn_kernel,
        grid=(b, s // tq),
        in_specs=[pl.BlockSpec((1, tq, B_WIDTH), lambda i, t: (i, t, 0)),
                  kt_edge(prev), pl.BlockSpec((1, KV_WIDTH, tq), lambda i, t: (i, 0, t)), kt_edge(nxt),
                  v_edge(prev), pl.BlockSpec((1, tq, KV_WIDTH), lambda i, t: (i, t, 0)), v_edge(nxt),
                  pl.BlockSpec((B_KV_HEADS, 4 * Q_BLOCK, span), lambda i, t: (0, 0, 0)),
                  pl.BlockSpec((B_KV_HEADS, 4 * Q_BLOCK, LANES), lambda i, t: (0, 0, 0))],
        out_specs=pl.BlockSpec((1, tq, B_WIDTH), lambda i, t: (i, t, 0)),
        out_shape=jax.ShapeDtypeStruct((b, s, B_WIDTH), BF16),
        compiler_params=_params(("arbitrary", "arbitrary")),
        name="window_attention",
    )(qb, kbt, kbt, kbt, vb, vb, vb, bias4, sink4)


def _pack_bf16_pairs(x):
    k = x.shape[1] // 2
    hi = pltpu.bitcast(x[:, :k].astype(BF16).astype(F32), U32)
    lo = pltpu.bitcast(x[:, k:].astype(BF16).astype(F32), U32)
    return hi | (lo >> 16)


def _unpack_bf16_pairs(w):
    hi = pltpu.bitcast(w & jnp.uint32(0xFFFF0000), F32)
    lo = pltpu.bitcast(w << 16, F32)
    return jnp.concatenate([hi, lo], axis=1).astype(BF16)


def _outproj_kernel(oa_ref, ob_ref, sga_ref, sgb_ref, x_ref, gt1_ref, sc2_ref, sh2_ref, g2_ref,
                    wa_ref, wb_ref, wo_ref, wr_ref, x1_ref, h2p_ref, afft_ref):
    tm = x_ref.shape[1]
    halves = [slice(0, tm // 2), slice(tm // 2, tm)]
    merged = []
    for r in halves:
        a = _dot(oa_ref[0, r], wa_ref[...])
        b = _dot(ob_ref[0, r], wb_ref[...])
        merged.append((sga_ref[0, r].astype(F32) * a + sgb_ref[0, r].astype(F32) * b).astype(BF16))
    h2s = []
    for r, mg in zip(halves, merged):
        x1 = x_ref[0, r] + gt1_ref[0] * _dot(mg, wo_ref[...])
        x1_ref[0, r] = x1
        ms = jnp.mean(x1 * x1, axis=-1, keepdims=True)
        h2s.append((x1 * lax.rsqrt(ms + EPS) * g2_ref[...]) * (1.0 + sc2_ref[0]) + sh2_ref[0])
    for r, h2 in zip(halves, h2s):
        h2p_ref[r] = _pack_bf16_pairs(h2).reshape(tm // 2, 1, PACKED)
        logits = _dot(h2.astype(BF16), wr_ref[...])
        lane = lax.broadcasted_iota(I32, logits.shape, 1)
        logits = jnp.where(lane < N_EXPERTS, logits, -jnp.inf)
        m = jnp.max(logits, axis=-1, keepdims=True)
        e = jnp.exp(logits - m)
        aff = e / jnp.sum(e, axis=-1, keepdims=True)
        afft_ref[:, r] = aff.T[:N_EXPERTS, :]


def _out_projection(oa, ob, sga, sgb, x, gt1, sc2, sh2, g2, wa, wb, wo, wr, tm):
    b, s, d = x.shape
    nt = s // tm
    n = b * s
    tok = lambda w: pl.BlockSpec((1, tm, w), lambda i, t: (i, t, 0))
    vec = pl.BlockSpec((1, 1, d), lambda i, t: (i, 0, 0))
    full = lambda r, c: pl.BlockSpec((r, c), lambda i, t: (0, 0))
    sd = jax.ShapeDtypeStruct
    return pl.pallas_call(
        _outproj_kernel,
        grid=(b, nt),
        in_specs=[tok(A_WIDTH), tok(B_WIDTH), tok(d), tok(d), tok(d), vec, vec, vec, full(1, d),
                  full(A_WIDTH, d), full(B_WIDTH, d), full(d, d), full(d, LANES)],
        out_specs=[tok(d),
                   pl.BlockSpec((tm, 1, PACKED), lambda i, t: (i * nt + t, 0, 0)),
                   pl.BlockSpec((N_EXPERTS, tm), lambda i, t: (0, i * nt + t))],
        out_shape=[sd((b, s, d), F32), sd((n, 1, PACKED), U32), sd((N_EXPERTS, n), F32)],
        compiler_params=_params(("arbitrary", "arbitrary")),
        name="out_projection",
    )(oa, ob, sga, sgb, x, gt1, sc2, sh2, g2.reshape(1, d), wa, wb, wo, wr)


def _threshold_kernel(aff_ref, thr_ref, cut_ref, *, cap):
    bits = pltpu.bitcast(aff_ref[...], I32)
    n = bits.shape[1]

    def value_step(i, lo):
        cand = lo | jnp.left_shift(jnp.int32(1), 30 - i)
        cnt = jnp.sum((bits >= cand).astype(F32), axis=1, keepdims=True)
        return jnp.where(cnt >= cap, cand, lo)

    thr = lax.fori_loop(0, 31, value_step, jnp.zeros((N_EXPERTS, 1), I32))
    need = cap - jnp.sum((bits > thr).astype(F32), axis=1, keepdims=True)
    eq = bits == thr
    tpos = lax.broadcasted_iota(I32, (1, n), 1)
    nbits = max(n.bit_length(), 1)

    def index_step(i, cut):
        cand = cut + jnp.left_shift(jnp.int32(1), nbits - 1 - i)
        below = jnp.sum(jnp.where(eq & (tpos < cand), 1.0, 0.0), axis=1, keepdims=True)
        return jnp.where((below <= need) & (cand <= n), cand, cut)

    cut = lax.fori_loop(0, nbits, index_step, jnp.zeros((N_EXPERTS, 1), I32))
    thr_ref[...] = jnp.broadcast_to(thr, thr_ref.shape)
    cut_ref[...] = jnp.broadcast_to(cut, cut_ref.shape)


def _capacity_threshold(afft, cap):
    e, n = afft.shape
    out = pl.BlockSpec((e, LANES), lambda i: (0, 0))
    return pl.pallas_call(
        functools.partial(_threshold_kernel, cap=cap),
        grid=(1,),
        in_specs=[pl.BlockSpec((e, n), lambda i: (0, 0))],
        out_specs=[out, out],
        out_shape=[jax.ShapeDtypeStruct((e, LANES), I32)] * 2,
        compiler_params=_params(("arbitrary",)),
        name="capacity_threshold",
    )(afft)


_AUX_ROWS = 8


def _prefix_kernel(aff_ref, thr_ref, cut_ref, u_ref, ls_ref, posm_ref, slot_ref, tokrow_ref, offs_ref,
                   run_ref, *, tb, ch):
    i = pl.program_id(0)

    @pl.when(i == 0)
    def _():
        run_ref[...] = jnp.zeros_like(run_ref)

    bits = pltpu.bitcast(aff_ref[...], I32)
    tpos = i * tb + lax.broadcasted_iota(I32, (1, tb), 1)
    thr = thr_ref[:, 0:1]
    sel = (bits > thr) | ((bits == thr) & (tpos < cut_ref[:, 0:1]))
    s = jnp.where(sel, 1.0, 0.0)
    for j in range(tb // ch):
        sj = s[:, j * ch:(j + 1) * ch]
        cntj = jnp.sum(sj, axis=0, keepdims=True)
        x = jnp.concatenate([sj, cntj, jnp.zeros((_AUX_ROWS - 1, ch), F32)], axis=0).astype(BF16)
        run = run_ref[:, 0:1]
        incl = _dot(x, u_ref[...]) + run
        posm_ref[:, j * ch:(j + 1) * ch] = jnp.where(sj > 0.0, incl[:N_EXPERTS] - sj, -1.0)
        tok_end = incl[N_EXPERTS:N_EXPERTS + 1]
        tok_off = tok_end - cntj
        slot_ref[:, j * ch:(j + 1) * ch] = tok_off + _dot(ls_ref[...], sj.astype(BF16))
        tokrow_ref[:, j * ch:(j + 1) * ch] = jnp.concatenate(
            [tok_off, tok_end, jnp.zeros((6, ch), F32)], axis=0)
        offs_ref[j] = jnp.broadcast_to(run[:N_EXPERTS], (N_EXPERTS, LANES))
        run_ref[...] = jnp.broadcast_to(incl[:, ch - 1:ch], run_ref.shape)


def _routing_prefix(afft, thr, cut, tb, ch):
    e, n = afft.shape
    upper = jnp.asarray(np.triu(np.ones((ch, ch), np.float32)), BF16)
    lower_strict = jnp.asarray(np.tril(np.ones((e, e), np.float32), -1), BF16)
    rows = lambda r: pl.BlockSpec((r, tb), lambda i: (0, i))
    const = lambda r, c: pl.BlockSpec((r, c), lambda i: (0, 0))
    sd = jax.ShapeDtypeStruct
    return pl.pallas_call(
        functools.partial(_prefix_kernel, tb=tb, ch=ch),
        grid=(n // tb,),
        in_specs=[rows(e), const(e, LANES), const(e, LANES), const(ch, ch), const(e, e)],
        out_specs=[rows(e), rows(e), rows(8), pl.BlockSpec((tb // ch, e, LANES), lambda i: (i, 0, 0))],
        out_shape=[sd((e, n), F32), sd((e, n), F32), sd((8, n), F32), sd((n // ch, e, LANES), F32)],
        scratch_shapes=[pltpu.VMEM((e + _AUX_ROWS, LANES), F32)],
        compiler_params=_params(("arbitrary",)),
        name="routing_prefix",
    )(afft, thr, cut, upper, lower_strict)


def _compact_kernel(win_ref, aff_ref, posm_ref, slot_ref, list_ref, *, tb, ch, nc):
    i = pl.program_id(0)

    @pl.when(i == 0)
    def _():
        list_ref[...] = jnp.zeros_like(list_ref)

    rank = lax.broadcasted_iota(I32, (2 * ch, ch), 0).astype(F32)
    tok_local = lax.broadcasted_iota(I32, (1, ch), 1).astype(F32)

    def expert_body(e, carry):
        for j in range(tb // ch):
            c = i * (tb // ch) + j
            w0 = win_ref[e * nc + c]
            lanes = slice(j * ch, (j + 1) * ch)
            rel = posm_ref[pl.ds(e, 1), lanes] - (w0 * ch).astype(F32)
            onehot_t = jnp.where(rank == rel, 1.0, 0.0).astype(BF16)
            slot = slot_ref[pl.ds(e, 1), lanes].astype(I32)
            g = aff_ref[pl.ds(e, 1), lanes]
            g_hi = g.astype(BF16).astype(F32)
            g_mid = (g - g_hi).astype(BF16).astype(F32)
            g_lo = g - g_hi - g_mid
            vals = jnp.concatenate(
                [tok_local, jnp.full((1, ch), c, I32).astype(F32),
                 (slot & 255).astype(F32), ((slot >> 8) & 255).astype(F32), (slot >> 16).astype(F32),
                 g_hi, g_mid, g_lo], axis=0).astype(BF16)
            out = lax.dot_general(vals, onehot_t, (((1,), (1,)), ((), ())), preferred_element_type=F32)
            list_ref[e, w0] += out[:, :ch]
            list_ref[e, w0 + 1] += out[:, ch:]
        return carry

    lax.fori_loop(0, N_EXPERTS, expert_body, 0)


def _routing_compact(win, afft, posm, slot, cap, tb, ch):
    e, n = afft.shape
    nc = n // ch
    nwin = cap // ch
    rows = pl.BlockSpec((e, tb), lambda i, w: (0, i))
    grid_spec = pltpu.PrefetchScalarGridSpec(
        num_scalar_prefetch=1,
        grid=(n // tb,),
        in_specs=[rows, rows, rows],
        out_specs=pl.BlockSpec((e, nwin + 2, 8, ch), lambda i, w: (0, 0, 0, 0)),
    )
    return pl.pallas_call(
        functools.partial(_compact_kernel, tb=tb, ch=ch, nc=nc),
        grid_spec=grid_spec,
        out_shape=jax.ShapeDtypeStruct((e, nwin + 2, 8, ch), F32),
        compiler_params=_params(("arbitrary",)),
        name="routing_compact",
    )(win, afft, posm, slot)


def _routing(afft, cap, tm, tt, blk, tb=2048, ch=256):
    e, n = afft.shape
    assert n // ch <= 256 and cap % ch == 0 and cap % tm == 0 and e * cap < (1 << 24)
    tb = min(tb, n)
    thr, cut = _capacity_threshold(afft, cap)
    posm, slot, tokrow, offs = _routing_prefix(afft, thr, cut, tb, ch)
    win = (offs[:, :, 0].astype(I32) // ch).T.reshape(-1)
    lists = _routing_compact(win, afft, posm, slot, cap, tb, ch)[:, :cap // ch]
    nsteps = e * cap // tm
    idx = (lists[:, :, 1] * ch + lists[:, :, 0]).astype(I32).reshape(nsteps, tm)
    dst = (lists[:, :, 2] + lists[:, :, 3] * 256.0 + lists[:, :, 4] * 65536.0).astype(I32).reshape(nsteps, tm)
    gate = ((lists[:, :, 5] + lists[:, :, 6]) + lists[:, :, 7]).reshape(nsteps, 1, tm)
    meta = jnp.concatenate([idx, dst], axis=1)
    gate8 = jnp.broadcast_to(gate, (nsteps, 8, tm))

    npairs = e * cap
    ntile, nblk = n // tt, npairs // blk
    start = tokrow[0, ::tt].astype(I32)
    end = jnp.concatenate([start[1:], jnp.full((1,), npairs, I32)])
    b0 = jnp.minimum(start // blk, nblk - 1)
    b1 = jnp.where(end > start, (end - 1) // blk, b0)
    nb = b1 - b0 + 1
    wend = jnp.cumsum(nb)
    woff = wend - nb
    w = jnp.arange(nblk + ntile, dtype=I32)
    wt = jnp.minimum(jnp.sum((w[:, None] >= wend[None, :]).astype(I32), axis=1), ntile - 1)
    valid = (w < wend[-1]).astype(I32)
    wblk = jnp.where(valid == 1, b0[wt] + (w - woff[wt]), b1[ntile - 1]).astype(I32)
    return meta, gate8, tokrow, wt, wblk, valid


def _moe_kernel(meta_hbm, gate_ref, h2_hbm, wg_ref, wu_ref, wd_ref, z_hbm,
                meta_smem, xbuf, x2d, zbuf, sem_m, sem_g, sem_s, *, tm, nsteps):
    nt = pl.num_programs(1)
    s = pl.program_id(0) * nt + pl.program_id(1)
    slot = s % 2

    def meta_copy(step, mslot):
        return pltpu.make_async_copy(meta_hbm.at[step], meta_smem.at[mslot], sem_m)

    def issue_gather(mslot, bslot):
        for r in range(tm):
            tok = meta_smem[mslot, r]
            pltpu.make_async_copy(h2_hbm.at[tok], xbuf.at[bslot, r], sem_g.at[bslot]).start(priority=r % 2)

    def wait_gather(bslot):
        pltpu.make_async_copy(h2_hbm.at[pl.ds(0, tm)], xbuf.at[bslot], sem_g.at[bslot]).wait()

    def issue_scatter(mslot, bslot):
        for r in range(tm):
            dst = meta_smem[mslot, tm + r]
            pltpu.make_async_copy(zbuf.at[bslot, r], z_hbm.at[dst], sem_s.at[bslot]).start(priority=r % 2)

    def wait_scatter(bslot):
        pltpu.make_async_copy(zbuf.at[bslot], z_hbm.at[pl.ds(0, tm)], sem_s.at[bslot]).wait()

    last = nsteps - 1

    @pl.when(s == 0)
    def _():
        for step, mslot in ((0, 0), (min(1, last), 1)):
            c = meta_copy(step, mslot)
            c.start()
            c.wait()
        for r in range(tm):
            meta_smem[3, tm + r] = nsteps * tm + r
        zbuf[1] = jnp.zeros(zbuf.shape[1:], zbuf.dtype)
        issue_gather(0, 0)

    wait_gather(slot)

    @pl.when(s >= 1)
    def _():
        wait_scatter(slot)

    x2d[...] = xbuf[slot].reshape(tm, PACKED)
    prefetch = meta_copy(jnp.minimum(s + 2, last), (s + 2) % 4)
    prefetch.start()
    issue_gather((s + 1) % 4, 1 - slot)
    issue_scatter((s + 3) % 4, 1 - slot)
    xe = _unpack_bf16_pairs(x2d[...])
    gate = _dot(xe, wg_ref[...])
    up = _dot(xe, wu_ref[...])
    hid = (gate * jax.nn.sigmoid(gate) * up).astype(BF16)
    gcol = gate_ref[...].T[:, 0:1]
    halves = [slice(0, tm // 2), slice(tm // 2, tm)]
    ye = [_dot(hid[h], wd_ref[...]) for h in halves]
    for h, y in zip(halves, ye):
        zbuf[slot, h] = _pack_bf16_pairs(y * gcol[h]).reshape(tm // 2, 1, PACKED)
    prefetch.wait()

    @pl.when(s == last)
    def _():
        issue_scatter(s % 4, slot)
        wait_scatter(slot)
        wait_scatter(1 - slot)
        wait_gather(1 - slot)


def _expert_mlp(meta, gate8, h2p, wg, wu, wd, tm):
    nsteps = meta.shape[0]
    nt = nsteps // N_EXPERTS
    d, f = wg.shape[1], wg.shape[2]
    step = lambda e, i: (e * nt + i, 0, 0)
    return pl.pallas_call(
        functools.partial(_moe_kernel, tm=tm, nsteps=nsteps),
        grid=(N_EXPERTS, nt),
        in_specs=[pl.BlockSpec(memory_space=pl.ANY),
                  pl.BlockSpec((None, 8, tm), step),
                  pl.BlockSpec(memory_space=pl.ANY),
                  pl.BlockSpec((None, d, f), lambda e, i: (e, 0, 0)),
                  pl.BlockSpec((None, d, f), lambda e, i: (e, 0, 0)),
                  pl.BlockSpec((None, f, d), lambda e, i: (e, 0, 0))],
        out_specs=pl.BlockSpec(memory_space=pl.ANY),
        out_shape=jax.ShapeDtypeStruct(((nsteps + 1) * tm, 1, PACKED), U32),
        scratch_shapes=[pltpu.SMEM((4, 2 * tm), I32),
                        pltpu.VMEM((2, tm, 1, PACKED), U32),
                        pltpu.VMEM((tm, PACKED), U32),
                        pltpu.VMEM((2, tm, 1, PACKED), U32),
                        pltpu.SemaphoreType.DMA,
                        pltpu.SemaphoreType.DMA((2,)),
                        pltpu.SemaphoreType.DMA((2,))],
        compiler_params=_params(("arbitrary", "arbitrary")),
        name="expert_mlp",
    )(meta, gate8, h2p, wg, wu, wd)


def _combine_kernel(wtile_ref, wblk_ref, wvalid_ref, z_ref, tokrow_ref, x1_ref, gt2_ref, gf_ref,
                    y_ref, acc_ref, z2d, *, tt):
    w = pl.program_id(0)
    nw = pl.num_programs(0)
    tile = wtile_ref[w]
    first = jnp.logical_or(w == 0, wtile_ref[jnp.maximum(w - 1, 0)] != tile)
    last = jnp.logical_or(w == nw - 1, wtile_ref[jnp.minimum(w + 1, nw - 1)] != tile)

    @pl.when(first)
    def _():
        acc_ref[...] = jnp.zeros_like(acc_ref)

    @pl.when(wvalid_ref[w] == 1)
    def _():
        blk = z2d.shape[0]
        z2d[...] = z_ref[...].reshape(z2d.shape)
        zrows = _unpack_bf16_pairs(z2d[...])
        pair = (wblk_ref[w] * blk + lax.broadcasted_iota(I32, (blk, tt), 0)).astype(F32)
        owned = (pair >= tokrow_ref[0:1, :]) & (pair < tokrow_ref[1:2, :])
        onehot = jnp.where(owned, 1.0, 0.0).T.astype(BF16)
        acc_ref[...] += _dot(onehot, zrows)

    @pl.when(last)
    def _():
        x2 = x1_ref[...] + gt2_ref[0] * acc_ref[...]
        ms = jnp.mean(x2 * x2, axis=-1, keepdims=True)
        y_ref[...] = x2 * lax.rsqrt(ms + EPS) * gf_ref[...]


def _combine(wtile, wblk, wvalid, z, tokrow, x1, gt2, gf, s, tt, blk):
    n, d = x1.shape
    nw = wtile.shape[0]
    grid_spec = pltpu.PrefetchScalarGridSpec(
        num_scalar_prefetch=3,
        grid=(nw,),
        in_specs=[pl.BlockSpec((blk, 1, PACKED), lambda w, wt, wb, wv: (wb[w], 0, 0)),
                  pl.BlockSpec((8, tt), lambda w, wt, wb, wv: (0, wt[w])),
                  pl.BlockSpec((tt, d), lambda w, wt, wb, wv: (wt[w], 0)),
                  pl.BlockSpec((1, 1, d), lambda w, wt, wb, wv: (wt[w] * tt // s, 0, 0)),
                  pl.BlockSpec((1, d), lambda w, wt, wb, wv: (0, 0))],
        out_specs=pl.BlockSpec((tt, d), lambda w, wt, wb, wv: (wt[w], 0)),
        scratch_shapes=[pltpu.VMEM((tt, d), F32), pltpu.VMEM((blk, PACKED), U32)],
    )
    return pl.pallas_call(
        functools.partial(_combine_kernel, tt=tt),
        grid_spec=grid_spec,
        out_shape=jax.ShapeDtypeStruct((n, d), F32),
        compiler_params=_params(("arbitrary",)),
        name="combine_final_norm",
    )(wtile, wblk, wvalid, z, tokrow, x1, gt2, gf.reshape(1, d))


def _trunk(x, mod, w, tm_proj=512, tq=512, tu=256, tq_win=1024, tm_moe=512, route_ch=256, tt=512, blk=512):
    b, s, d = x.shape
    n = b * s
    cap = CAPACITY_FACTOR * n // N_EXPERTS
    sh1, sc1, gt1, sh2, sc2, gt2 = [m.reshape(b, 1, d) for m in jnp.split(mod, N_MOD, axis=-1)]

    qa, kat, va, qb, kbt, vb, sga, sgb = _in_projection(
        x, sc1, sh1, w["norm1_g"], w["w_in"], w["q_norm_g"], w["k_norm_g"], tm_proj)
    oa = _global_attention(qa, kat, va, tq, tu)
    ob = _window_attention(qb, kbt, vb, w["rel_bias"], w["sink"], tq_win)
    x1, h2p, afft = _out_projection(oa, ob, sga, sgb, x, gt1, sc2, sh2, w["norm2_g"],
                                    w["w_branch_a"], w["w_branch_b"], w["w_out"], w["w_router"], tm_proj)
    meta, gate8, tokrow, wtile, wblk, wvalid = _routing(afft, cap, tm_moe, tt, blk, ch=route_ch)
    z = _expert_mlp(meta, gate8, h2p, w["w_e_gate"], w["w_e_up"], w["w_e_down"], tm_moe)
    y = _combine(wtile, wblk, wvalid, z, tokrow, x1.reshape(n, d), gt2, w["norm_f_g"], s, tt, blk)
    return y.reshape(b, s, d)


def kernel(x_prompt, x_sample, c_prompt, c_sample, w_ada, b_ada, norm1_g, w_in, q_norm_g, k_norm_g, sink,
           w_branch_a, w_branch_b, w_out, norm2_g, w_router, w_e_gate, w_e_up, w_e_down, rel_bias, norm_f_g):
    assert w_ada.shape[0] == 1, "single layer"
    bp, bs = c_prompt.shape[0], c_sample.shape[0]
    rows = -(-(bp + bs) // 16) * 16
    c = jnp.concatenate([c_prompt, c_sample, jnp.zeros((rows - bp - bs, D_MODEL), F32)], axis=0)
    mod = _modulation(c, w_ada[0], b_ada[0])

    router = jnp.zeros((D_MODEL, LANES), BF16).at[:, :N_EXPERTS].set(w_router[0].astype(BF16))
    w = dict(norm1_g=norm1_g[0], w_in=w_in[0].astype(BF16), q_norm_g=q_norm_g[0], k_norm_g=k_norm_g[0],
             sink=sink[0], w_branch_a=w_branch_a[0].astype(BF16), w_branch_b=w_branch_b[0].astype(BF16),
             w_out=w_out[0].astype(BF16), norm2_g=norm2_g[0], w_router=router,
             w_e_gate=w_e_gate[0].astype(BF16), w_e_up=w_e_up[0].astype(BF16),
             w_e_down=w_e_down[0].astype(BF16), rel_bias=rel_bias, norm_f_g=norm_f_g)
    y_prompt = _trunk(x_prompt, mod[:bp], w)
    y_sample = _trunk(x_sample, mod[bp:bp + bs], w)
    return (y_prompt, y_sample)
```

```python
import functools
import math

import jax
import jax.numpy as jnp
import numpy as np
from jax import lax
from jax.experimental import pallas as pl
from jax.experimental.pallas import tpu as pltpu

F32 = jnp.float32
BF16 = jnp.bfloat16
I32 = jnp.int32
U32 = jnp.uint32

D_MODEL = 1024
HEAD_DIM = 64
A_HEADS = 8
A_KV_HEADS = 2
B_HEADS = 8
B_KV_HEADS = 2
A_WIDTH = A_HEADS * HEAD_DIM
B_WIDTH = B_HEADS * HEAD_DIM
KV_WIDTH = A_KV_HEADS * HEAD_DIM
Q_BLOCK = 128
WINDOW = 128
GRID_W = 64
ROPE_THETA = 10000.0
AXIS_DIM = HEAD_DIM // 2
N_BUCKETS = 32
MAX_DISTANCE = 128
N_EXPERTS = 16
CAPACITY_FACTOR = 2
D_FF = 2048
N_MOD = 6
EPS = 1e-6
NEG_INF = -1e30
IN_COLS = A_WIDTH + 2 * KV_WIDTH + B_WIDTH + 2 * KV_WIDTH + 2 * D_MODEL

LANES = 128
PACKED = D_MODEL // 2
VMEM_LIMIT = 56 * 1024 * 1024
LOGITS_VMEM_BUDGET = 16 * 1024 * 1024

_QA, _KA, _VA = 0, A_WIDTH, A_WIDTH + KV_WIDTH
_QB = A_WIDTH + 2 * KV_WIDTH
_KB, _VB = _QB + B_WIDTH, _QB + B_WIDTH + KV_WIDTH
_GA = _QB + B_WIDTH + 2 * KV_WIDTH
_GB = _GA + D_MODEL


def _params(sem, vmem=VMEM_LIMIT):
    return pltpu.CompilerParams(dimension_semantics=sem, vmem_limit_bytes=vmem)


def _dot(a, b):
    return jnp.dot(a, b, preferred_element_type=F32)


def _mod_kernel(c_ref, w_ref, b_ref, o_ref):
    c = c_ref[...]
    s = c * jax.nn.sigmoid(c)
    o_ref[...] = _dot(s.astype(BF16), w_ref[...].astype(BF16)) + b_ref[...]


def _modulation(c, w_ada, b_ada):
    bp, d = c.shape
    n = w_ada.shape[1]
    tn = 1536
    return pl.pallas_call(
        _mod_kernel,
        grid=(n // tn,),
        in_specs=[pl.BlockSpec((bp, d), lambda j: (0, 0)),
                  pl.BlockSpec((d, tn), lambda j: (0, j)),
                  pl.BlockSpec((1, tn), lambda j: (0, j))],
        out_specs=pl.BlockSpec((bp, tn), lambda j: (0, j)),
        out_shape=jax.ShapeDtypeStruct((bp, n), F32),
        compiler_params=_params(("arbitrary",)),
        name="modulation",
    )(c, w_ada, b_ada.reshape(1, n))


def _swap16(x):
    n = x.shape[-1]
    left = pltpu.roll(x, n - 16, axis=1)
    right = pltpu.roll(x, 16, axis=1)
    lane = lax.broadcasted_iota(I32, x.shape, 1)
    return jnp.where((lane % 32) < 16, left, right)


def _inproj_kernel(x_ref, sc_ref, sh_ref, g1_ref, w_ref, aq_ref, bq_ref, ak_ref, bk_ref, hm_ref,
                   qa_ref, kat_ref, va_ref, qb_ref, kbt_ref, vb_ref, sga_ref, sgb_ref):
    tm = x_ref.shape[1]
    halves = [slice(0, tm // 2), slice(tm // 2, tm)]
    hs = []
    for r in halves:
        x = x_ref[0, r]
        ms = jnp.mean(x * x, axis=-1, keepdims=True)
        xn = x * lax.rsqrt(ms + EPS) * g1_ref[...]
        hs.append((xn * (1.0 + sc_ref[0]) + sh_ref[0]).astype(BF16))

    def proj(h, lo, width):
        return _dot(h, w_ref[:, lo:lo + width])

    def head_rms(q):
        width = q.shape[1]
        if width == LANES:
            msq = _dot((q * q).astype(BF16), hm_ref[:LANES, :LANES])
        else:
            msq = jnp.concatenate([_dot((q[:, j:j + 2 * LANES] * q[:, j:j + 2 * LANES]).astype(BF16), hm_ref[...])
                                   for j in range(0, width, 2 * LANES)], axis=1)
        return q * lax.rsqrt(msq + EPS)

    def rope(qh, a, b):
        reps = qh.shape[1] // LANES
        if reps > 1:
            a = jnp.concatenate([a] * reps, axis=1)
            b = jnp.concatenate([b] * reps, axis=1)
        return qh * a + _swap16(qh) * b

    for r, h in zip(halves, hs):
        sga_ref[0, r] = jax.nn.sigmoid(proj(h, _GA, D_MODEL)).astype(BF16)
    for r, h in zip(halves, hs):
        qa_ref[0, r] = rope(head_rms(proj(h, _QA, A_WIDTH)), aq_ref[r], bq_ref[r]).astype(BF16)
    for r, h in zip(halves, hs):
        kv = proj(h, _KA, 2 * KV_WIDTH)
        kat_ref[0, :, r] = rope(head_rms(kv[:, :KV_WIDTH]), ak_ref[r], bk_ref[r]).T.astype(BF16)
        va_ref[0, r] = kv[:, KV_WIDTH:].astype(BF16)
    for r, h in zip(halves, hs):
        qb_ref[0, r] = (proj(h, _QB, B_WIDTH) * (HEAD_DIM ** -0.5)).astype(BF16)
    for r, h in zip(halves, hs):
        kv = proj(h, _KB, 2 * KV_WIDTH)
        kbt_ref[0, :, r] = kv[:, :KV_WIDTH].T.astype(BF16)
        vb_ref[0, r] = kv[:, KV_WIDTH:].astype(BF16)
    for r, h in zip(halves, hs):
        sgb_ref[0, r] = jax.nn.sigmoid(proj(h, _GB, D_MODEL)).astype(BF16)


def _rope_tables(s, gain, scale):
    pos = jnp.arange(s, dtype=I32)
    row = (pos // GRID_W).astype(F32)
    col = (pos % GRID_W).astype(F32)
    inv_freq = ROPE_THETA ** (-jnp.arange(0, AXIS_DIM, 2, dtype=F32) / AXIS_DIM)
    ang_r = row[:, None] * inv_freq
    ang_c = col[:, None] * inv_freq
    cos64 = jnp.concatenate([jnp.cos(ang_r)] * 2 + [jnp.cos(ang_c)] * 2, axis=1)
    sin64 = jnp.concatenate([-jnp.sin(ang_r), jnp.sin(ang_r), -jnp.sin(ang_c), jnp.sin(ang_c)], axis=1)
    g = gain.astype(F32)
    gp = jnp.concatenate([g[16:32], g[0:16], g[48:64], g[32:48]])
    a = cos64 * g[None, :] * scale
    b = sin64 * gp[None, :] * scale
    return jnp.concatenate([a, a], axis=1), jnp.concatenate([b, b], axis=1)


def _in_projection(x, sc1, sh1, g1, w_in, q_gain, k_gain, tm):
    b, s, d = x.shape
    aq, bq = _rope_tables(s, q_gain, HEAD_DIM ** -0.5 * math.log2(math.e))
    ak, bk = _rope_tables(s, k_gain, 1.0)
    head_mean = jnp.asarray(np.kron(np.eye(2 * LANES // HEAD_DIM), np.full((HEAD_DIM, HEAD_DIM), 1.0 / HEAD_DIM)),
                            BF16)
    tok = lambda w: pl.BlockSpec((1, tm, w), lambda i, t: (i, t, 0))
    tr = pl.BlockSpec((1, KV_WIDTH, tm), lambda i, t: (i, 0, t))
    vec = pl.BlockSpec((1, 1, d), lambda i, t: (i, 0, 0))
    tab = pl.BlockSpec((tm, LANES), lambda i, t: (t, 0))
    sd = jax.ShapeDtypeStruct
    return pl.pallas_call(
        _inproj_kernel,
        grid=(b, s // tm),
        in_specs=[tok(d), vec, vec,
                  pl.BlockSpec((1, d), lambda i, t: (0, 0)),
                  pl.BlockSpec((d, IN_COLS), lambda i, t: (0, 0)),
                  tab, tab, tab, tab,
                  pl.BlockSpec((2 * LANES, 2 * LANES), lambda i, t: (0, 0))],
        out_specs=[tok(A_WIDTH), tr, tok(KV_WIDTH), tok(B_WIDTH), tr, tok(KV_WIDTH), tok(d), tok(d)],
        out_shape=[sd((b, s, A_WIDTH), BF16), sd((b, KV_WIDTH, s), BF16), sd((b, s, KV_WIDTH), BF16),
                   sd((b, s, B_WIDTH), BF16), sd((b, KV_WIDTH, s), BF16), sd((b, s, KV_WIDTH), BF16),
                   sd((b, s, d), BF16), sd((b, s, d), BF16)],
        compiler_params=_params(("arbitrary", "arbitrary")),
        name="in_projection",
    )(x, sc1, sh1, g1.reshape(1, d), w_in, aq, bq, ak, bk, head_mean)


def _half_lane_variants(v, k_is_one):
    lane = lax.broadcasted_iota(I32, v.shape, 1)
    vr = pltpu.roll(v, HEAD_DIM, axis=1)
    own_lo = jnp.where(k_is_one, vr, v)
    own_hi = jnp.where(k_is_one, v, vr)
    lo = jnp.where(lane < HEAD_DIM, own_lo, 0.0)
    hi = jnp.where(lane >= HEAD_DIM, own_hi, 0.0)
    return lo, hi


_SUM_LANE_LO = LANES - 1
_SUM_LANE_HI = 0


def _gattn_kernel(q_ref, kt_ref, v_ref, o_ref, vv_ref, s_ref, qs_ref, os_ref, *, tu):
    k = pl.program_id(1)
    qi = pl.program_id(2)
    tq = q_ref.shape[1]
    group = A_HEADS // A_KV_HEADS
    nu = group * (tq // tu)

    @pl.when(qi == 0)
    def _():
        lo, hi = _half_lane_variants(v_ref[0].astype(F32), k == 1)
        lane = lax.broadcasted_iota(I32, lo.shape, 1)
        vv_ref[0] = jnp.where(lane == _SUM_LANE_LO, 1.0, lo).astype(BF16)
        vv_ref[1] = jnp.where(lane == _SUM_LANE_HI, 1.0, hi).astype(BF16)

    for u in range(nu):
        r, g = divmod(u, group)
        qs_ref[u] = q_ref[0, r * tu:(r + 1) * tu, g * HEAD_DIM:(g + 1) * HEAD_DIM]

    upb = s_ref.shape[0]

    def units(i, carry):
        maxes = []
        for j in range(upb):
            sc = _dot(qs_ref[upb * i + j], kt_ref[0])
            s_ref[j] = sc
            maxes.append(jnp.max(sc, axis=-1, keepdims=True))
        for j in range(upb):
            p = jnp.exp2(s_ref[j] - maxes[j])
            os_ref[upb * i + j] = _dot(p.astype(BF16), vv_ref[j % 2])
        return carry

    lax.fori_loop(0, nu // upb, units, 0)

    lane_o = lax.broadcasted_iota(I32, (tu, LANES), 1)
    for r in range(tq // tu):
        for j in range(group // 2):
            even = os_ref[r * group + 2 * j]
            odd = os_ref[r * group + 2 * j + 1]
            o = (jnp.where(lane_o < HEAD_DIM, even, 0.0) / even[:, _SUM_LANE_LO:_SUM_LANE_LO + 1]
                 + jnp.where(lane_o >= HEAD_DIM, odd, 0.0) / odd[:, _SUM_LANE_HI:_SUM_LANE_HI + 1])
            o_ref[0, r * tu:(r + 1) * tu, j * LANES:(j + 1) * LANES] = o.astype(o_ref.dtype)


def _global_attention(qa, kat, va, tq, tu):
    b, s, _ = qa.shape
    gw = A_WIDTH // A_KV_HEADS
    tq = min(tq, s)
    tu = min(tu, tq)
    nu = gw // HEAD_DIM * (tq // tu)
    upb = max(2, min(nu, LOGITS_VMEM_BUDGET // (tu * s * 4)))
    while nu % upb or upb % 2:
        upb -= 1
    return pl.pallas_call(
        functools.partial(_gattn_kernel, tu=tu),
        grid=(b, A_KV_HEADS, s // tq),
        in_specs=[pl.BlockSpec((1, tq, gw), lambda i, k, t: (i, t, k)),
                  pl.BlockSpec((1, HEAD_DIM, s), lambda i, k, t: (i, k, 0)),
                  pl.BlockSpec((1, s, KV_WIDTH), lambda i, k, t: (i, 0, 0))],
        out_specs=pl.BlockSpec((1, tq, gw), lambda i, k, t: (i, t, k)),
        out_shape=jax.ShapeDtypeStruct((b, s, A_WIDTH), BF16),
        scratch_shapes=[pltpu.VMEM((2, s, KV_WIDTH), BF16), pltpu.VMEM((upb, tu, s), F32),
                        pltpu.VMEM((nu, tu, HEAD_DIM), BF16), pltpu.VMEM((nu, tu, LANES), F32)],
        compiler_params=_params(("arbitrary", "arbitrary", "arbitrary")),
        name="global_attention",
    )(qa, kat, va)


def _t5_bucket_np(rel):
    half = N_BUCKETS // 2
    max_exact = half // 2
    base = (rel > 0).astype(np.int32) * half
    n = np.abs(rel)
    large = max_exact + (np.log(np.maximum(n, 1).astype(np.float32) / max_exact)
                         / math.log(MAX_DISTANCE / max_exact) * (half - max_exact)).astype(np.int32)
    large = np.minimum(large, half - 1)
    return base + np.where(n < max_exact, n, large)


def _stack_order(k):
    group = B_HEADS // B_KV_HEADS
    return [k * group + g for g in (0, 2, 1, 3)]


def _wattn_kernel(q_ref, ktp_ref, ktc_ref, ktn_ref, vp_ref, vc_ref, vn_ref, bias_ref, sink_ref, o_ref):
    t = pl.program_id(1)
    nt = pl.num_programs(1)
    span = Q_BLOCK + 2 * WINDOW
    nqb = q_ref.shape[1] // Q_BLOCK
    kt = jnp.concatenate([ktp_ref[0], ktc_ref[0], ktn_ref[0]], axis=1)
    v = jnp.concatenate([vp_ref[0], vc_ref[0], vn_ref[0]], axis=0).astype(F32)
    col = lax.broadcasted_iota(I32, (1, span), 1)
    first_ok = (col >= WINDOW) | (t > 0)
    last_ok = (col < WINDOW + Q_BLOCK) | (t < nt - 1)
    half_rows = 2 * Q_BLOCK
    values = []
    for k in range(B_KV_HEADS):
        lo, hi = _half_lane_variants(v, k == 1)
        values.append((lo.astype(BF16), hi.astype(BF16)))
    sinks = [sink_ref[k][:, 0:1] for k in range(B_KV_HEADS)]
    items = [(k, jb) for k in range(B_KV_HEADS) for jb in range(nqb)]
    keys = lambda jb: slice(jb * Q_BLOCK, jb * Q_BLOCK + span)
    rows = lambda jb: slice(jb * Q_BLOCK, (jb + 1) * Q_BLOCK)

    logits = []
    for k, jb in items:
        q4 = jnp.concatenate([q_ref[0, rows(jb), h * HEAD_DIM:(h + 1) * HEAD_DIM] for h in _stack_order(k)],
                             axis=0)
        lg = _dot(q4, kt[k * HEAD_DIM:(k + 1) * HEAD_DIM, keys(jb)]) + bias_ref[k]
        if jb == 0:
            lg = jnp.where(first_ok, lg, NEG_INF)
        if jb == nqb - 1:
            lg = jnp.where(last_ok, lg, NEG_INF)
        logits.append(lg)
    maxes = [jnp.maximum(jnp.max(lg, axis=-1, keepdims=True), sinks[k]) for (k, _), lg in zip(items, logits)]
    probs = [jnp.exp(lg - m) for lg, m in zip(logits, maxes)]
    dens = [jnp.sum(p, axis=-1, keepdims=True) + jnp.exp(sinks[k] - m)
            for (k, _), p, m in zip(items, probs, maxes)]
    outs = []
    for (k, jb), p, den in zip(items, probs, dens):
        pb = p.astype(BF16)
        vlo, vhi = values[k]
        o_even = _dot(pb[:half_rows], vlo[keys(jb)]) / den[:half_rows]
        o_odd = _dot(pb[half_rows:], vhi[keys(jb)]) / den[half_rows:]
        outs.append((o_even, o_odd))
    for (k, jb), (o_even, o_odd) in zip(items, outs):
        for j in range(2):
            pair = o_even[j * Q_BLOCK:(j + 1) * Q_BLOCK] + o_odd[j * Q_BLOCK:(j + 1) * Q_BLOCK]
            pj = 2 * k + j
            o_ref[0, rows(jb), pj * LANES:(pj + 1) * LANES] = pair.astype(o_ref.dtype)


def _window_attention(qb, kbt, vb, rel_bias, sink, tq):
    b, s, _ = qb.shape
    tq = min(tq, s)
    nb = s // Q_BLOCK
    per = tq // Q_BLOCK
    span = Q_BLOCK + 2 * WINDOW
    rel = np.arange(span)[None, :] - WINDOW - np.arange(Q_BLOCK)[:, None]
    band = np.abs(rel) <= WINDOW
    onehot = np.eye(N_BUCKETS, dtype=np.float32)[:, _t5_bucket_np(rel).reshape(-1)]
    bias = jnp.dot(rel_bias.astype(F32).T, jnp.asarray(onehot), precision=lax.Precision.HIGHEST)
    bias = jnp.where(jnp.asarray(band)[None], bias.reshape(B_HEADS, Q_BLOCK, span), NEG_INF)
    order = np.array([_stack_order(k) for k in range(B_KV_HEADS)])
    bias4 = bias[order].reshape(B_KV_HEADS, 4 * Q_BLOCK, span)
    sink4 = jnp.broadcast_to(sink.astype(F32)[order][:, :, None, None],
                             (B_KV_HEADS, 4, Q_BLOCK, LANES)).reshape(B_KV_HEADS, 4 * Q_BLOCK, LANES)
    prev = lambda i, t: jnp.maximum(t * per - 1, 0)
    nxt = lambda i, t: jnp.minimum((t + 1) * per, nb - 1)
    kt_edge = lambda f: pl.BlockSpec((1, KV_WIDTH, Q_BLOCK), lambda i, t: (i, 0, f(i, t)))
    v_edge = lambda f: pl.BlockSpec((1, Q_BLOCK, KV_WIDTH), lambda i, t: (i, f(i, t), 0))
    return pl.pallas_call(
        _wattn_kernel,
        grid=(b, s // tq),
        in_specs=[pl.BlockSpec((1, tq, B_WIDTH), lambda i, t: (i, t, 0)),
                  kt_edge(prev), pl.BlockSpec((1, KV_WIDTH, tq), lambda i, t: (i, 0, t)), kt_edge(nxt),
                  v_edge(prev), pl.BlockSpec((1, tq, KV_WIDTH), lambda i, t: (i, t, 0)), v_edge(nxt),
                  pl.BlockSpec((B_KV_HEADS, 4 * Q_BLOCK, span), lambda i, t: (0, 0, 0)),
                  pl.BlockSpec((B_KV_HEADS, 4 * Q_BLOCK, LANES), lambda i, t: (0, 0, 0))],
        out_specs=pl.BlockSpec((1, tq, B_WIDTH), lambda i, t: (i, t, 0)),
        out_shape=jax.ShapeDtypeStruct((b, s, B_WIDTH), BF16),
        compiler_params=_params(("arbitrary", "arbitrary")),
        name="window_attention",
    )(qb, kbt, kbt, kbt, vb, vb, vb, bias4, sink4)


def _pack_bf16_pairs(x):
    k = x.shape[1] // 2
    hi = pltpu.bitcast(x[:, :k].astype(BF16).astype(F32), U32)
    lo = pltpu.bitcast(x[:, k:].astype(BF16).astype(F32), U32)
    return hi | (lo >> 16)


def _unpack_bf16_pairs(w):
    hi = pltpu.bitcast(w & jnp.uint32(0xFFFF0000), F32)
    lo = pltpu.bitcast(w << 16, F32)
    return jnp.concatenate([hi, lo], axis=1).astype(BF16)


def _outproj_kernel(oa_ref, ob_ref, sga_ref, sgb_ref, x_ref, gt1_ref, sc2_ref, sh2_ref, g2_ref,
                    wa_ref, wb_ref, wo_ref, wr_ref, x1_ref, h2p_ref, afft_ref):
    tm = x_ref.shape[1]
    halves = [slice(0, tm // 2), slice(tm // 2, tm)]
    merged = []
    for r in halves:
        a = _dot(oa_ref[0, r], wa_ref[...])
        b = _dot(ob_ref[0, r], wb_ref[...])
        merged.append((sga_ref[0, r].astype(F32) * a + sgb_ref[0, r].astype(F32) * b).astype(BF16))
    h2s = []
    for r, mg in zip(halves, merged):
        x1 = x_ref[0, r] + gt1_ref[0] * _dot(mg, wo_ref[...])
        x1_ref[0, r] = x1
        ms = jnp.mean(x1 * x1, axis=-1, keepdims=True)
        h2s.append((x1 * lax.rsqrt(ms + EPS) * g2_ref[...]) * (1.0 + sc2_ref[0]) + sh2_ref[0])
    for r, h2 in zip(halves, h2s):
        h2p_ref[r] = _pack_bf16_pairs(h2).reshape(tm // 2, 1, PACKED)
        logits = _dot(h2.astype(BF16), wr_ref[...])
        lane = lax.broadcasted_iota(I32, logits.shape, 1)
        logits = jnp.where(lane < N_EXPERTS, logits, -jnp.inf)
        m = jnp.max(logits, axis=-1, keepdims=True)
        e = jnp.exp(logits - m)
        aff = e / jnp.sum(e, axis=-1, keepdims=True)
        afft_ref[:, r] = aff.T[:N_EXPERTS, :]


def _out_projection(oa, ob, sga, sgb, x, gt1, sc2, sh2, g2, wa, wb, wo, wr, tm):
    b, s, d = x.shape
    nt = s // tm
    n = b * s
    tok = lambda w: pl.BlockSpec((1, tm, w), lambda i, t: (i, t, 0))
    vec = pl.BlockSpec((1, 1, d), lambda i, t: (i, 0, 0))
    full = lambda r, c: pl.BlockSpec((r, c), lambda i, t: (0, 0))
    sd = jax.ShapeDtypeStruct
    return pl.pallas_call(
        _outproj_kernel,
        grid=(b, nt),
        in_specs=[tok(A_WIDTH), tok(B_WIDTH), tok(d), tok(d), tok(d), vec, vec, vec, full(1, d),
                  full(A_WIDTH, d), full(B_WIDTH, d), full(d, d), full(d, LANES)],
        out_specs=[tok(d),
                   pl.BlockSpec((tm, 1, PACKED), lambda i, t: (i * nt + t, 0, 0)),
                   pl.BlockSpec((N_EXPERTS, tm), lambda i, t: (0, i * nt + t))],
        out_shape=[sd((b, s, d), F32), sd((n, 1, PACKED), U32), sd((N_EXPERTS, n), F32)],
        compiler_params=_params(("arbitrary", "arbitrary")),
        name="out_projection",
    )(oa, ob, sga, sgb, x, gt1, sc2, sh2, g2.reshape(1, d), wa, wb, wo, wr)


def _threshold_kernel(aff_ref, thr_ref, cut_ref, *, cap):
    bits = pltpu.bitcast(aff_ref[...], I32)
    n = bits.shape[1]

    def value_step(i, lo):
        cand = lo | jnp.left_shift(jnp.int32(1), 30 - i)
        cnt = jnp.sum((bits >= cand).astype(F32), axis=1, keepdims=True)
        return jnp.where(cnt >= cap, cand, lo)

    thr = lax.fori_loop(0, 31, value_step, jnp.zeros((N_EXPERTS, 1), I32))
    need = cap - jnp.sum((bits > thr).astype(F32), axis=1, keepdims=True)
    eq = bits == thr
    tpos = lax.broadcasted_iota(I32, (1, n), 1)
    nbits = max(n.bit_length(), 1)

    def index_step(i, cut):
        cand = cut + jnp.left_shift(jnp.int32(1), nbits - 1 - i)
        below = jnp.sum(jnp.where(eq & (tpos < cand), 1.0, 0.0), axis=1, keepdims=True)
        return jnp.where((below <= need) & (cand <= n), cand, cut)

    cut = lax.fori_loop(0, nbits, index_step, jnp.zeros((N_EXPERTS, 1), I32))
    thr_ref[...] = jnp.broadcast_to(thr, thr_ref.shape)
    cut_ref[...] = jnp.broadcast_to(cut, cut_ref.shape)


def _capacity_threshold(afft, cap):
    e, n = afft.shape
    out = pl.BlockSpec((e, LANES), lambda i: (0, 0))
    return pl.pallas_call(
        functools.partial(_threshold_kernel, cap=cap),
        grid=(1,),
        in_specs=[pl.BlockSpec((e, n), lambda i: (0, 0))],
        out_specs=[out, out],
        out_shape=[jax.ShapeDtypeStruct((e, LANES), I32)] * 2,
        compiler_params=_params(("arbitrary",)),
        name="capacity_threshold",
    )(afft)


_AUX_ROWS = 8


def _prefix_kernel(aff_ref, thr_ref, cut_ref, u_ref, ls_ref, posm_ref, slot_ref, tokrow_ref, offs_ref,
                   run_ref, *, tb, ch):
    i = pl.program_id(0)

    @pl.when(i == 0)
    def _():
        run_ref[...] = jnp.zeros_like(run_ref)

    bits = pltpu.bitcast(aff_ref[...], I32)
    tpos = i * tb + lax.broadcasted_iota(I32, (1, tb), 1)
    thr = thr_ref[:, 0:1]
    sel = (bits > thr) | ((bits == thr) & (tpos < cut_ref[:, 0:1]))
    s = jnp.where(sel, 1.0, 0.0)
    for j in range(tb // ch):
        sj = s[:, j * ch:(j + 1) * ch]
        cntj = jnp.sum(sj, axis=0, keepdims=True)
        x = jnp.concatenate([sj, cntj, jnp.zeros((_AUX_ROWS - 1, ch), F32)], axis=0).astype(BF16)
        run = run_ref[:, 0:1]
        incl = _dot(x, u_ref[...]) + run
        posm_ref[:, j * ch:(j + 1) * ch] = jnp.where(sj > 0.0, incl[:N_EXPERTS] - sj, -1.0)
        tok_end = incl[N_EXPERTS:N_EXPERTS + 1]
        tok_off = tok_end - cntj
        slot_ref[:, j * ch:(j + 1) * ch] = tok_off + _dot(ls_ref[...], sj.astype(BF16))
        tokrow_ref[:, j * ch:(j + 1) * ch] = jnp.concatenate(
            [tok_off, tok_end, jnp.zeros((6, ch), F32)], axis=0)
        offs_ref[j] = jnp.broadcast_to(run[:N_EXPERTS], (N_EXPERTS, LANES))
        run_ref[...] = jnp.broadcast_to(incl[:, ch - 1:ch], run_ref.shape)


def _routing_prefix(afft, thr, cut, tb, ch):
    e, n = afft.shape
    upper = jnp.asarray(np.triu(np.ones((ch, ch), np.float32)), BF16)
    lower_strict = jnp.asarray(np.tril(np.ones((e, e), np.float32), -1), BF16)
    rows = lambda r: pl.BlockSpec((r, tb), lambda i: (0, i))
    const = lambda r, c: pl.BlockSpec((r, c), lambda i: (0, 0))
    sd = jax.ShapeDtypeStruct
    return pl.pallas_call(
        functools.partial(_prefix_kernel, tb=tb, ch=ch),
        grid=(n // tb,),
        in_specs=[rows(e), const(e, LANES), const(e, LANES), const(ch, ch), const(e, e)],
        out_specs=[rows(e), rows(e), rows(8), pl.BlockSpec((tb // ch, e, LANES), lambda i: (i, 0, 0))],
        out_shape=[sd((e, n), F32), sd((e, n), F32), sd((8, n), F32), sd((n // ch, e, LANES), F32)],
        scratch_shapes=[pltpu.VMEM((e + _AUX_ROWS, LANES), F32)],
        compiler_params=_params(("arbitrary",)),
        name="routing_prefix",
    )(afft, thr, cut, upper, lower_strict)


def _compact_kernel(win_ref, aff_ref, posm_ref, slot_ref, list_ref, *, tb, ch, nc):
    i = pl.program_id(0)

    @pl.when(i == 0)
    def _():
        list_ref[...] = jnp.zeros_like(list_ref)

    rank = lax.broadcasted_iota(I32, (2 * ch, ch), 0).astype(F32)
    tok_local = lax.broadcasted_iota(I32, (1, ch), 1).astype(F32)

    def expert_body(e, carry):
        for j in range(tb // ch):
            c = i * (tb // ch) + j
            w0 = win_ref[e * nc + c]
            lanes = slice(j * ch, (j + 1) * ch)
            rel = posm_ref[pl.ds(e, 1), lanes] - (w0 * ch).astype(F32)
            onehot_t = jnp.where(rank == rel, 1.0, 0.0).astype(BF16)
            slot = slot_ref[pl.ds(e, 1), lanes].astype(I32)
            g = aff_ref[pl.ds(e, 1), lanes]
            g_hi = g.astype(BF16).astype(F32)
            g_mid = (g - g_hi).astype(BF16).astype(F32)
            g_lo = g - g_hi - g_mid
            vals = jnp.concatenate(
                [tok_local, jnp.full((1, ch), c, I32).astype(F32),
                 (slot & 255).astype(F32), ((slot >> 8) & 255).astype(F32), (slot >> 16).astype(F32),
                 g_hi, g_mid, g_lo], axis=0).astype(BF16)
            out = lax.dot_general(vals, onehot_t, (((1,), (1,)), ((), ())), preferred_element_type=F32)
            list_ref[e, w0] += out[:, :ch]
            list_ref[e, w0 + 1] += out[:, ch:]
        return carry

    lax.fori_loop(0, N_EXPERTS, expert_body, 0)


def _routing_compact(win, afft, posm, slot, cap, tb, ch):
    e, n = afft.shape
    nc = n // ch
    nwin = cap // ch
    rows = pl.BlockSpec((e, tb), lambda i, w: (0, i))
    grid_spec = pltpu.PrefetchScalarGridSpec(
        num_scalar_prefetch=1,
        grid=(n // tb,),
        in_specs=[rows, rows, rows],
        out_specs=pl.BlockSpec((e, nwin + 2, 8, ch), lambda i, w: (0, 0, 0, 0)),
    )
    return pl.pallas_call(
        functools.partial(_compact_kernel, tb=tb, ch=ch, nc=nc),
        grid_spec=grid_spec,
        out_shape=jax.ShapeDtypeStruct((e, nwin + 2, 8, ch), F32),
        compiler_params=_params(("arbitrary",)),
        name="routing_compact",
    )(win, afft, posm, slot)


def _routing(afft, cap, tm, tt, blk, tb=2048, ch=256):
    e, n = afft.shape
    assert n // ch <= 256 and cap % ch == 0 and cap % tm == 0 and e * cap < (1 << 24)
    tb = min(tb, n)
    thr, cut = _capacity_threshold(afft, cap)
    posm, slot, tokrow, offs = _routing_prefix(afft, thr, cut, tb, ch)
    win = (offs[:, :, 0].astype(I32) // ch).T.reshape(-1)
    lists = _routing_compact(win, afft, posm, slot, cap, tb, ch)[:, :cap // ch]
    nsteps = e * cap // tm
    idx = (lists[:, :, 1] * ch + lists[:, :, 0]).astype(I32).reshape(nsteps, tm)
    dst = (lists[:, :, 2] + lists[:, :, 3] * 256.0 + lists[:, :, 4] * 65536.0).astype(I32).reshape(nsteps, tm)
    gate = ((lists[:, :, 5] + lists[:, :, 6]) + lists[:, :, 7]).reshape(nsteps, 1, tm)
    meta = jnp.concatenate([idx, dst], axis=1)
    gate8 = jnp.broadcast_to(gate, (nsteps, 8, tm))

    npairs = e * cap
    ntile, nblk = n // tt, npairs // blk
    start = tokrow[0, ::tt].astype(I32)
    end = jnp.concatenate([start[1:], jnp.full((1,), npairs, I32)])
    b0 = jnp.minimum(start // blk, nblk - 1)
    b1 = jnp.where(end > start, (end - 1) // blk, b0)
    nb = b1 - b0 + 1
    wend = jnp.cumsum(nb)
    woff = wend - nb
    w = jnp.arange(nblk + ntile, dtype=I32)
    wt = jnp.minimum(jnp.sum((w[:, None] >= wend[None, :]).astype(I32), axis=1), ntile - 1)
    valid = (w < wend[-1]).astype(I32)
    wblk = jnp.where(valid == 1, b0[wt] + (w - woff[wt]), b1[ntile - 1]).astype(I32)
    return meta, gate8, tokrow, wt, wblk, valid


def _moe_kernel(meta_hbm, gate_ref, h2_hbm, wg_ref, wu_ref, wd_ref, z_hbm,
                meta_smem, xbuf, x2d, zbuf, sem_m, sem_g, sem_s, *, tm, nsteps):
    nt = pl.num_programs(1)
    s = pl.program_id(0) * nt + pl.program_id(1)
    slot = s % 2

    def meta_copy(step, mslot):
        return pltpu.make_async_copy(meta_hbm.at[step], meta_smem.at[mslot], sem_m)

    def issue_gather(mslot, bslot):
        for r in range(tm):
            tok = meta_smem[mslot, r]
            pltpu.make_async_copy(h2_hbm.at[tok], xbuf.at[bslot, r], sem_g.at[bslot]).start(priority=r % 2)

    def wait_gather(bslot):
        pltpu.make_async_copy(h2_hbm.at[pl.ds(0, tm)], xbuf.at[bslot], sem_g.at[bslot]).wait()

    def issue_scatter(mslot, bslot):
        for r in range(tm):
            dst = meta_smem[mslot, tm + r]
            pltpu.make_async_copy(zbuf.at[bslot, r], z_hbm.at[dst], sem_s.at[bslot]).start(priority=r % 2)

    def wait_scatter(bslot):
        pltpu.make_async_copy(zbuf.at[bslot], z_hbm.at[pl.ds(0, tm)], sem_s.at[bslot]).wait()

    last = nsteps - 1

    @pl.when(s == 0)
    def _():
        for step, mslot in ((0, 0), (min(1, last), 1)):
            c = meta_copy(step, mslot)
            c.start()
            c.wait()
        for r in range(tm):
            meta_smem[3, tm + r] = nsteps * tm + r
        zbuf[1] = jnp.zeros(zbuf.shape[1:], zbuf.dtype)
        issue_gather(0, 0)

    wait_gather(slot)

    @pl.when(s >= 1)
    def _():
        wait_scatter(slot)

    x2d[...] = xbuf[slot].reshape(tm, PACKED)
    prefetch = meta_copy(jnp.minimum(s + 2, last), (s + 2) % 4)
    prefetch.start()
    issue_gather((s + 1) % 4, 1 - slot)
    issue_scatter((s + 3) % 4, 1 - slot)
    xe = _unpack_bf16_pairs(x2d[...])
    gate = _dot(xe, wg_ref[...])
    up = _dot(xe, wu_ref[...])
    hid = (gate * jax.nn.sigmoid(gate) * up).astype(BF16)
    gcol = gate_ref[...].T[:, 0:1]
    halves = [slice(0, tm // 2), slice(tm // 2, tm)]
    ye = [_dot(hid[h], wd_ref[...]) for h in halves]
    for h, y in zip(halves, ye):
        zbuf[slot, h] = _pack_bf16_pairs(y * gcol[h]).reshape(tm // 2, 1, PACKED)
    prefetch.wait()

    @pl.when(s == last)
    def _():
        issue_scatter(s % 4, slot)
        wait_scatter(slot)
        wait_scatter(1 - slot)
        wait_gather(1 - slot)


def _expert_mlp(meta, gate8, h2p, wg, wu, wd, tm):
    nsteps = meta.shape[0]
    nt = nsteps // N_EXPERTS
    d, f = wg.shape[1], wg.shape[2]
    step = lambda e, i: (e * nt + i, 0, 0)
    return pl.pallas_call(
        functools.partial(_moe_kernel, tm=tm, nsteps=nsteps),
        grid=(N_EXPERTS, nt),
        in_specs=[pl.BlockSpec(memory_space=pl.ANY),
                  pl.BlockSpec((None, 8, tm), step),
                  pl.BlockSpec(memory_space=pl.ANY),
                  pl.BlockSpec((None, d, f), lambda e, i: (e, 0, 0)),
                  pl.BlockSpec((None, d, f), lambda e, i: (e, 0, 0)),
                  pl.BlockSpec((None, f, d), lambda e, i: (e, 0, 0))],
        out_specs=pl.BlockSpec(memory_space=pl.ANY),
        out_shape=jax.ShapeDtypeStruct(((nsteps + 1) * tm, 1, PACKED), U32),
        scratch_shapes=[pltpu.SMEM((4, 2 * tm), I32),
                        pltpu.VMEM((2, tm, 1, PACKED), U32),
                        pltpu.VMEM((tm, PACKED), U32),
                        pltpu.VMEM((2, tm, 1, PACKED), U32),
                        pltpu.SemaphoreType.DMA,
                        pltpu.SemaphoreType.DMA((2,)),
                        pltpu.SemaphoreType.DMA((2,))],
        compiler_params=_params(("arbitrary", "arbitrary")),
        name="expert_mlp",
    )(meta, gate8, h2p, wg, wu, wd)


def _combine_kernel(wtile_ref, wblk_ref, wvalid_ref, z_ref, tokrow_ref, x1_ref, gt2_ref, gf_ref,
                    y_ref, acc_ref, z2d, *, tt):
    w = pl.program_id(0)
    nw = pl.num_programs(0)
    tile = wtile_ref[w]
    first = jnp.logical_or(w == 0, wtile_ref[jnp.maximum(w - 1, 0)] != tile)
    last = jnp.logical_or(w == nw - 1, wtile_ref[jnp.minimum(w + 1, nw - 1)] != tile)

    @pl.when(first)
    def _():
        acc_ref[...] = jnp.zeros_like(acc_ref)

    @pl.when(wvalid_ref[w] == 1)
    def _():
        blk = z2d.shape[0]
        z2d[...] = z_ref[...].reshape(z2d.shape)
        zrows = _unpack_bf16_pairs(z2d[...])
        pair = (wblk_ref[w] * blk + lax.broadcasted_iota(I32, (blk, tt), 0)).astype(F32)
        owned = (pair >= tokrow_ref[0:1, :]) & (pair < tokrow_ref[1:2, :])
        onehot = jnp.where(owned, 1.0, 0.0).T.astype(BF16)
        acc_ref[...] += _dot(onehot, zrows)

    @pl.when(last)
    def _():
        x2 = x1_ref[...] + gt2_ref[0] * acc_ref[...]
        ms = jnp.mean(x2 * x2, axis=-1, keepdims=True)
        y_ref[...] = x2 * lax.rsqrt(ms + EPS) * gf_ref[...]


def _combine(wtile, wblk, wvalid, z, tokrow, x1, gt2, gf, s, tt, blk):
    n, d = x1.shape
    nw = wtile.shape[0]
    grid_spec = pltpu.PrefetchScalarGridSpec(
        num_scalar_prefetch=3,
        grid=(nw,),
        in_specs=[pl.BlockSpec((blk, 1, PACKED), lambda w, wt, wb, wv: (wb[w], 0, 0)),
                  pl.BlockSpec((8, tt), lambda w, wt, wb, wv: (0, wt[w])),
                  pl.BlockSpec((tt, d), lambda w, wt, wb, wv: (wt[w], 0)),
                  pl.BlockSpec((1, 1, d), lambda w, wt, wb, wv: (wt[w] * tt // s, 0, 0)),
                  pl.BlockSpec((1, d), lambda w, wt, wb, wv: (0, 0))],
        out_specs=pl.BlockSpec((tt, d), lambda w, wt, wb, wv: (wt[w], 0)),
        scratch_shapes=[pltpu.VMEM((tt, d), F32), pltpu.VMEM((blk, PACKED), U32)],
    )
    return pl.pallas_call(
        functools.partial(_combine_kernel, tt=tt),
        grid_spec=grid_spec,
        out_shape=jax.ShapeDtypeStruct((n, d), F32),
        compiler_params=_params(("arbitrary",)),
        name="combine_final_norm",
    )(wtile, wblk, wvalid, z, tokrow, x1, gt2, gf.reshape(1, d))


def _trunk(x, mod, w, tm_proj=512, tq=512, tu=256, tq_win=1024, tm_moe=512, route_ch=256, tt=512, blk=512):
    b, s, d = x.shape
    n = b * s
    cap = CAPACITY_FACTOR * n // N_EXPERTS
    sh1, sc1, gt1, sh2, sc2, gt2 = [m.reshape(b, 1, d) for m in jnp.split(mod, N_MOD, axis=-1)]

    qa, kat, va, qb, kbt, vb, sga, sgb = _in_projection(
        x, sc1, sh1, w["norm1_g"], w["w_in"], w["q_norm_g"], w["k_norm_g"], tm_proj)
    oa = _global_attention(qa, kat, va, tq, tu)
    ob = _window_attention(qb, kbt, vb, w["rel_bias"], w["sink"], tq_win)
    x1, h2p, afft = _out_projection(oa, ob, sga, sgb, x, gt1, sc2, sh2, w["norm2_g"],
                                    w["w_branch_a"], w["w_branch_b"], w["w_out"], w["w_router"], tm_proj)
    meta, gate8, tokrow, wtile, wblk, wvalid = _routing(afft, cap, tm_moe, tt, blk, ch=route_ch)
    z = _expert_mlp(meta, gate8, h2p, w["w_e_gate"], w["w_e_up"], w["w_e_down"], tm_moe)
    y = _combine(wtile, wblk, wvalid, z, tokrow, x1.reshape(n, d), gt2, w["norm_f_g"], s, tt, blk)
    return y.reshape(b, s, d)


def kernel(x_prompt, x_sample, c_prompt, c_sample, w_ada, b_ada, norm1_g, w_in, q_norm_g, k_norm_g, sink,
           w_branch_a, w_branch_b, w_out, norm2_g, w_router, w_e_gate, w_e_up, w_e_down, rel_bias, norm_f_g):
    assert w_ada.shape[0] == 1, "single layer"
    bp, bs = c_prompt.shape[0], c_sample.shape[0]
    rows = -(-(bp + bs) // 16) * 16
    c = jnp.concatenate([c_prompt, c_sample, jnp.zeros((rows - bp - bs, D_MODEL), F32)], axis=0)
    mod = _modulation(c, w_ada[0], b_ada[0])

    router = jnp.zeros((D_MODEL, LANES), BF16).at[:, :N_EXPERTS].set(w_router[0].astype(BF16))
    w = dict(norm1_g=norm1_g[0], w_in=w_in[0].astype(BF16), q_norm_g=q_norm_g[0], k_norm_g=k_norm_g[0],
             sink=sink[0], w_branch_a=w_branch_a[0].astype(BF16), w_branch_b=w_branch_b[0].astype(BF16),
             w_out=w_out[0].astype(BF16), norm2_g=norm2_g[0], w_router=router,
             w_e_gate=w_e_gate[0].astype(BF16), w_e_up=w_e_up[0].astype(BF16),
             w_e_down=w_e_down[0].astype(BF16), rel_bias=rel_bias, norm_f_g=norm_f_g)
    y_prompt = _trunk(x_prompt, mod[:bp], w)
    y_sample = _trunk(x_sample, mod[bp:bp + bs], w)
    return (y_prompt, y_sample)
```

```python
import functools
import math

import jax
import jax.numpy as jnp
import numpy as np
from jax import lax
from jax.experimental import pallas as pl
from jax.experimental.pallas import tpu as pltpu

F32 = jnp.float32
BF16 = jnp.bfloat16
I32 = jnp.int32
U32 = jnp.uint32

D_MODEL = 1024
HEAD_DIM = 64
A_HEADS = 8
A_KV_HEADS = 2
B_HEADS = 8
B_KV_HEADS = 2
A_WIDTH = A_HEADS * HEAD_DIM
B_WIDTH = B_HEADS * HEAD_DIM
KV_WIDTH = A_KV_HEADS * HEAD_DIM
Q_BLOCK = 128
WINDOW = 128
GRID_W = 64
ROPE_THETA = 10000.0
AXIS_DIM = HEAD_DIM // 2
N_BUCKETS = 32
MAX_DISTANCE = 128
N_EXPERTS = 16
CAPACITY_FACTOR = 2
D_FF = 2048
N_MOD = 6
EPS = 1e-6
NEG_INF = -1e30
IN_COLS = A_WIDTH + 2 * KV_WIDTH + B_WIDTH + 2 * KV_WIDTH + 2 * D_MODEL

LANES = 128
SUBLANES = 8
ROT_HALF = AXIS_DIM // 2
DIGIT_BITS = 8
DIGIT = 1 << DIGIT_BITS
PACKED = D_MODEL // 2
VMEM_LIMIT = 56 * 1024 * 1024
LOGITS_VMEM_BUDGET = 16 * 1024 * 1024

_QA, _KA, _VA = 0, A_WIDTH, A_WIDTH + KV_WIDTH
_QB = A_WIDTH + 2 * KV_WIDTH
_KB, _VB = _QB + B_WIDTH, _QB + B_WIDTH + KV_WIDTH
_GA = _QB + B_WIDTH + 2 * KV_WIDTH
_GB = _GA + D_MODEL


def _params(sem, vmem=VMEM_LIMIT):
    return pltpu.CompilerParams(dimension_semantics=sem, vmem_limit_bytes=vmem)


def _dot(a, b):
    return jnp.dot(a, b, preferred_element_type=F32)


def _mod_kernel(c_ref, w_ref, b_ref, o_ref):
    c = c_ref[...]
    s = c * jax.nn.sigmoid(c)
    o_ref[...] = _dot(s.astype(BF16), w_ref[...].astype(BF16)) + b_ref[...]


def _modulation(c, w_ada, b_ada):
    bp, d = c.shape
    n = w_ada.shape[1]
    tn = 1536
    return pl.pallas_call(
        _mod_kernel,
        grid=(n // tn,),
        in_specs=[pl.BlockSpec((bp, d), lambda j: (0, 0)),
                  pl.BlockSpec((d, tn), lambda j: (0, j)),
                  pl.BlockSpec((1, tn), lambda j: (0, j))],
        out_specs=pl.BlockSpec((bp, tn), lambda j: (0, j)),
        out_shape=jax.ShapeDtypeStruct((bp, n), F32),
        compiler_params=_params(("arbitrary",)),
        name="modulation",
    )(c, w_ada, b_ada.reshape(1, n))


def _swap16(x):
    n = x.shape[-1]
    left = pltpu.roll(x, n - ROT_HALF, axis=1)
    right = pltpu.roll(x, ROT_HALF, axis=1)
    lane = lax.broadcasted_iota(I32, x.shape, 1)
    return jnp.where((lane % AXIS_DIM) < ROT_HALF, left, right)


def _inproj_kernel(x_ref, sc_ref, sh_ref, g1_ref, w_ref, aq_ref, bq_ref, ak_ref, bk_ref, hm_ref,
                   qa_ref, kat_ref, va_ref, qb_ref, kbt_ref, vb_ref, sga_ref, sgb_ref):
    tm = x_ref.shape[1]
    halves = [slice(0, tm // 2), slice(tm // 2, tm)]
    hs = []
    for r in halves:
        x = x_ref[0, r]
        ms = jnp.mean(x * x, axis=-1, keepdims=True)
        xn = x * lax.rsqrt(ms + EPS) * g1_ref[...]
        hs.append((xn * (1.0 + sc_ref[0]) + sh_ref[0]).astype(BF16))

    def proj(h, lo, width):
        return _dot(h, w_ref[:, lo:lo + width])

    def head_rms(q):
        width = q.shape[1]
        if width == LANES:
            msq = _dot((q * q).astype(BF16), hm_ref[:LANES, :LANES])
        else:
            msq = jnp.concatenate([_dot((q[:, j:j + 2 * LANES] * q[:, j:j + 2 * LANES]).astype(BF16), hm_ref[...])
                                   for j in range(0, width, 2 * LANES)], axis=1)
        return q * lax.rsqrt(msq + EPS)

    def rope(qh, a, b):
        reps = qh.shape[1] // LANES
        if reps > 1:
            a = jnp.concatenate([a] * reps, axis=1)
            b = jnp.concatenate([b] * reps, axis=1)
        return qh * a + _swap16(qh) * b

    for r, h in zip(halves, hs):
        sga_ref[0, r] = jax.nn.sigmoid(proj(h, _GA, D_MODEL)).astype(BF16)
    for r, h in zip(halves, hs):
        qa_ref[0, r] = rope(head_rms(proj(h, _QA, A_WIDTH)), aq_ref[r], bq_ref[r]).astype(BF16)
    for r, h in zip(halves, hs):
        kv = proj(h, _KA, 2 * KV_WIDTH)
        kat_ref[0, :, r] = rope(head_rms(kv[:, :KV_WIDTH]), ak_ref[r], bk_ref[r]).T.astype(BF16)
        va_ref[0, r] = kv[:, KV_WIDTH:].astype(BF16)
    for r, h in zip(halves, hs):
        qb_ref[0, r] = (proj(h, _QB, B_WIDTH) * (HEAD_DIM ** -0.5)).astype(BF16)
    for r, h in zip(halves, hs):
        kv = proj(h, _KB, 2 * KV_WIDTH)
        kbt_ref[0, :, r] = kv[:, :KV_WIDTH].T.astype(BF16)
        vb_ref[0, r] = kv[:, KV_WIDTH:].astype(BF16)
    for r, h in zip(halves, hs):
        sgb_ref[0, r] = jax.nn.sigmoid(proj(h, _GB, D_MODEL)).astype(BF16)


def _rope_tables(s, gain, scale):
    pos = jnp.arange(s, dtype=I32)
    row = (pos // GRID_W).astype(F32)
    col = (pos % GRID_W).astype(F32)
    inv_freq = ROPE_THETA ** (-jnp.arange(0, AXIS_DIM, 2, dtype=F32) / AXIS_DIM)
    ang_r = row[:, None] * inv_freq
    ang_c = col[:, None] * inv_freq
    cos64 = jnp.concatenate([jnp.cos(ang_r)] * 2 + [jnp.cos(ang_c)] * 2, axis=1)
    sin64 = jnp.concatenate([-jnp.sin(ang_r), jnp.sin(ang_r), -jnp.sin(ang_c), jnp.sin(ang_c)], axis=1)
    g = gain.astype(F32)
    h, a = ROT_HALF, AXIS_DIM
    gp = jnp.concatenate([g[h:a], g[0:h], g[a + h:2 * a], g[a:a + h]])
    a = cos64 * g[None, :] * scale
    b = sin64 * gp[None, :] * scale
    return jnp.concatenate([a, a], axis=1), jnp.concatenate([b, b], axis=1)


def _in_projection(x, sc1, sh1, g1, w_in, q_gain, k_gain, tm):
    b, s, d = x.shape
    aq, bq = _rope_tables(s, q_gain, HEAD_DIM ** -0.5 * math.log2(math.e))
    ak, bk = _rope_tables(s, k_gain, 1.0)
    head_mean = jnp.asarray(np.kron(np.eye(2 * LANES // HEAD_DIM), np.full((HEAD_DIM, HEAD_DIM), 1.0 / HEAD_DIM)),
                            BF16)
    tok = lambda w: pl.BlockSpec((1, tm, w), lambda i, t: (i, t, 0))
    tr = pl.BlockSpec((1, KV_WIDTH, tm), lambda i, t: (i, 0, t))
    vec = pl.BlockSpec((1, 1, d), lambda i, t: (i, 0, 0))
    tab = pl.BlockSpec((tm, LANES), lambda i, t: (t, 0))
    sd = jax.ShapeDtypeStruct
    return pl.pallas_call(
        _inproj_kernel,
        grid=(b, s // tm),
        in_specs=[tok(d), vec, vec,
                  pl.BlockSpec((1, d), lambda i, t: (0, 0)),
                  pl.BlockSpec((d, IN_COLS), lambda i, t: (0, 0)),
                  tab, tab, tab, tab,
                  pl.BlockSpec((2 * LANES, 2 * LANES), lambda i, t: (0, 0))],
        out_specs=[tok(A_WIDTH), tr, tok(KV_WIDTH), tok(B_WIDTH), tr, tok(KV_WIDTH), tok(d), tok(d)],
        out_shape=[sd((b, s, A_WIDTH), BF16), sd((b, KV_WIDTH, s), BF16), sd((b, s, KV_WIDTH), BF16),
                   sd((b, s, B_WIDTH), BF16), sd((b, KV_WIDTH, s), BF16), sd((b, s, KV_WIDTH), BF16),
                   sd((b, s, d), BF16), sd((b, s, d), BF16)],
        compiler_params=_params(("arbitrary", "arbitrary")),
        name="in_projection",
    )(x, sc1, sh1, g1.reshape(1, d), w_in, aq, bq, ak, bk, head_mean)


def _half_lane_variants(v, k_is_one):
    lane = lax.broadcasted_iota(I32, v.shape, 1)
    vr = pltpu.roll(v, HEAD_DIM, axis=1)
    own_lo = jnp.where(k_is_one, vr, v)
    own_hi = jnp.where(k_is_one, v, vr)
    lo = jnp.where(lane < HEAD_DIM, own_lo, 0.0)
    hi = jnp.where(lane >= HEAD_DIM, own_hi, 0.0)
    return lo, hi


_SUM_LANE_LO = LANES - 1
_SUM_LANE_HI = 0


def _gattn_kernel(q_ref, kt_ref, v_ref, o_ref, vv_ref, s_ref, qs_ref, os_ref, *, tu):
    k = pl.program_id(1)
    qi = pl.program_id(2)
    tq = q_ref.shape[1]
    group = A_HEADS // A_KV_HEADS
    nu = group * (tq // tu)

    @pl.when(qi == 0)
    def _():
        lo, hi = _half_lane_variants(v_ref[0].astype(F32), k == 1)
        lane = lax.broadcasted_iota(I32, lo.shape, 1)
        vv_ref[0] = jnp.where(lane == _SUM_LANE_LO, 1.0, lo).astype(BF16)
        vv_ref[1] = jnp.where(lane == _SUM_LANE_HI, 1.0, hi).astype(BF16)

    for u in range(nu):
        r, g = divmod(u, group)
        qs_ref[u] = q_ref[0, r * tu:(r + 1) * tu, g * HEAD_DIM:(g + 1) * HEAD_DIM]

    upb = s_ref.shape[0]

    def units(i, carry):
        maxes = []
        for j in range(upb):
            sc = _dot(qs_ref[upb * i + j], kt_ref[0])
            s_ref[j] = sc
            maxes.append(jnp.max(sc, axis=-1, keepdims=True))
        for j in range(upb):
            p = jnp.exp2(s_ref[j] - maxes[j])
            os_ref[upb * i + j] = _dot(p.astype(BF16), vv_ref[j % 2])
        return carry

    lax.fori_loop(0, nu // upb, units, 0)

    lane_o = lax.broadcasted_iota(I32, (tu, LANES), 1)
    for r in range(tq // tu):
        for j in range(group // 2):
            even = os_ref[r * group + 2 * j]
            odd = os_ref[r * group + 2 * j + 1]
            o = (jnp.where(lane_o < HEAD_DIM, even, 0.0) / even[:, _SUM_LANE_LO:_SUM_LANE_LO + 1]
                 + jnp.where(lane_o >= HEAD_DIM, odd, 0.0) / odd[:, _SUM_LANE_HI:_SUM_LANE_HI + 1])
            o_ref[0, r * tu:(r + 1) * tu, j * LANES:(j + 1) * LANES] = o.astype(o_ref.dtype)


def _global_attention(qa, kat, va, tq, tu):
    b, s, _ = qa.shape
    gw = A_WIDTH // A_KV_HEADS
    tq = min(tq, s)
    tu = min(tu, tq)
    nu = gw // HEAD_DIM * (tq // tu)
    upb = max(2, min(nu, LOGITS_VMEM_BUDGET // (tu * s * 4)))
    while nu % upb or upb % 2:
        upb -= 1
    return pl.pallas_call(
        functools.partial(_gattn_kernel, tu=tu),
        grid=(b, A_KV_HEADS, s // tq),
        in_specs=[pl.BlockSpec((1, tq, gw), lambda i, k, t: (i, t, k)),
                  pl.BlockSpec((1, HEAD_DIM, s), lambda i, k, t: (i, k, 0)),
                  pl.BlockSpec((1, s, KV_WIDTH), lambda i, k, t: (i, 0, 0))],
        out_specs=pl.BlockSpec((1, tq, gw), lambda i, k, t: (i, t, k)),
        out_shape=jax.ShapeDtypeStruct((b, s, A_WIDTH), BF16),
        scratch_shapes=[pltpu.VMEM((2, s, KV_WIDTH), BF16), pltpu.VMEM((upb, tu, s), F32),
                        pltpu.VMEM((nu, tu, HEAD_DIM), BF16), pltpu.VMEM((nu, tu, LANES), F32)],
        compiler_params=_params(("arbitrary", "arbitrary", "arbitrary")),
        name="global_attention",
    )(qa, kat, va)


def _t5_bucket_np(rel):
    half = N_BUCKETS // 2
    max_exact = half // 2
    base = (rel > 0).astype(np.int32) * half
    n = np.abs(rel)
    large = max_exact + (np.log(np.maximum(n, 1).astype(np.float32) / max_exact)
                         / math.log(MAX_DISTANCE / max_exact) * (half - max_exact)).astype(np.int32)
    large = np.minimum(large, half - 1)
    return base + np.where(n < max_exact, n, large)


def _stack_order(k):
    group = B_HEADS // B_KV_HEADS
    return [k * group + g for g in (0, 2, 1, 3)]


def _wattn_kernel(q_ref, ktp_ref, ktc_ref, ktn_ref, vp_ref, vc_ref, vn_ref, bias_ref, sink_ref, o_ref):
    t = pl.program_id(1)
    nt = pl.num_programs(1)
    span = Q_BLOCK + 2 * WINDOW
    nqb = q_ref.shape[1] // Q_BLOCK
    kt = jnp.concatenate([ktp_ref[0], ktc_ref[0], ktn_ref[0]], axis=1)
    v = jnp.concatenate([vp_ref[0], vc_ref[0], vn_ref[0]], axis=0).astype(F32)
    col = lax.broadcasted_iota(I32, (1, span), 1)
    first_ok = (col >= WINDOW) | (t > 0)
    last_ok = (col < WINDOW + Q_BLOCK) | (t < nt - 1)
    half_rows = 2 * Q_BLOCK
    values = []
    for k in range(B_KV_HEADS):
        lo, hi = _half_lane_variants(v, k == 1)
        values.append((lo.astype(BF16), hi.astype(BF16)))
    sinks = [sink_ref[k][:, 0:1] for k in range(B_KV_HEADS)]
    items = [(k, jb) for k in range(B_KV_HEADS) for jb in range(nqb)]
    keys = lambda jb: slice(jb * Q_BLOCK, jb * Q_BLOCK + span)
    rows = lambda jb: slice(jb * Q_BLOCK, (jb + 1) * Q_BLOCK)

    logits = []
    for k, jb in items:
        q4 = jnp.concatenate([q_ref[0, rows(jb), h * HEAD_DIM:(h + 1) * HEAD_DIM] for h in _stack_order(k)],
                             axis=0)
        lg = _dot(q4, kt[k * HEAD_DIM:(k + 1) * HEAD_DIM, keys(jb)]) + bias_ref[k]
        if jb == 0:
            lg = jnp.where(first_ok, lg, NEG_INF)
        if jb == nqb - 1:
            lg = jnp.where(last_ok, lg, NEG_INF)
        logits.append(lg)
    maxes = [jnp.maximum(jnp.max(lg, axis=-1, keepdims=True), sinks[k]) for (k, _), lg in zip(items, logits)]
    probs = [jnp.exp(lg - m) for lg, m in zip(logits, maxes)]
    dens = [jnp.sum(p, axis=-1, keepdims=True) + jnp.exp(sinks[k] - m)
            for (k, _), p, m in zip(items, probs, maxes)]
    outs = []
    for (k, jb), p, den in zip(items, probs, dens):
        pb = p.astype(BF16)
        vlo, vhi = values[k]
        o_even = _dot(pb[:half_rows], vlo[keys(jb)]) / den[:half_rows]
        o_odd = _dot(pb[half_rows:], vhi[keys(jb)]) / den[half_rows:]
        outs.append((o_even, o_odd))
    for (k, jb), (o_even, o_odd) in zip(items, outs):
        for j in range(2):
            pair = o_even[j * Q_BLOCK:(j + 1) * Q_BLOCK] + o_odd[j * Q_BLOCK:(j + 1) * Q_BLOCK]
            pj = 2 * k + j
            o_ref[0, rows(jb), pj * LANES:(pj + 1) * LANES] = pair.astype(o_ref.dtype)


def _window_attention(qb, kbt, vb, rel_bias, sink, tq):
    b, s, _ = qb.shape
    tq = min(tq, s)
    nb = s // Q_BLOCK
    per = tq // Q_BLOCK
    span = Q_BLOCK + 2 * WINDOW
    rel = np.arange(span)[None, :] - WINDOW - np.arange(Q_BLOCK)[:, None]
    band = np.abs(rel) <= WINDOW
    onehot = np.eye(N_BUCKETS, dtype=np.float32)[:, _t5_bucket_np(rel).reshape(-1)]
    bias = jnp.dot(rel_bias.astype(F32).T, jnp.asarray(onehot), precision=lax.Precision.HIGHEST)
    bias = jnp.where(jnp.asarray(band)[None], bias.reshape(B_HEADS, Q_BLOCK, span), NEG_INF)
    order = np.array([_stack_order(k) for k in range(B_KV_HEADS)])
    bias4 = bias[order].reshape(B_KV_HEADS, 4 * Q_BLOCK, span)
    sink4 = jnp.broadcast_to(sink.astype(F32)[order][:, :, None, None],
                             (B_KV_HEADS, 4, Q_BLOCK, LANES)).reshape(B_KV_HEADS, 4 * Q_BLOCK, LANES)
    prev = lambda i, t: jnp.maximum(t * per - 1, 0)
    nxt = lambda i, t: jnp.minimum((t + 1) * per, nb - 1)
    kt_edge = lambda f: pl.BlockSpec((1, KV_WIDTH, Q_BLOCK), lambda i, t: (i, 0, f(i, t)))
    v_edge = lambda f: pl.BlockSpec((1, Q_BLOCK, KV_WIDTH), lambda i, t: (i, f(i, t), 0))
    return pl.pallas_call(
        _wattn_kernel,
        grid=(b, s // tq),
        in_specs=[pl.BlockSpec((1, tq, B_WIDTH), lambda i, t: (i, t, 0)),
                  kt_edge(prev), pl.BlockSpec((1, KV_WIDTH, tq), lambda i, t: (i, 0, t)), kt_edge(nxt),
                  v_edge(prev), pl.BlockSpec((1, tq, KV_WIDTH), lambda i, t: (i, t, 0)), v_edge(nxt),
                  pl.BlockSpec((B_KV_HEADS, 4 * Q_BLOCK, span), lambda i, t: (0, 0, 0)),
                  pl.BlockSpec((B_KV_HEADS, 4 * Q_BLOCK, LANES), lambda i, t: (0, 0, 0))],
        out_specs=pl.BlockSpec((1, tq, B_WIDTH), lambda i, t: (i, t, 0)),
        out_shape=jax.ShapeDtypeStruct((b, s, B_WIDTH), BF16),
        compiler_params=_params(("arbitrary", "arbitrary")),
        name="window_attention",
    )(qb, kbt, kbt, kbt, vb, vb, vb, bias4, sink4)


def _pack_bf16_pairs(x):
    k = x.shape[1] // 2
    hi = pltpu.bitcast(x[:, :k].astype(BF16).astype(F32), U32)
    lo = pltpu.bitcast(x[:, k:].astype(BF16).astype(F32), U32)
    return hi | (lo >> 16)


def _unpack_bf16_pairs(w):
    hi = pltpu.bitcast(w & jnp.uint32(0xFFFF0000), F32)
    lo = pltpu.bitcast(w << 16, F32)
    return jnp.concatenate([hi, lo], axis=1).astype(BF16)


def _outproj_kernel(oa_ref, ob_ref, sga_ref, sgb_ref, x_ref, gt1_ref, sc2_ref, sh2_ref, g2_ref,
                    wa_ref, wb_ref, wo_ref, wr_ref, x1_ref, h2p_ref, afft_ref):
    tm = x_ref.shape[1]
    halves = [slice(0, tm // 2), slice(tm // 2, tm)]
    merged = []
    for r in halves:
        a = _dot(oa_ref[0, r], wa_ref[...])
        b = _dot(ob_ref[0, r], wb_ref[...])
        merged.append((sga_ref[0, r].astype(F32) * a + sgb_ref[0, r].astype(F32) * b).astype(BF16))
    h2s = []
    for r, mg in zip(halves, merged):
        x1 = x_ref[0, r] + gt1_ref[0] * _dot(mg, wo_ref[...])
        x1_ref[0, r] = x1
        ms = jnp.mean(x1 * x1, axis=-1, keepdims=True)
        h2s.append((x1 * lax.rsqrt(ms + EPS) * g2_ref[...]) * (1.0 + sc2_ref[0]) + sh2_ref[0])
    for r, h2 in zip(halves, h2s):
        h2p_ref[r] = _pack_bf16_pairs(h2).reshape(tm // 2, 1, PACKED)
        logits = _dot(h2.astype(BF16), wr_ref[...])
        lane = lax.broadcasted_iota(I32, logits.shape, 1)
        logits = jnp.where(lane < N_EXPERTS, logits, -jnp.inf)
        m = jnp.max(logits, axis=-1, keepdims=True)
        e = jnp.exp(logits - m)
        aff = e / jnp.sum(e, axis=-1, keepdims=True)
        afft_ref[:, r] = aff.T[:N_EXPERTS, :]


def _out_projection(oa, ob, sga, sgb, x, gt1, sc2, sh2, g2, wa, wb, wo, wr, tm):
    b, s, d = x.shape
    nt = s // tm
    n = b * s
    tok = lambda w: pl.BlockSpec((1, tm, w), lambda i, t: (i, t, 0))
    vec = pl.BlockSpec((1, 1, d), lambda i, t: (i, 0, 0))
    full = lambda r, c: pl.BlockSpec((r, c), lambda i, t: (0, 0))
    sd = jax.ShapeDtypeStruct
    return pl.pallas_call(
        _outproj_kernel,
        grid=(b, nt),
        in_specs=[tok(A_WIDTH), tok(B_WIDTH), tok(d), tok(d), tok(d), vec, vec, vec, full(1, d),
                  full(A_WIDTH, d), full(B_WIDTH, d), full(d, d), full(d, LANES)],
        out_specs=[tok(d),
                   pl.BlockSpec((tm, 1, PACKED), lambda i, t: (i * nt + t, 0, 0)),
                   pl.BlockSpec((N_EXPERTS, tm), lambda i, t: (0, i * nt + t))],
        out_shape=[sd((b, s, d), F32), sd((n, 1, PACKED), U32), sd((N_EXPERTS, n), F32)],
        compiler_params=_params(("arbitrary", "arbitrary")),
        name="out_projection",
    )(oa, ob, sga, sgb, x, gt1, sc2, sh2, g2.reshape(1, d), wa, wb, wo, wr)


def _threshold_kernel(aff_ref, thr_ref, cut_ref, *, cap):
    bits = pltpu.bitcast(aff_ref[...], I32)
    n = bits.shape[1]

    def value_step(i, lo):
        cand = lo | jnp.left_shift(jnp.int32(1), 30 - i)
        cnt = jnp.sum((bits >= cand).astype(F32), axis=1, keepdims=True)
        return jnp.where(cnt >= cap, cand, lo)

    thr = lax.fori_loop(0, 31, value_step, jnp.zeros((N_EXPERTS, 1), I32))
    need = cap - jnp.sum((bits > thr).astype(F32), axis=1, keepdims=True)
    eq = bits == thr
    tpos = lax.broadcasted_iota(I32, (1, n), 1)
    nbits = max(n.bit_length(), 1)

    def index_step(i, cut):
        cand = cut + jnp.left_shift(jnp.int32(1), nbits - 1 - i)
        below = jnp.sum(jnp.where(eq & (tpos < cand), 1.0, 0.0), axis=1, keepdims=True)
        return jnp.where((below <= need) & (cand <= n), cand, cut)

    cut = lax.fori_loop(0, nbits, index_step, jnp.zeros((N_EXPERTS, 1), I32))
    thr_ref[...] = jnp.broadcast_to(thr, thr_ref.shape)
    cut_ref[...] = jnp.broadcast_to(cut, cut_ref.shape)


def _capacity_threshold(afft, cap):
    e, n = afft.shape
    out = pl.BlockSpec((e, LANES), lambda i: (0, 0))
    return pl.pallas_call(
        functools.partial(_threshold_kernel, cap=cap),
        grid=(1,),
        in_specs=[pl.BlockSpec((e, n), lambda i: (0, 0))],
        out_specs=[out, out],
        out_shape=[jax.ShapeDtypeStruct((e, LANES), I32)] * 2,
        compiler_params=_params(("arbitrary",)),
        name="capacity_threshold",
    )(afft)


_AUX_ROWS = SUBLANES


def _prefix_kernel(aff_ref, thr_ref, cut_ref, u_ref, ls_ref, posm_ref, slot_ref, tokrow_ref, offs_ref,
                   run_ref, *, tb, ch):
    i = pl.program_id(0)

    @pl.when(i == 0)
    def _():
        run_ref[...] = jnp.zeros_like(run_ref)

    bits = pltpu.bitcast(aff_ref[...], I32)
    tpos = i * tb + lax.broadcasted_iota(I32, (1, tb), 1)
    thr = thr_ref[:, 0:1]
    sel = (bits > thr) | ((bits == thr) & (tpos < cut_ref[:, 0:1]))
    s = jnp.where(sel, 1.0, 0.0)
    for j in range(tb // ch):
        sj = s[:, j * ch:(j + 1) * ch]
        cntj = jnp.sum(sj, axis=0, keepdims=True)
        x = jnp.concatenate([sj, cntj, jnp.zeros((_AUX_ROWS - 1, ch), F32)], axis=0).astype(BF16)
        run = run_ref[:, 0:1]
        incl = _dot(x, u_ref[...]) + run
        posm_ref[:, j * ch:(j + 1) * ch] = jnp.where(sj > 0.0, incl[:N_EXPERTS] - sj, -1.0)
        tok_end = incl[N_EXPERTS:N_EXPERTS + 1]
        tok_off = tok_end - cntj
        slot_ref[:, j * ch:(j + 1) * ch] = tok_off + _dot(ls_ref[...], sj.astype(BF16))
        tokrow_ref[:, j * ch:(j + 1) * ch] = jnp.concatenate(
            [tok_off, tok_end, jnp.zeros((6, ch), F32)], axis=0)
        offs_ref[j] = jnp.broadcast_to(run[:N_EXPERTS], (N_EXPERTS, LANES))
        run_ref[...] = jnp.broadcast_to(incl[:, ch - 1:ch], run_ref.shape)


def _routing_prefix(afft, thr, cut, tb, ch):
    e, n = afft.shape
    upper = jnp.asarray(np.triu(np.ones((ch, ch), np.float32)), BF16)
    lower_strict = jnp.asarray(np.tril(np.ones((e, e), np.float32), -1), BF16)
    rows = lambda r: pl.BlockSpec((r, tb), lambda i: (0, i))
    const = lambda r, c: pl.BlockSpec((r, c), lambda i: (0, 0))
    sd = jax.ShapeDtypeStruct
    return pl.pallas_call(
        functools.partial(_prefix_kernel, tb=tb, ch=ch),
        grid=(n // tb,),
        in_specs=[rows(e), const(e, LANES), const(e, LANES), const(ch, ch), const(e, e)],
        out_specs=[rows(e), rows(e), rows(8), pl.BlockSpec((tb // ch, e, LANES), lambda i: (i, 0, 0))],
        out_shape=[sd((e, n), F32), sd((e, n), F32), sd((8, n), F32), sd((n // ch, e, LANES), F32)],
        scratch_shapes=[pltpu.VMEM((e + _AUX_ROWS, LANES), F32)],
        compiler_params=_params(("arbitrary",)),
        name="routing_prefix",
    )(afft, thr, cut, upper, lower_strict)


def _compact_kernel(win_ref, aff_ref, posm_ref, slot_ref, list_ref, *, tb, ch, nc):
    i = pl.program_id(0)

    @pl.when(i == 0)
    def _():
        list_ref[...] = jnp.zeros_like(list_ref)

    rank = lax.broadcasted_iota(I32, (2 * ch, ch), 0).astype(F32)
    tok_local = lax.broadcasted_iota(I32, (1, ch), 1).astype(F32)

    def expert_body(e, carry):
        for j in range(tb // ch):
            c = i * (tb // ch) + j
            w0 = win_ref[e * nc + c]
            lanes = slice(j * ch, (j + 1) * ch)
            rel = posm_ref[pl.ds(e, 1), lanes] - (w0 * ch).astype(F32)
            onehot_t = jnp.where(rank == rel, 1.0, 0.0).astype(BF16)
            slot = slot_ref[pl.ds(e, 1), lanes].astype(I32)
            g = aff_ref[pl.ds(e, 1), lanes]
            g_hi = g.astype(BF16).astype(F32)
            g_mid = (g - g_hi).astype(BF16).astype(F32)
            g_lo = g - g_hi - g_mid
            vals = jnp.concatenate(
                [tok_local, jnp.full((1, ch), c, I32).astype(F32),
                 (slot & (DIGIT - 1)).astype(F32), ((slot >> DIGIT_BITS) & (DIGIT - 1)).astype(F32),
                 (slot >> (2 * DIGIT_BITS)).astype(F32),
                 g_hi, g_mid, g_lo], axis=0).astype(BF16)
            out = lax.dot_general(vals, onehot_t, (((1,), (1,)), ((), ())), preferred_element_type=F32)
            list_ref[e, w0] += out[:, :ch]
            list_ref[e, w0 + 1] += out[:, ch:]
        return carry

    lax.fori_loop(0, N_EXPERTS, expert_body, 0)


def _routing_compact(win, afft, posm, slot, cap, tb, ch):
    e, n = afft.shape
    nc = n // ch
    nwin = cap // ch
    rows = pl.BlockSpec((e, tb), lambda i, w: (0, i))
    grid_spec = pltpu.PrefetchScalarGridSpec(
        num_scalar_prefetch=1,
        grid=(n // tb,),
        in_specs=[rows, rows, rows],
        out_specs=pl.BlockSpec((e, nwin + 2, 8, ch), lambda i, w: (0, 0, 0, 0)),
    )
    return pl.pallas_call(
        functools.partial(_compact_kernel, tb=tb, ch=ch, nc=nc),
        grid_spec=grid_spec,
        out_shape=jax.ShapeDtypeStruct((e, nwin + 2, 8, ch), F32),
        compiler_params=_params(("arbitrary",)),
        name="routing_compact",
    )(win, afft, posm, slot)


def _routing(afft, cap, tm, tt, blk, tb=2048, ch=256):
    e, n = afft.shape
    assert n // ch <= DIGIT and ch <= DIGIT and cap % ch == 0 and cap % tm == 0 and e * cap < (1 << 24)
    tb = min(tb, n)
    thr, cut = _capacity_threshold(afft, cap)
    posm, slot, tokrow, offs = _routing_prefix(afft, thr, cut, tb, ch)
    win = (offs[:, :, 0].astype(I32) // ch).T.reshape(-1)
    lists = _routing_compact(win, afft, posm, slot, cap, tb, ch)[:, :cap // ch]
    nsteps = e * cap // tm
    idx = (lists[:, :, 1] * ch + lists[:, :, 0]).astype(I32).reshape(nsteps, tm)
    dst = (lists[:, :, 2] + lists[:, :, 3] * float(DIGIT) + lists[:, :, 4] * float(DIGIT * DIGIT)).astype(I32)
    dst = dst.reshape(nsteps, tm)
    gate = ((lists[:, :, 5] + lists[:, :, 6]) + lists[:, :, 7]).reshape(nsteps, 1, tm)
    meta = jnp.concatenate([idx, dst], axis=1)
    gate8 = jnp.broadcast_to(gate, (nsteps, SUBLANES, tm))

    npairs = e * cap
    ntile, nblk = n // tt, npairs // blk
    start = tokrow[0, ::tt].astype(I32)
    end = jnp.concatenate([start[1:], jnp.full((1,), npairs, I32)])
    b0 = jnp.minimum(start // blk, nblk - 1)
    b1 = jnp.where(end > start, (end - 1) // blk, b0)
    nb = b1 - b0 + 1
    wend = jnp.cumsum(nb)
    woff = wend - nb
    w = jnp.arange(nblk + ntile, dtype=I32)
    wt = jnp.minimum(jnp.sum((w[:, None] >= wend[None, :]).astype(I32), axis=1), ntile - 1)
    valid = (w < wend[-1]).astype(I32)
    wblk = jnp.where(valid == 1, b0[wt] + (w - woff[wt]), b1[ntile - 1]).astype(I32)
    return meta, gate8, tokrow, wt, wblk, valid


def _moe_kernel(meta_hbm, gate_ref, h2_hbm, wg_ref, wu_ref, wd_ref, z_hbm,
                meta_smem, xbuf, x2d, zbuf, sem_m, sem_g, sem_s, *, tm, nsteps):
    nt = pl.num_programs(1)
    s = pl.program_id(0) * nt + pl.program_id(1)
    slot = s % 2

    def meta_copy(step, mslot):
        return pltpu.make_async_copy(meta_hbm.at[step], meta_smem.at[mslot], sem_m)

    def issue_gather(mslot, bslot):
        for r in range(tm):
            tok = meta_smem[mslot, r]
            pltpu.make_async_copy(h2_hbm.at[tok], xbuf.at[bslot, r], sem_g.at[bslot]).start(priority=r % 2)

    def wait_gather(bslot):
        pltpu.make_async_copy(h2_hbm.at[pl.ds(0, tm)], xbuf.at[bslot], sem_g.at[bslot]).wait()

    def issue_scatter(mslot, bslot):
        for r in range(tm):
            dst = meta_smem[mslot, tm + r]
            pltpu.make_async_copy(zbuf.at[bslot, r], z_hbm.at[dst], sem_s.at[bslot]).start(priority=r % 2)

    def wait_scatter(bslot):
        pltpu.make_async_copy(zbuf.at[bslot], z_hbm.at[pl.ds(0, tm)], sem_s.at[bslot]).wait()

    last = nsteps - 1

    @pl.when(s == 0)
    def _():
        for step, mslot in ((0, 0), (min(1, last), 1)):
            c = meta_copy(step, mslot)
            c.start()
            c.wait()
        for r in range(tm):
            meta_smem[3, tm + r] = nsteps * tm + r
        zbuf[1] = jnp.zeros(zbuf.shape[1:], zbuf.dtype)
        issue_gather(0, 0)

    wait_gather(slot)

    @pl.when(s >= 1)
    def _():
        wait_scatter(slot)

    x2d[...] = xbuf[slot].reshape(tm, PACKED)
    prefetch = meta_copy(jnp.minimum(s + 2, last), (s + 2) % 4)
    prefetch.start()
    issue_gather((s + 1) % 4, 1 - slot)
    issue_scatter((s + 3) % 4, 1 - slot)
    xe = _unpack_bf16_pairs(x2d[...])
    gate = _dot(xe, wg_ref[...])
    up = _dot(xe, wu_ref[...])
    hid = (gate * jax.nn.sigmoid(gate) * up).astype(BF16)
    gcol = gate_ref[...].T[:, 0:1]
    halves = [slice(0, tm // 2), slice(tm // 2, tm)]
    ye = [_dot(hid[h], wd_ref[...]) for h in halves]
    for h, y in zip(halves, ye):
        zbuf[slot, h] = _pack_bf16_pairs(y * gcol[h]).reshape(tm // 2, 1, PACKED)
    prefetch.wait()

    @pl.when(s == last)
    def _():
        issue_scatter(s % 4, slot)
        wait_scatter(slot)
        wait_scatter(1 - slot)
        wait_gather(1 - slot)


def _expert_mlp(meta, gate8, h2p, wg, wu, wd, tm):
    nsteps = meta.shape[0]
    nt = nsteps // N_EXPERTS
    d, f = wg.shape[1], wg.shape[2]
    step = lambda e, i: (e * nt + i, 0, 0)
    return pl.pallas_call(
        functools.partial(_moe_kernel, tm=tm, nsteps=nsteps),
        grid=(N_EXPERTS, nt),
        in_specs=[pl.BlockSpec(memory_space=pl.ANY),
                  pl.BlockSpec((None, SUBLANES, tm), step),
                  pl.BlockSpec(memory_space=pl.ANY),
                  pl.BlockSpec((None, d, f), lambda e, i: (e, 0, 0)),
                  pl.BlockSpec((None, d, f), lambda e, i: (e, 0, 0)),
                  pl.BlockSpec((None, f, d), lambda e, i: (e, 0, 0))],
        out_specs=pl.BlockSpec(memory_space=pl.ANY),
        out_shape=jax.ShapeDtypeStruct(((nsteps + 1) * tm, 1, PACKED), U32),
        scratch_shapes=[pltpu.SMEM((4, 2 * tm), I32),
                        pltpu.VMEM((2, tm, 1, PACKED), U32),
                        pltpu.VMEM((tm, PACKED), U32),
                        pltpu.VMEM((2, tm, 1, PACKED), U32),
                        pltpu.SemaphoreType.DMA,
                        pltpu.SemaphoreType.DMA((2,)),
                        pltpu.SemaphoreType.DMA((2,))],
        compiler_params=_params(("arbitrary", "arbitrary")),
        name="expert_mlp",
    )(meta, gate8, h2p, wg, wu, wd)


def _combine_kernel(wtile_ref, wblk_ref, wvalid_ref, z_ref, tokrow_ref, x1_ref, gt2_ref, gf_ref,
                    y_ref, acc_ref, z2d, *, tt):
    w = pl.program_id(0)
    nw = pl.num_programs(0)
    tile = wtile_ref[w]
    first = jnp.logical_or(w == 0, wtile_ref[jnp.maximum(w - 1, 0)] != tile)
    last = jnp.logical_or(w == nw - 1, wtile_ref[jnp.minimum(w + 1, nw - 1)] != tile)

    @pl.when(first)
    def _():
        acc_ref[...] = jnp.zeros_like(acc_ref)

    @pl.when(wvalid_ref[w] == 1)
    def _():
        blk = z2d.shape[0]
        z2d[...] = z_ref[...].reshape(z2d.shape)
        zrows = _unpack_bf16_pairs(z2d[...])
        pair = (wblk_ref[w] * blk + lax.broadcasted_iota(I32, (blk, tt), 0)).astype(F32)
        owned = (pair >= tokrow_ref[0:1, :]) & (pair < tokrow_ref[1:2, :])
        onehot = jnp.where(owned, 1.0, 0.0).T.astype(BF16)
        acc_ref[...] += _dot(onehot, zrows)

    @pl.when(last)
    def _():
        x2 = x1_ref[...] + gt2_ref[0] * acc_ref[...]
        ms = jnp.mean(x2 * x2, axis=-1, keepdims=True)
        y_ref[...] = x2 * lax.rsqrt(ms + EPS) * gf_ref[...]


def _combine(wtile, wblk, wvalid, z, tokrow, x1, gt2, gf, s, tt, blk):
    n, d = x1.shape
    nw = wtile.shape[0]
    grid_spec = pltpu.PrefetchScalarGridSpec(
        num_scalar_prefetch=3,
        grid=(nw,),
        in_specs=[pl.BlockSpec((blk, 1, PACKED), lambda w, wt, wb, wv: (wb[w], 0, 0)),
                  pl.BlockSpec((8, tt), lambda w, wt, wb, wv: (0, wt[w])),
                  pl.BlockSpec((tt, d), lambda w, wt, wb, wv: (wt[w], 0)),
                  pl.BlockSpec((1, 1, d), lambda w, wt, wb, wv: (wt[w] * tt // s, 0, 0)),
                  pl.BlockSpec((1, d), lambda w, wt, wb, wv: (0, 0))],
        out_specs=pl.BlockSpec((tt, d), lambda w, wt, wb, wv: (wt[w], 0)),
        scratch_shapes=[pltpu.VMEM((tt, d), F32), pltpu.VMEM((blk, PACKED), U32)],
    )
    return pl.pallas_call(
        functools.partial(_combine_kernel, tt=tt),
        grid_spec=grid_spec,
        out_shape=jax.ShapeDtypeStruct((n, d), F32),
        compiler_params=_params(("arbitrary",)),
        name="combine_final_norm",
    )(wtile, wblk, wvalid, z, tokrow, x1, gt2, gf.reshape(1, d))


def _trunk(x, mod, w, tm_proj=512, tq=512, tu=256, tq_win=1024, tm_moe=512, route_ch=256, tt=512, blk=512):
    b, s, d = x.shape
    n = b * s
    cap = CAPACITY_FACTOR * n // N_EXPERTS
    sh1, sc1, gt1, sh2, sc2, gt2 = [m.reshape(b, 1, d) for m in jnp.split(mod, N_MOD, axis=-1)]

    qa, kat, va, qb, kbt, vb, sga, sgb = _in_projection(
        x, sc1, sh1, w["norm1_g"], w["w_in"], w["q_norm_g"], w["k_norm_g"], tm_proj)
    oa = _global_attention(qa, kat, va, tq, tu)
    ob = _window_attention(qb, kbt, vb, w["rel_bias"], w["sink"], tq_win)
    x1, h2p, afft = _out_projection(oa, ob, sga, sgb, x, gt1, sc2, sh2, w["norm2_g"],
                                    w["w_branch_a"], w["w_branch_b"], w["w_out"], w["w_router"], tm_proj)
    meta, gate8, tokrow, wtile, wblk, wvalid = _routing(afft, cap, tm_moe, tt, blk, ch=route_ch)
    z = _expert_mlp(meta, gate8, h2p, w["w_e_gate"], w["w_e_up"], w["w_e_down"], tm_moe)
    y = _combine(wtile, wblk, wvalid, z, tokrow, x1.reshape(n, d), gt2, w["norm_f_g"], s, tt, blk)
    return y.reshape(b, s, d)


def kernel(x_prompt, x_sample, c_prompt, c_sample, w_ada, b_ada, norm1_g, w_in, q_norm_g, k_norm_g, sink,
           w_branch_a, w_branch_b, w_out, norm2_g, w_router, w_e_gate, w_e_up, w_e_down, rel_bias, norm_f_g):
    assert w_ada.shape[0] == 1, "single layer"
    bp, bs = c_prompt.shape[0], c_sample.shape[0]
    rows = -(-(bp + bs) // 16) * 16
    c = jnp.concatenate([c_prompt, c_sample, jnp.zeros((rows - bp - bs, D_MODEL), F32)], axis=0)
    mod = _modulation(c, w_ada[0], b_ada[0])

    router = jnp.zeros((D_MODEL, LANES), BF16).at[:, :N_EXPERTS].set(w_router[0].astype(BF16))
    w = dict(norm1_g=norm1_g[0], w_in=w_in[0].astype(BF16), q_norm_g=q_norm_g[0], k_norm_g=k_norm_g[0],
             sink=sink[0], w_branch_a=w_branch_a[0].astype(BF16), w_branch_b=w_branch_b[0].astype(BF16),
             w_out=w_out[0].astype(BF16), norm2_g=norm2_g[0], w_router=router,
             w_e_gate=w_e_gate[0].astype(BF16), w_e_up=w_e_up[0].astype(BF16),
             w_e_down=w_e_down[0].astype(BF16), rel_bias=rel_bias, norm_f_g=norm_f_g)
    y_prompt = _trunk(x_prompt, mod[:bp], w)
    y_sample = _trunk(x_sample, mod[bp:bp + bs], w)
    return (y_prompt, y_sample)
```

```python
import functools
import math

import jax
import jax.numpy as jnp
import numpy as np
from jax import lax
from jax.experimental import pallas as pl
from jax.experimental.pallas import tpu as pltpu

F32 = jnp.float32
BF16 = jnp.bfloat16
I32 = jnp.int32
U32 = jnp.uint32

D_MODEL = 1024
HEAD_DIM = 64
A_HEADS = 8
A_KV_HEADS = 2
B_HEADS = 8
B_KV_HEADS = 2
A_WIDTH = A_HEADS * HEAD_DIM
B_WIDTH = B_HEADS * HEAD_DIM
KV_WIDTH = A_KV_HEADS * HEAD_DIM
Q_BLOCK = 128
WINDOW = 128
GRID_W = 64
ROPE_THETA = 10000.0
AXIS_DIM = HEAD_DIM // 2
N_BUCKETS = 32
MAX_DISTANCE = 128
N_EXPERTS = 16
CAPACITY_FACTOR = 2
D_FF = 2048
N_MOD = 6
EPS = 1e-6
NEG_INF = -1e30
IN_COLS = A_WIDTH + 2 * KV_WIDTH + B_WIDTH + 2 * KV_WIDTH + 2 * D_MODEL

LANES = 128
SUBLANES = 8
ROT_HALF = AXIS_DIM // 2
DIGIT_BITS = 8
DIGIT = 1 << DIGIT_BITS
PACKED = D_MODEL // 2
VMEM_LIMIT = 56 * 1024 * 1024
LOGITS_VMEM_BUDGET = 16 * 1024 * 1024

_QA, _KA, _VA = 0, A_WIDTH, A_WIDTH + KV_WIDTH
_QB = A_WIDTH + 2 * KV_WIDTH
_KB, _VB = _QB + B_WIDTH, _QB + B_WIDTH + KV_WIDTH
_GA = _QB + B_WIDTH + 2 * KV_WIDTH
_GB = _GA + D_MODEL


def _params(sem, vmem=VMEM_LIMIT):
    return pltpu.CompilerParams(dimension_semantics=sem, vmem_limit_bytes=vmem)


def _dot(a, b):
    return jnp.dot(a, b, preferred_element_type=F32)


def _mod_kernel(c_ref, w_ref, b_ref, o_ref):
    c = c_ref[...]
    s = c * jax.nn.sigmoid(c)
    o_ref[...] = _dot(s.astype(BF16), w_ref[...].astype(BF16)) + b_ref[...]


def _modulation(c, w_ada, b_ada):
    bp, d = c.shape
    n = w_ada.shape[1]
    tn = 1536
    return pl.pallas_call(
        _mod_kernel,
        grid=(n // tn,),
        in_specs=[pl.BlockSpec((bp, d), lambda j: (0, 0)),
                  pl.BlockSpec((d, tn), lambda j: (0, j)),
                  pl.BlockSpec((1, tn), lambda j: (0, j))],
        out_specs=pl.BlockSpec((bp, tn), lambda j: (0, j)),
        out_shape=jax.ShapeDtypeStruct((bp, n), F32),
        compiler_params=_params(("arbitrary",)),
        name="modulation",
    )(c, w_ada, b_ada.reshape(1, n))


def _swap16(x):
    n = x.shape[-1]
    left = pltpu.roll(x, n - ROT_HALF, axis=1)
    right = pltpu.roll(x, ROT_HALF, axis=1)
    lane = lax.broadcasted_iota(I32, x.shape, 1)
    return jnp.where((lane % AXIS_DIM) < ROT_HALF, left, right)


def _inproj_kernel(x_ref, sc_ref, sh_ref, g1_ref, w_ref, aq_ref, bq_ref, ak_ref, bk_ref, hm_ref,
                   qa_ref, kat_ref, va_ref, qb_ref, kbt_ref, vb_ref, sga_ref, sgb_ref):
    tm = x_ref.shape[1]
    halves = [slice(0, tm // 2), slice(tm // 2, tm)]
    hs = []
    for r in halves:
        x = x_ref[0, r]
        ms = jnp.mean(x * x, axis=-1, keepdims=True)
        xn = x * lax.rsqrt(ms + EPS) * g1_ref[...]
        hs.append((xn * (1.0 + sc_ref[0]) + sh_ref[0]).astype(BF16))

    def proj(h, lo, width):
        return _dot(h, w_ref[:, lo:lo + width])

    def head_rms(q):
        width = q.shape[1]
        if width == LANES:
            msq = _dot((q * q).astype(BF16), hm_ref[:LANES, :LANES])
        else:
            msq = jnp.concatenate([_dot((q[:, j:j + 2 * LANES] * q[:, j:j + 2 * LANES]).astype(BF16), hm_ref[...])
                                   for j in range(0, width, 2 * LANES)], axis=1)
        return q * lax.rsqrt(msq + EPS)

    def rope(qh, a, b):
        reps = qh.shape[1] // LANES
        if reps > 1:
            a = jnp.concatenate([a] * reps, axis=1)
            b = jnp.concatenate([b] * reps, axis=1)
        return qh * a + _swap16(qh) * b

    for r, h in zip(halves, hs):
        sga_ref[0, r] = jax.nn.sigmoid(proj(h, _GA, D_MODEL)).astype(BF16)
    for r, h in zip(halves, hs):
        qa_ref[0, r] = rope(head_rms(proj(h, _QA, A_WIDTH)), aq_ref[r], bq_ref[r]).astype(BF16)
    for r, h in zip(halves, hs):
        kv = proj(h, _KA, 2 * KV_WIDTH)
        kat_ref[0, :, r] = rope(head_rms(kv[:, :KV_WIDTH]), ak_ref[r], bk_ref[r]).T.astype(BF16)
        va_ref[0, r] = kv[:, KV_WIDTH:].astype(BF16)
    for r, h in zip(halves, hs):
        qb_ref[0, r] = (proj(h, _QB, B_WIDTH) * (HEAD_DIM ** -0.5)).astype(BF16)
    for r, h in zip(halves, hs):
        kv = proj(h, _KB, 2 * KV_WIDTH)
        kbt_ref[0, :, r] = kv[:, :KV_WIDTH].T.astype(BF16)
        vb_ref[0, r] = kv[:, KV_WIDTH:].astype(BF16)
    for r, h in zip(halves, hs):
        sgb_ref[0, r] = jax.nn.sigmoid(proj(h, _GB, D_MODEL)).astype(BF16)


def _rope_tables(s, gain, scale):
    pos = jnp.arange(s, dtype=I32)
    row = (pos // GRID_W).astype(F32)
    col = (pos % GRID_W).astype(F32)
    inv_freq = ROPE_THETA ** (-jnp.arange(0, AXIS_DIM, 2, dtype=F32) / AXIS_DIM)
    ang_r = row[:, None] * inv_freq
    ang_c = col[:, None] * inv_freq
    cos64 = jnp.concatenate([jnp.cos(ang_r)] * 2 + [jnp.cos(ang_c)] * 2, axis=1)
    sin64 = jnp.concatenate([-jnp.sin(ang_r), jnp.sin(ang_r), -jnp.sin(ang_c), jnp.sin(ang_c)], axis=1)
    g = gain.astype(F32)
    h, a = ROT_HALF, AXIS_DIM
    gp = jnp.concatenate([g[h:a], g[0:h], g[a + h:2 * a], g[a:a + h]])
    a = cos64 * g[None, :] * scale
    b = sin64 * gp[None, :] * scale
    return jnp.concatenate([a, a], axis=1), jnp.concatenate([b, b], axis=1)


def _in_projection(x, sc1, sh1, g1, w_in, q_gain, k_gain, tm):
    b, s, d = x.shape
    aq, bq = _rope_tables(s, q_gain, HEAD_DIM ** -0.5 * math.log2(math.e))
    ak, bk = _rope_tables(s, k_gain, 1.0)
    head_mean = jnp.asarray(np.kron(np.eye(2 * LANES // HEAD_DIM), np.full((HEAD_DIM, HEAD_DIM), 1.0 / HEAD_DIM)),
                            BF16)
    tok = lambda w: pl.BlockSpec((1, tm, w), lambda i, t: (i, t, 0))
    tr = pl.BlockSpec((1, KV_WIDTH, tm), lambda i, t: (i, 0, t))
    vec = pl.BlockSpec((1, 1, d), lambda i, t: (i, 0, 0))
    tab = pl.BlockSpec((tm, LANES), lambda i, t: (t, 0))
    sd = jax.ShapeDtypeStruct
    return pl.pallas_call(
        _inproj_kernel,
        grid=(b, s // tm),
        in_specs=[tok(d), vec, vec,
                  pl.BlockSpec((1, d), lambda i, t: (0, 0)),
                  pl.BlockSpec((d, IN_COLS), lambda i, t: (0, 0)),
                  tab, tab, tab, tab,
                  pl.BlockSpec((2 * LANES, 2 * LANES), lambda i, t: (0, 0))],
        out_specs=[tok(A_WIDTH), tr, tok(KV_WIDTH), tok(B_WIDTH), tr, tok(KV_WIDTH), tok(d), tok(d)],
        out_shape=[sd((b, s, A_WIDTH), BF16), sd((b, KV_WIDTH, s), BF16), sd((b, s, KV_WIDTH), BF16),
                   sd((b, s, B_WIDTH), BF16), sd((b, KV_WIDTH, s), BF16), sd((b, s, KV_WIDTH), BF16),
                   sd((b, s, d), BF16), sd((b, s, d), BF16)],
        compiler_params=_params(("arbitrary", "arbitrary")),
        name="in_projection",
    )(x, sc1, sh1, g1.reshape(1, d), w_in, aq, bq, ak, bk, head_mean)


def _half_lane_variants(v, k_is_one):
    lane = lax.broadcasted_iota(I32, v.shape, 1)
    vr = pltpu.roll(v, HEAD_DIM, axis=1)
    own_lo = jnp.where(k_is_one, vr, v)
    own_hi = jnp.where(k_is_one, v, vr)
    lo = jnp.where(lane < HEAD_DIM, own_lo, 0.0)
    hi = jnp.where(lane >= HEAD_DIM, own_hi, 0.0)
    return lo, hi


_SUM_LANE_LO = LANES - 1
_SUM_LANE_HI = 0


def _gattn_kernel(q_ref, kt_ref, v_ref, o_ref, vv_ref, s_ref, qs_ref, os_ref, *, tu):
    k = pl.program_id(1)
    qi = pl.program_id(2)
    tq = q_ref.shape[1]
    group = A_HEADS // A_KV_HEADS
    nu = group * (tq // tu)

    @pl.when(qi == 0)
    def _():
        lo, hi = _half_lane_variants(v_ref[0].astype(F32), k == 1)
        lane = lax.broadcasted_iota(I32, lo.shape, 1)
        vv_ref[0] = jnp.where(lane == _SUM_LANE_LO, 1.0, lo).astype(BF16)
        vv_ref[1] = jnp.where(lane == _SUM_LANE_HI, 1.0, hi).astype(BF16)

    for u in range(nu):
        r, g = divmod(u, group)
        qs_ref[u] = q_ref[0, r * tu:(r + 1) * tu, g * HEAD_DIM:(g + 1) * HEAD_DIM]

    upb = s_ref.shape[0]

    def units(i, carry):
        maxes = []
        for j in range(upb):
            sc = _dot(qs_ref[upb * i + j], kt_ref[0])
            s_ref[j] = sc
            maxes.append(jnp.max(sc, axis=-1, keepdims=True))
        for j in range(upb):
            p = jnp.exp2(s_ref[j] - maxes[j])
            os_ref[upb * i + j] = _dot(p.astype(BF16), vv_ref[j % 2])
        return carry

    lax.fori_loop(0, nu // upb, units, 0)

    lane_o = lax.broadcasted_iota(I32, (tu, LANES), 1)
    for r in range(tq // tu):
        for j in range(group // 2):
            even = os_ref[r * group + 2 * j]
            odd = os_ref[r * group + 2 * j + 1]
            o = (jnp.where(lane_o < HEAD_DIM, even, 0.0) / even[:, _SUM_LANE_LO:_SUM_LANE_LO + 1]
                 + jnp.where(lane_o >= HEAD_DIM, odd, 0.0) / odd[:, _SUM_LANE_HI:_SUM_LANE_HI + 1])
            o_ref[0, r * tu:(r + 1) * tu, j * LANES:(j + 1) * LANES] = o.astype(o_ref.dtype)


def _global_attention(qa, kat, va, tq, tu):
    b, s, _ = qa.shape
    gw = A_WIDTH // A_KV_HEADS
    tq = min(tq, s)
    tu = min(tu, tq)
    nu = gw // HEAD_DIM * (tq // tu)
    upb = max(2, min(nu, LOGITS_VMEM_BUDGET // (tu * s * 4)))
    while nu % upb or upb % 2:
        upb -= 1
    return pl.pallas_call(
        functools.partial(_gattn_kernel, tu=tu),
        grid=(b, A_KV_HEADS, s // tq),
        in_specs=[pl.BlockSpec((1, tq, gw), lambda i, k, t: (i, t, k)),
                  pl.BlockSpec((1, HEAD_DIM, s), lambda i, k, t: (i, k, 0)),
                  pl.BlockSpec((1, s, KV_WIDTH), lambda i, k, t: (i, 0, 0))],
        out_specs=pl.BlockSpec((1, tq, gw), lambda i, k, t: (i, t, k)),
        out_shape=jax.ShapeDtypeStruct((b, s, A_WIDTH), BF16),
        scratch_shapes=[pltpu.VMEM((2, s, KV_WIDTH), BF16), pltpu.VMEM((upb, tu, s), F32),
                        pltpu.VMEM((nu, tu, HEAD_DIM), BF16), pltpu.VMEM((nu, tu, LANES), F32)],
        compiler_params=_params(("arbitrary", "arbitrary", "arbitrary")),
        name="global_attention",
    )(qa, kat, va)


def _t5_bucket_np(rel):
    half = N_BUCKETS // 2
    max_exact = half // 2
    base = (rel > 0).astype(np.int32) * half
    n = np.abs(rel)
    large = max_exact + (np.log(np.maximum(n, 1).astype(np.float32) / max_exact)
                         / math.log(MAX_DISTANCE / max_exact) * (half - max_exact)).astype(np.int32)
    large = np.minimum(large, half - 1)
    return base + np.where(n < max_exact, n, large)


def _stack_order(k):
    group = B_HEADS // B_KV_HEADS
    return [k * group + g for g in (0, 2, 1, 3)]


def _wattn_kernel(q_ref, ktp_ref, ktc_ref, ktn_ref, vp_ref, vc_ref, vn_ref, bias_ref, sink_ref, o_ref):
    t = pl.program_id(1)
    nt = pl.num_programs(1)
    span = Q_BLOCK + 2 * WINDOW
    nqb = q_ref.shape[1] // Q_BLOCK
    kt = jnp.concatenate([ktp_ref[0], ktc_ref[0], ktn_ref[0]], axis=1)
    v = jnp.concatenate([vp_ref[0], vc_ref[0], vn_ref[0]], axis=0).astype(F32)
    col = lax.broadcasted_iota(I32, (1, span), 1)
    first_ok = (col >= WINDOW) | (t > 0)
    last_ok = (col < WINDOW + Q_BLOCK) | (t < nt - 1)
    half_rows = 2 * Q_BLOCK
    values = []
    for k in range(B_KV_HEADS):
        lo, hi = _half_lane_variants(v, k == 1)
        values.append((lo.astype(BF16), hi.astype(BF16)))
    sinks = [sink_ref[k][:, 0:1] for k in range(B_KV_HEADS)]
    items = [(k, jb) for k in range(B_KV_HEADS) for jb in range(nqb)]
    keys = lambda jb: slice(jb * Q_BLOCK, jb * Q_BLOCK + span)
    rows = lambda jb: slice(jb * Q_BLOCK, (jb + 1) * Q_BLOCK)

    logits = []
    for k, jb in items:
        q4 = jnp.concatenate([q_ref[0, rows(jb), h * HEAD_DIM:(h + 1) * HEAD_DIM] for h in _stack_order(k)],
                             axis=0)
        lg = _dot(q4, kt[k * HEAD_DIM:(k + 1) * HEAD_DIM, keys(jb)]) + bias_ref[k]
        if jb == 0:
            lg = jnp.where(first_ok, lg, NEG_INF)
        if jb == nqb - 1:
            lg = jnp.where(last_ok, lg, NEG_INF)
        logits.append(lg)
    maxes = [jnp.maximum(jnp.max(lg, axis=-1, keepdims=True), sinks[k]) for (k, _), lg in zip(items, logits)]
    probs = [jnp.exp(lg - m) for lg, m in zip(logits, maxes)]
    dens = [jnp.sum(p, axis=-1, keepdims=True) + jnp.exp(sinks[k] - m)
            for (k, _), p, m in zip(items, probs, maxes)]
    outs = []
    for (k, jb), p, den in zip(items, probs, dens):
        pb = p.astype(BF16)
        vlo, vhi = values[k]
        o_even = _dot(pb[:half_rows], vlo[keys(jb)]) / den[:half_rows]
        o_odd = _dot(pb[half_rows:], vhi[keys(jb)]) / den[half_rows:]
        outs.append((o_even, o_odd))
    for (k, jb), (o_even, o_odd) in zip(items, outs):
        for j in range(2):
            pair = o_even[j * Q_BLOCK:(j + 1) * Q_BLOCK] + o_odd[j * Q_BLOCK:(j + 1) * Q_BLOCK]
            pj = 2 * k + j
            o_ref[0, rows(jb), pj * LANES:(pj + 1) * LANES] = pair.astype(o_ref.dtype)


def _window_attention(qb, kbt, vb, rel_bias, sink, tq):
    b, s, _ = qb.shape
    tq = min(tq, s)
    nb = s // Q_BLOCK
    per = tq // Q_BLOCK
    span = Q_BLOCK + 2 * WINDOW
    rel = np.arange(span)[None, :] - WINDOW - np.arange(Q_BLOCK)[:, None]
    band = np.abs(rel) <= WINDOW
    onehot = np.eye(N_BUCKETS, dtype=np.float32)[:, _t5_bucket_np(rel).reshape(-1)]
    bias = jnp.dot(rel_bias.astype(F32).T, jnp.asarray(onehot), precision=lax.Precision.HIGHEST)
    bias = jnp.where(jnp.asarray(band)[None], bias.reshape(B_HEADS, Q_BLOCK, span), NEG_INF)
    order = np.array([_stack_order(k) for k in range(B_KV_HEADS)])
    bias4 = bias[order].reshape(B_KV_HEADS, 4 * Q_BLOCK, span)
    sink4 = jnp.broadcast_to(sink.astype(F32)[order][:, :, None, None],
                             (B_KV_HEADS, 4, Q_BLOCK, LANES)).reshape(B_KV_HEADS, 4 * Q_BLOCK, LANES)
    prev = lambda i, t: jnp.maximum(t * per - 1, 0)
    nxt = lambda i, t: jnp.minimum((t + 1) * per, nb - 1)
    kt_edge = lambda f: pl.BlockSpec((1, KV_WIDTH, Q_BLOCK), lambda i, t: (i, 0, f(i, t)))
    v_edge = lambda f: pl.BlockSpec((1, Q_BLOCK, KV_WIDTH), lambda i, t: (i, f(i, t), 0))
    return pl.pallas_call(
        _wattn_kernel,
        grid=(b, s // tq),
        in_specs=[pl.BlockSpec((1, tq, B_WIDTH), lambda i, t: (i, t, 0)),
                  kt_edge(prev), pl.BlockSpec((1, KV_WIDTH, tq), lambda i, t: (i, 0, t)), kt_edge(nxt),
                  v_edge(prev), pl.BlockSpec((1, tq, KV_WIDTH), lambda i, t: (i, t, 0)), v_edge(nxt),
                  pl.BlockSpec((B_KV_HEADS, 4 * Q_BLOCK, span), lambda i, t: (0, 0, 0)),
                  pl.BlockSpec((B_KV_HEADS, 4 * Q_BLOCK, LANES), lambda i, t: (0, 0, 0))],
        out_specs=pl.BlockSpec((1, tq, B_WIDTH), lambda i, t: (i, t, 0)),
        out_shape=jax.ShapeDtypeStruct((b, s, B_WIDTH), BF16),
        compiler_params=_params(("arbitrary", "arbitrary")),
        name="window_attention",
    )(qb, kbt, kbt, kbt, vb, vb, vb, bias4, sink4)


def _pack_bf16_pairs(x):
    k = x.shape[1] // 2
    hi = pltpu.bitcast(x[:, :k].astype(BF16).astype(F32), U32)
    lo = pltpu.bitcast(x[:, k:].astype(BF16).astype(F32), U32)
    return hi | (lo >> 16)


def _unpack_bf16_pairs(w):
    hi = pltpu.bitcast(w & jnp.uint32(0xFFFF0000), F32)
    lo = pltpu.bitcast(w << 16, F32)
    return jnp.concatenate([hi, lo], axis=1).astype(BF16)


def _outproj_kernel(oa_ref, ob_ref, sga_ref, sgb_ref, x_ref, gt1_ref, sc2_ref, sh2_ref, g2_ref,
                    wa_ref, wb_ref, wo_ref, wr_ref, x1_ref, h2p_ref, afft_ref):
    tm = x_ref.shape[1]
    halves = [slice(0, tm // 2), slice(tm // 2, tm)]
    merged = []
    for r in halves:
        a = _dot(oa_ref[0, r], wa_ref[...])
        b = _dot(ob_ref[0, r], wb_ref[...])
        merged.append((sga_ref[0, r].astype(F32) * a + sgb_ref[0, r].astype(F32) * b).astype(BF16))
    h2s = []
    for r, mg in zip(halves, merged):
        x1 = x_ref[0, r] + gt1_ref[0] * _dot(mg, wo_ref[...])
        x1_ref[0, r] = x1
        ms = jnp.mean(x1 * x1, axis=-1, keepdims=True)
        h2s.append((x1 * lax.rsqrt(ms + EPS) * g2_ref[...]) * (1.0 + sc2_ref[0]) + sh2_ref[0])
    for r, h2 in zip(halves, h2s):
        h2p_ref[r] = _pack_bf16_pairs(h2).reshape(tm // 2, 1, PACKED)
        logits = _dot(h2.astype(BF16), wr_ref[...])
        lane = lax.broadcasted_iota(I32, logits.shape, 1)
        logits = jnp.where(lane < N_EXPERTS, logits, -jnp.inf)
        m = jnp.max(logits, axis=-1, keepdims=True)
        e = jnp.exp(logits - m)
        aff = e / jnp.sum(e, axis=-1, keepdims=True)
        afft_ref[:, r] = aff.T[:N_EXPERTS, :]


def _out_projection(oa, ob, sga, sgb, x, gt1, sc2, sh2, g2, wa, wb, wo, wr, tm):
    b, s, d = x.shape
    nt = s // tm
    n = b * s
    tok = lambda w: pl.BlockSpec((1, tm, w), lambda i, t: (i, t, 0))
    vec = pl.BlockSpec((1, 1, d), lambda i, t: (i, 0, 0))
    full = lambda r, c: pl.BlockSpec((r, c), lambda i, t: (0, 0))
    sd = jax.ShapeDtypeStruct
    return pl.pallas_call(
        _outproj_kernel,
        grid=(b, nt),
        in_specs=[tok(A_WIDTH), tok(B_WIDTH), tok(d), tok(d), tok(d), vec, vec, vec, full(1, d),
                  full(A_WIDTH, d), full(B_WIDTH, d), full(d, d), full(d, LANES)],
        out_specs=[tok(d),
                   pl.BlockSpec((tm, 1, PACKED), lambda i, t: (i * nt + t, 0, 0)),
                   pl.BlockSpec((N_EXPERTS, tm), lambda i, t: (0, i * nt + t))],
        out_shape=[sd((b, s, d), F32), sd((n, 1, PACKED), U32), sd((N_EXPERTS, n), F32)],
        compiler_params=_params(("arbitrary", "arbitrary")),
        name="out_projection",
    )(oa, ob, sga, sgb, x, gt1, sc2, sh2, g2.reshape(1, d), wa, wb, wo, wr)


def _threshold_kernel(aff_ref, thr_ref, cut_ref, *, cap):
    bits = pltpu.bitcast(aff_ref[...], I32)
    n = bits.shape[1]

    def value_step(i, lo):
        cand = lo | jnp.left_shift(jnp.int32(1), 30 - i)
        cnt = jnp.sum((bits >= cand).astype(F32), axis=1, keepdims=True)
        return jnp.where(cnt >= cap, cand, lo)

    thr = lax.fori_loop(0, 31, value_step, jnp.zeros((N_EXPERTS, 1), I32))
    need = cap - jnp.sum((bits > thr).astype(F32), axis=1, keepdims=True)
    eq = bits == thr
    tpos = lax.broadcasted_iota(I32, (1, n), 1)
    nbits = max(n.bit_length(), 1)

    def index_step(i, cut):
        cand = cut + jnp.left_shift(jnp.int32(1), nbits - 1 - i)
        below = jnp.sum(jnp.where(eq & (tpos < cand), 1.0, 0.0), axis=1, keepdims=True)
        return jnp.where((below <= need) & (cand <= n), cand, cut)

    cut = lax.fori_loop(0, nbits, index_step, jnp.zeros((N_EXPERTS, 1), I32))
    thr_ref[...] = jnp.broadcast_to(thr, thr_ref.shape)
    cut_ref[...] = jnp.broadcast_to(cut, cut_ref.shape)


def _capacity_threshold(afft, cap):
    e, n = afft.shape
    out = pl.BlockSpec((e, LANES), lambda i: (0, 0))
    return pl.pallas_call(
        functools.partial(_threshold_kernel, cap=cap),
        grid=(1,),
        in_specs=[pl.BlockSpec((e, n), lambda i: (0, 0))],
        out_specs=[out, out],
        out_shape=[jax.ShapeDtypeStruct((e, LANES), I32)] * 2,
        compiler_params=_params(("arbitrary",)),
        name="capacity_threshold",
    )(afft)


_AUX_ROWS = SUBLANES


def _prefix_kernel(aff_ref, thr_ref, cut_ref, u_ref, ls_ref, posm_ref, slot_ref, tokrow_ref, offs_ref,
                   run_ref, *, tb, ch):
    i = pl.program_id(0)

    @pl.when(i == 0)
    def _():
        run_ref[...] = jnp.zeros_like(run_ref)

    bits = pltpu.bitcast(aff_ref[...], I32)
    tpos = i * tb + lax.broadcasted_iota(I32, (1, tb), 1)
    thr = thr_ref[:, 0:1]
    sel = (bits > thr) | ((bits == thr) & (tpos < cut_ref[:, 0:1]))
    s = jnp.where(sel, 1.0, 0.0)
    for j in range(tb // ch):
        sj = s[:, j * ch:(j + 1) * ch]
        cntj = jnp.sum(sj, axis=0, keepdims=True)
        x = jnp.concatenate([sj, cntj, jnp.zeros((_AUX_ROWS - 1, ch), F32)], axis=0).astype(BF16)
        run = run_ref[:, 0:1]
        incl = _dot(x, u_ref[...]) + run
        posm_ref[:, j * ch:(j + 1) * ch] = jnp.where(sj > 0.0, incl[:N_EXPERTS] - sj, -1.0)
        tok_end = incl[N_EXPERTS:N_EXPERTS + 1]
        tok_off = tok_end - cntj
        slot_ref[:, j * ch:(j + 1) * ch] = tok_off + _dot(ls_ref[...], sj.astype(BF16))
        tokrow_ref[:, j * ch:(j + 1) * ch] = jnp.concatenate(
            [tok_off, tok_end, jnp.zeros((6, ch), F32)], axis=0)
        offs_ref[j] = jnp.broadcast_to(run[:N_EXPERTS], (N_EXPERTS, LANES))
        run_ref[...] = jnp.broadcast_to(incl[:, ch - 1:ch], run_ref.shape)


def _routing_prefix(afft, thr, cut, tb, ch):
    e, n = afft.shape
    upper = jnp.asarray(np.triu(np.ones((ch, ch), np.float32)), BF16)
    lower_strict = jnp.asarray(np.tril(np.ones((e, e), np.float32), -1), BF16)
    rows = lambda r: pl.BlockSpec((r, tb), lambda i: (0, i))
    const = lambda r, c: pl.BlockSpec((r, c), lambda i: (0, 0))
    sd = jax.ShapeDtypeStruct
    return pl.pallas_call(
        functools.partial(_prefix_kernel, tb=tb, ch=ch),
        grid=(n // tb,),
        in_specs=[rows(e), const(e, LANES), const(e, LANES), const(ch, ch), const(e, e)],
        out_specs=[rows(e), rows(e), rows(8), pl.BlockSpec((tb // ch, e, LANES), lambda i: (i, 0, 0))],
        out_shape=[sd((e, n), F32), sd((e, n), F32), sd((8, n), F32), sd((n // ch, e, LANES), F32)],
        scratch_shapes=[pltpu.VMEM((e + _AUX_ROWS, LANES), F32)],
        compiler_params=_params(("arbitrary",)),
        name="routing_prefix",
    )(afft, thr, cut, upper, lower_strict)


def _compact_kernel(win_ref, aff_ref, posm_ref, slot_ref, list_ref, *, tb, ch, nc):
    i = pl.program_id(0)

    @pl.when(i == 0)
    def _():
        list_ref[...] = jnp.zeros_like(list_ref)

    rank = lax.broadcasted_iota(I32, (2 * ch, ch), 0).astype(F32)
    tok_local = lax.broadcasted_iota(I32, (1, ch), 1).astype(F32)

    def expert_body(e, carry):
        for j in range(tb // ch):
            c = i * (tb // ch) + j
            w0 = win_ref[e * nc + c]
            lanes = slice(j * ch, (j + 1) * ch)
            rel = posm_ref[pl.ds(e, 1), lanes] - (w0 * ch).astype(F32)
            onehot_t = jnp.where(rank == rel, 1.0, 0.0).astype(BF16)
            slot = slot_ref[pl.ds(e, 1), lanes].astype(I32)
            g = aff_ref[pl.ds(e, 1), lanes]
            g_hi = g.astype(BF16).astype(F32)
            g_mid = (g - g_hi).astype(BF16).astype(F32)
            g_lo = g - g_hi - g_mid
            vals = jnp.concatenate(
                [tok_local, jnp.full((1, ch), c, I32).astype(F32),
                 (slot & (DIGIT - 1)).astype(F32), ((slot >> DIGIT_BITS) & (DIGIT - 1)).astype(F32),
                 (slot >> (2 * DIGIT_BITS)).astype(F32),
                 g_hi, g_mid, g_lo], axis=0).astype(BF16)
            out = lax.dot_general(vals, onehot_t, (((1,), (1,)), ((), ())), preferred_element_type=F32)
            list_ref[e, w0] += out[:, :ch]
            list_ref[e, w0 + 1] += out[:, ch:]
        return carry

    lax.fori_loop(0, N_EXPERTS, expert_body, 0)


def _routing_compact(win, afft, posm, slot, cap, tb, ch):
    e, n = afft.shape
    nc = n // ch
    nwin = cap // ch
    rows = pl.BlockSpec((e, tb), lambda i, w: (0, i))
    grid_spec = pltpu.PrefetchScalarGridSpec(
        num_scalar_prefetch=1,
        grid=(n // tb,),
        in_specs=[rows, rows, rows],
        out_specs=pl.BlockSpec((e, nwin + 2, 8, ch), lambda i, w: (0, 0, 0, 0)),
    )
    return pl.pallas_call(
        functools.partial(_compact_kernel, tb=tb, ch=ch, nc=nc),
        grid_spec=grid_spec,
        out_shape=jax.ShapeDtypeStruct((e, nwin + 2, 8, ch), F32),
        compiler_params=_params(("arbitrary",)),
        name="routing_compact",
    )(win, afft, posm, slot)


def _routing(afft, cap, tm, tt, blk, tb=2048, ch=256):
    e, n = afft.shape
    assert n // ch <= DIGIT and ch <= DIGIT and cap % ch == 0 and cap % tm == 0 and e * cap < (1 << 24)
    tb = min(tb, n)
    thr, cut = _capacity_threshold(afft, cap)
    posm, slot, tokrow, offs = _routing_prefix(afft, thr, cut, tb, ch)
    win = (offs[:, :, 0].astype(I32) // ch).T.reshape(-1)
    lists = _routing_compact(win, afft, posm, slot, cap, tb, ch)[:, :cap // ch]
    nsteps = e * cap // tm
    idx = (lists[:, :, 1] * ch + lists[:, :, 0]).astype(I32).reshape(nsteps, tm)
    dst = (lists[:, :, 2] + lists[:, :, 3] * float(DIGIT) + lists[:, :, 4] * float(DIGIT * DIGIT)).astype(I32)
    dst = dst.reshape(nsteps, tm)
    gate = ((lists[:, :, 5] + lists[:, :, 6]) + lists[:, :, 7]).reshape(nsteps, 1, tm)
    meta = jnp.concatenate([idx, dst], axis=1)
    gate8 = jnp.broadcast_to(gate, (nsteps, SUBLANES, tm))

    npairs = e * cap
    ntile, nblk = n // tt, npairs // blk
    start = tokrow[0, ::tt].astype(I32)
    end = jnp.concatenate([start[1:], jnp.full((1,), npairs, I32)])
    b0 = jnp.minimum(start // blk, nblk - 1)
    b1 = jnp.where(end > start, (end - 1) // blk, b0)
    nb = b1 - b0 + 1
    wend = jnp.cumsum(nb)
    woff = wend - nb
    w = jnp.arange(nblk + ntile, dtype=I32)
    wt = jnp.minimum(jnp.sum((w[:, None] >= wend[None, :]).astype(I32), axis=1), ntile - 1)
    valid = (w < wend[-1]).astype(I32)
    wblk = jnp.where(valid == 1, b0[wt] + (w - woff[wt]), b1[ntile - 1]).astype(I32)
    return meta, gate8, tokrow, wt, wblk, valid


def _moe_kernel(meta_hbm, gate_ref, h2_hbm, wg_ref, wu_ref, wd_ref, z_hbm,
                meta_smem, xbuf, x2d, zbuf, sem_m, sem_g, sem_s, *, tm, nsteps):
    nt = pl.num_programs(1)
    s = pl.program_id(0) * nt + pl.program_id(1)
    slot = s % 2

    def meta_copy(step, mslot):
        return pltpu.make_async_copy(meta_hbm.at[step], meta_smem.at[mslot], sem_m)

    def issue_gather(mslot, bslot):
        for r in range(tm):
            tok = meta_smem[mslot, r]
            pltpu.make_async_copy(h2_hbm.at[tok], xbuf.at[bslot, r], sem_g.at[bslot]).start(priority=r % 2)

    def wait_gather(bslot):
        pltpu.make_async_copy(h2_hbm.at[pl.ds(0, tm)], xbuf.at[bslot], sem_g.at[bslot]).wait()

    def issue_scatter(mslot, bslot):
        for r in range(tm):
            dst = meta_smem[mslot, tm + r]
            pltpu.make_async_copy(zbuf.at[bslot, r], z_hbm.at[dst], sem_s.at[bslot]).start(priority=r % 2)

    def wait_scatter(bslot):
        pltpu.make_async_copy(zbuf.at[bslot], z_hbm.at[pl.ds(0, tm)], sem_s.at[bslot]).wait()

    last = nsteps - 1

    @pl.when(s == 0)
    def _():
        for step, mslot in ((0, 0), (min(1, last), 1)):
            c = meta_copy(step, mslot)
            c.start()
            c.wait()
        for r in range(tm):
            meta_smem[3, tm + r] = nsteps * tm + r
        zbuf[1] = jnp.zeros(zbuf.shape[1:], zbuf.dtype)
        issue_gather(0, 0)

    wait_gather(slot)

    @pl.when(s >= 1)
    def _():
        wait_scatter(slot)

    x2d[...] = xbuf[slot].reshape(tm, PACKED)
    prefetch = meta_copy(jnp.minimum(s + 2, last), (s + 2) % 4)
    prefetch.start()
    issue_gather((s + 1) % 4, 1 - slot)
    issue_scatter((s + 3) % 4, 1 - slot)
    xe = _unpack_bf16_pairs(x2d[...])
    gate = _dot(xe, wg_ref[...])
    up = _dot(xe, wu_ref[...])
    hid = (gate * jax.nn.sigmoid(gate) * up).astype(BF16)
    gcol = gate_ref[...].T[:, 0:1]
    halves = [slice(0, tm // 2), slice(tm // 2, tm)]
    ye = [_dot(hid[h], wd_ref[...]) for h in halves]
    for h, y in zip(halves, ye):
        zbuf[slot, h] = _pack_bf16_pairs(y * gcol[h]).reshape(tm // 2, 1, PACKED)
    prefetch.wait()

    @pl.when(s == last)
    def _():
        issue_scatter(s % 4, slot)
        wait_scatter(slot)
        wait_scatter(1 - slot)
        wait_gather(1 - slot)


def _expert_mlp(meta, gate8, h2p, wg, wu, wd, tm):
    nsteps = meta.shape[0]
    nt = nsteps // N_EXPERTS
    d, f = wg.shape[1], wg.shape[2]
    step = lambda e, i: (e * nt + i, 0, 0)
    return pl.pallas_call(
        functools.partial(_moe_kernel, tm=tm, nsteps=nsteps),
        grid=(N_EXPERTS, nt),
        in_specs=[pl.BlockSpec(memory_space=pl.ANY),
                  pl.BlockSpec((None, SUBLANES, tm), step),
                  pl.BlockSpec(memory_space=pl.ANY),
                  pl.BlockSpec((None, d, f), lambda e, i: (e, 0, 0)),
                  pl.BlockSpec((None, d, f), lambda e, i: (e, 0, 0)),
                  pl.BlockSpec((None, f, d), lambda e, i: (e, 0, 0))],
        out_specs=pl.BlockSpec(memory_space=pl.ANY),
        out_shape=jax.ShapeDtypeStruct(((nsteps + 1) * tm, 1, PACKED), U32),
        scratch_shapes=[pltpu.SMEM((4, 2 * tm), I32),
                        pltpu.VMEM((2, tm, 1, PACKED), U32),
                        pltpu.VMEM((tm, PACKED), U32),
                        pltpu.VMEM((2, tm, 1, PACKED), U32),
                        pltpu.SemaphoreType.DMA,
                        pltpu.SemaphoreType.DMA((2,)),
                        pltpu.SemaphoreType.DMA((2,))],
        compiler_params=_params(("arbitrary", "arbitrary")),
        name="expert_mlp",
    )(meta, gate8, h2p, wg, wu, wd)


def _combine_kernel(wtile_ref, wblk_ref, wvalid_ref, z_ref, tokrow_ref, x1_ref, gt2_ref, gf_ref,
                    y_ref, acc_ref, z2d, *, tt):
    w = pl.program_id(0)
    nw = pl.num_programs(0)
    tile = wtile_ref[w]
    first = jnp.logical_or(w == 0, wtile_ref[jnp.maximum(w - 1, 0)] != tile)
    last = jnp.logical_or(w == nw - 1, wtile_ref[jnp.minimum(w + 1, nw - 1)] != tile)

    @pl.when(first)
    def _():
        acc_ref[...] = jnp.zeros_like(acc_ref)

    @pl.when(wvalid_ref[w] == 1)
    def _():
        blk = z2d.shape[0]
        z2d[...] = z_ref[...].reshape(z2d.shape)
        zrows = _unpack_bf16_pairs(z2d[...])
        pair = (wblk_ref[w] * blk + lax.broadcasted_iota(I32, (blk, tt), 0)).astype(F32)
        owned = (pair >= tokrow_ref[0:1, :]) & (pair < tokrow_ref[1:2, :])
        onehot = jnp.where(owned, 1.0, 0.0).T.astype(BF16)
        acc_ref[...] += _dot(onehot, zrows)

    @pl.when(last)
    def _():
        x2 = x1_ref[...] + gt2_ref[0] * acc_ref[...]
        ms = jnp.mean(x2 * x2, axis=-1, keepdims=True)
        y_ref[...] = x2 * lax.rsqrt(ms + EPS) * gf_ref[...]


def _combine(wtile, wblk, wvalid, z, tokrow, x1, gt2, gf, s, tt, blk):
    n, d = x1.shape
    nw = wtile.shape[0]
    grid_spec = pltpu.PrefetchScalarGridSpec(
        num_scalar_prefetch=3,
        grid=(nw,),
        in_specs=[pl.BlockSpec((blk, 1, PACKED), lambda w, wt, wb, wv: (wb[w], 0, 0)),
                  pl.BlockSpec((8, tt), lambda w, wt, wb, wv: (0, wt[w])),
                  pl.BlockSpec((tt, d), lambda w, wt, wb, wv: (wt[w], 0)),
                  pl.BlockSpec((1, 1, d), lambda w, wt, wb, wv: (wt[w] * tt // s, 0, 0)),
                  pl.BlockSpec((1, d), lambda w, wt, wb, wv: (0, 0))],
        out_specs=pl.BlockSpec((tt, d), lambda w, wt, wb, wv: (wt[w], 0)),
        scratch_shapes=[pltpu.VMEM((tt, d), F32), pltpu.VMEM((blk, PACKED), U32)],
    )
    return pl.pallas_call(
        functools.partial(_combine_kernel, tt=tt),
        grid_spec=grid_spec,
        out_shape=jax.ShapeDtypeStruct((n, d), F32),
        compiler_params=_params(("arbitrary",)),
        name="combine_final_norm",
    )(wtile, wblk, wvalid, z, tokrow, x1, gt2, gf.reshape(1, d))


def _trunk(x, mod, w, tm_proj=512, tq=512, tu=256, tq_win=1024, tm_moe=256, route_ch=256, tt=512, blk=512):
    b, s, d = x.shape
    n = b * s
    cap = CAPACITY_FACTOR * n // N_EXPERTS
    sh1, sc1, gt1, sh2, sc2, gt2 = [m.reshape(b, 1, d) for m in jnp.split(mod, N_MOD, axis=-1)]

    qa, kat, va, qb, kbt, vb, sga, sgb = _in_projection(
        x, sc1, sh1, w["norm1_g"], w["w_in"], w["q_norm_g"], w["k_norm_g"], tm_proj)
    oa = _global_attention(qa, kat, va, tq, tu)
    ob = _window_attention(qb, kbt, vb, w["rel_bias"], w["sink"], tq_win)
    x1, h2p, afft = _out_projection(oa, ob, sga, sgb, x, gt1, sc2, sh2, w["norm2_g"],
                                    w["w_branch_a"], w["w_branch_b"], w["w_out"], w["w_router"], tm_proj)
    meta, gate8, tokrow, wtile, wblk, wvalid = _routing(afft, cap, tm_moe, tt, blk, ch=route_ch)
    z = _expert_mlp(meta, gate8, h2p, w["w_e_gate"], w["w_e_up"], w["w_e_down"], tm_moe)
    y = _combine(wtile, wblk, wvalid, z, tokrow, x1.reshape(n, d), gt2, w["norm_f_g"], s, tt, blk)
    return y.reshape(b, s, d)


def kernel(x_prompt, x_sample, c_prompt, c_sample, w_ada, b_ada, norm1_g, w_in, q_norm_g, k_norm_g, sink,
           w_branch_a, w_branch_b, w_out, norm2_g, w_router, w_e_gate, w_e_up, w_e_down, rel_bias, norm_f_g):
    assert w_ada.shape[0] == 1, "single layer"
    bp, bs = c_prompt.shape[0], c_sample.shape[0]
    rows = -(-(bp + bs) // 16) * 16
    c = jnp.concatenate([c_prompt, c_sample, jnp.zeros((rows - bp - bs, D_MODEL), F32)], axis=0)
    mod = _modulation(c, w_ada[0], b_ada[0])

    router = jnp.zeros((D_MODEL, LANES), BF16).at[:, :N_EXPERTS].set(w_router[0].astype(BF16))
    w = dict(norm1_g=norm1_g[0], w_in=w_in[0].astype(BF16), q_norm_g=q_norm_g[0], k_norm_g=k_norm_g[0],
             sink=sink[0], w_branch_a=w_branch_a[0].astype(BF16), w_branch_b=w_branch_b[0].astype(BF16),
             w_out=w_out[0].astype(BF16), norm2_g=norm2_g[0], w_router=router,
             w_e_gate=w_e_gate[0].astype(BF16), w_e_up=w_e_up[0].astype(BF16),
             w_e_down=w_e_down[0].astype(BF16), rel_bias=rel_bias, norm_f_g=norm_f_g)
    y_prompt = _trunk(x_prompt, mod[:bp], w)
    y_sample = _trunk(x_sample, mod[bp:bp + bs], w)
    return (y_prompt, y_sample)
```

```python
import functools
import math

import jax
import jax.numpy as jnp
import numpy as np
from jax import lax
from jax.experimental import pallas as pl
from jax.experimental.pallas import tpu as pltpu

F32 = jnp.float32
BF16 = jnp.bfloat16
I32 = jnp.int32
U32 = jnp.uint32

D_MODEL = 1024
HEAD_DIM = 64
A_HEADS = 8
A_KV_HEADS = 2
B_HEADS = 8
B_KV_HEADS = 2
A_WIDTH = A_HEADS * HEAD_DIM
B_WIDTH = B_HEADS * HEAD_DIM
KV_WIDTH = A_KV_HEADS * HEAD_DIM
Q_BLOCK = 128
WINDOW = 128
GRID_W = 64
ROPE_THETA = 10000.0
AXIS_DIM = HEAD_DIM // 2
N_BUCKETS = 32
MAX_DISTANCE = 128
N_EXPERTS = 16
CAPACITY_FACTOR = 2
D_FF = 2048
N_MOD = 6
EPS = 1e-6
NEG_INF = -1e30
IN_COLS = A_WIDTH + 2 * KV_WIDTH + B_WIDTH + 2 * KV_WIDTH + 2 * D_MODEL

LANES = 128
SUBLANES = 8
ROT_HALF = AXIS_DIM // 2
DIGIT_BITS = 8
DIGIT = 1 << DIGIT_BITS
PACKED = D_MODEL // 2
VMEM_LIMIT = 56 * 1024 * 1024
LOGITS_VMEM_BUDGET = 16 * 1024 * 1024
MLP_ROWS = 512

_QA, _KA, _VA = 0, A_WIDTH, A_WIDTH + KV_WIDTH
_QB = A_WIDTH + 2 * KV_WIDTH
_KB, _VB = _QB + B_WIDTH, _QB + B_WIDTH + KV_WIDTH
_GA = _QB + B_WIDTH + 2 * KV_WIDTH
_GB = _GA + D_MODEL


def _params(sem, vmem=VMEM_LIMIT):
    return pltpu.CompilerParams(dimension_semantics=sem, vmem_limit_bytes=vmem)


def _dot(a, b):
    return jnp.dot(a, b, preferred_element_type=F32)


def _mod_kernel(c_ref, w_ref, b_ref, o_ref):
    c = c_ref[...]
    s = c * jax.nn.sigmoid(c)
    o_ref[...] = _dot(s.astype(BF16), w_ref[...].astype(BF16)) + b_ref[...]


def _modulation(c, w_ada, b_ada):
    bp, d = c.shape
    n = w_ada.shape[1]
    tn = 1536
    return pl.pallas_call(
        _mod_kernel,
        grid=(n // tn,),
        in_specs=[pl.BlockSpec((bp, d), lambda j: (0, 0)),
                  pl.BlockSpec((d, tn), lambda j: (0, j)),
                  pl.BlockSpec((1, tn), lambda j: (0, j))],
        out_specs=pl.BlockSpec((bp, tn), lambda j: (0, j)),
        out_shape=jax.ShapeDtypeStruct((bp, n), F32),
        compiler_params=_params(("arbitrary",)),
        name="modulation",
    )(c, w_ada, b_ada.reshape(1, n))


def _swap16(x):
    n = x.shape[-1]
    left = pltpu.roll(x, n - ROT_HALF, axis=1)
    right = pltpu.roll(x, ROT_HALF, axis=1)
    lane = lax.broadcasted_iota(I32, x.shape, 1)
    return jnp.where((lane % AXIS_DIM) < ROT_HALF, left, right)


def _inproj_kernel(x_ref, sc_ref, sh_ref, g1_ref, w_ref, aq_ref, bq_ref, ak_ref, bk_ref, hm_ref,
                   qa_ref, kat_ref, va_ref, qb_ref, kbt_ref, vb_ref, sga_ref, sgb_ref):
    tm = x_ref.shape[1]
    halves = [slice(0, tm // 2), slice(tm // 2, tm)]
    hs = []
    for r in halves:
        x = x_ref[0, r]
        ms = jnp.mean(x * x, axis=-1, keepdims=True)
        xn = x * lax.rsqrt(ms + EPS) * g1_ref[...]
        hs.append((xn * (1.0 + sc_ref[0]) + sh_ref[0]).astype(BF16))

    def proj(h, lo, width):
        return _dot(h, w_ref[:, lo:lo + width])

    def head_rms(q):
        width = q.shape[1]
        if width == LANES:
            msq = _dot((q * q).astype(BF16), hm_ref[:LANES, :LANES])
        else:
            msq = jnp.concatenate([_dot((q[:, j:j + 2 * LANES] * q[:, j:j + 2 * LANES]).astype(BF16), hm_ref[...])
                                   for j in range(0, width, 2 * LANES)], axis=1)
        return q * lax.rsqrt(msq + EPS)

    def rope(qh, a, b):
        reps = qh.shape[1] // LANES
        if reps > 1:
            a = jnp.concatenate([a] * reps, axis=1)
            b = jnp.concatenate([b] * reps, axis=1)
        return qh * a + _swap16(qh) * b

    for r, h in zip(halves, hs):
        sga_ref[0, r] = jax.nn.sigmoid(proj(h, _GA, D_MODEL)).astype(BF16)
    for r, h in zip(halves, hs):
        qa_ref[0, r] = rope(head_rms(proj(h, _QA, A_WIDTH)), aq_ref[r], bq_ref[r]).astype(BF16)
    for r, h in zip(halves, hs):
        kv = proj(h, _KA, 2 * KV_WIDTH)
        kat_ref[0, :, r] = rope(head_rms(kv[:, :KV_WIDTH]), ak_ref[r], bk_ref[r]).T.astype(BF16)
        va_ref[0, r] = kv[:, KV_WIDTH:].astype(BF16)
    for r, h in zip(halves, hs):
        qb_ref[0, r] = (proj(h, _QB, B_WIDTH) * (HEAD_DIM ** -0.5)).astype(BF16)
    for r, h in zip(halves, hs):
        kv = proj(h, _KB, 2 * KV_WIDTH)
        kbt_ref[0, :, r] = kv[:, :KV_WIDTH].T.astype(BF16)
        vb_ref[0, r] = kv[:, KV_WIDTH:].astype(BF16)
    for r, h in zip(halves, hs):
        sgb_ref[0, r] = jax.nn.sigmoid(proj(h, _GB, D_MODEL)).astype(BF16)


def _rope_tables(s, gain, scale):
    pos = jnp.arange(s, dtype=I32)
    row = (pos // GRID_W).astype(F32)
    col = (pos % GRID_W).astype(F32)
    inv_freq = ROPE_THETA ** (-jnp.arange(0, AXIS_DIM, 2, dtype=F32) / AXIS_DIM)
    ang_r = row[:, None] * inv_freq
    ang_c = col[:, None] * inv_freq
    cos64 = jnp.concatenate([jnp.cos(ang_r)] * 2 + [jnp.cos(ang_c)] * 2, axis=1)
    sin64 = jnp.concatenate([-jnp.sin(ang_r), jnp.sin(ang_r), -jnp.sin(ang_c), jnp.sin(ang_c)], axis=1)
    g = gain.astype(F32)
    h, a = ROT_HALF, AXIS_DIM
    gp = jnp.concatenate([g[h:a], g[0:h], g[a + h:2 * a], g[a:a + h]])
    a = cos64 * g[None, :] * scale
    b = sin64 * gp[None, :] * scale
    return jnp.concatenate([a, a], axis=1), jnp.concatenate([b, b], axis=1)


def _in_projection(x, sc1, sh1, g1, w_in, q_gain, k_gain, tm):
    b, s, d = x.shape
    aq, bq = _rope_tables(s, q_gain, HEAD_DIM ** -0.5 * math.log2(math.e))
    ak, bk = _rope_tables(s, k_gain, 1.0)
    head_mean = jnp.asarray(np.kron(np.eye(2 * LANES // HEAD_DIM), np.full((HEAD_DIM, HEAD_DIM), 1.0 / HEAD_DIM)),
                            BF16)
    tok = lambda w: pl.BlockSpec((1, tm, w), lambda i, t: (i, t, 0))
    tr = pl.BlockSpec((1, KV_WIDTH, tm), lambda i, t: (i, 0, t))
    vec = pl.BlockSpec((1, 1, d), lambda i, t: (i, 0, 0))
    tab = pl.BlockSpec((tm, LANES), lambda i, t: (t, 0))
    sd = jax.ShapeDtypeStruct
    return pl.pallas_call(
        _inproj_kernel,
        grid=(b, s // tm),
        in_specs=[tok(d), vec, vec,
                  pl.BlockSpec((1, d), lambda i, t: (0, 0)),
                  pl.BlockSpec((d, IN_COLS), lambda i, t: (0, 0)),
                  tab, tab, tab, tab,
                  pl.BlockSpec((2 * LANES, 2 * LANES), lambda i, t: (0, 0))],
        out_specs=[tok(A_WIDTH), tr, tok(KV_WIDTH), tok(B_WIDTH), tr, tok(KV_WIDTH), tok(d), tok(d)],
        out_shape=[sd((b, s, A_WIDTH), BF16), sd((b, KV_WIDTH, s), BF16), sd((b, s, KV_WIDTH), BF16),
                   sd((b, s, B_WIDTH), BF16), sd((b, KV_WIDTH, s), BF16), sd((b, s, KV_WIDTH), BF16),
                   sd((b, s, d), BF16), sd((b, s, d), BF16)],
        compiler_params=_params(("arbitrary", "arbitrary")),
        name="in_projection",
    )(x, sc1, sh1, g1.reshape(1, d), w_in, aq, bq, ak, bk, head_mean)


def _half_lane_variants(v, k_is_one):
    lane = lax.broadcasted_iota(I32, v.shape, 1)
    vr = pltpu.roll(v, HEAD_DIM, axis=1)
    own_lo = jnp.where(k_is_one, vr, v)
    own_hi = jnp.where(k_is_one, v, vr)
    lo = jnp.where(lane < HEAD_DIM, own_lo, 0.0)
    hi = jnp.where(lane >= HEAD_DIM, own_hi, 0.0)
    return lo, hi


_SUM_LANE_LO = LANES - 1
_SUM_LANE_HI = 0


def _gattn_kernel(q_ref, kt_ref, v_ref, o_ref, vv_ref, s_ref, qs_ref, os_ref, *, tu):
    k = pl.program_id(1)
    qi = pl.program_id(2)
    tq = q_ref.shape[1]
    group = A_HEADS // A_KV_HEADS
    nu = group * (tq // tu)

    @pl.when(qi == 0)
    def _():
        lo, hi = _half_lane_variants(v_ref[0].astype(F32), k == 1)
        lane = lax.broadcasted_iota(I32, lo.shape, 1)
        vv_ref[0] = jnp.where(lane == _SUM_LANE_LO, 1.0, lo).astype(BF16)
        vv_ref[1] = jnp.where(lane == _SUM_LANE_HI, 1.0, hi).astype(BF16)

    for u in range(nu):
        r, g = divmod(u, group)
        qs_ref[u] = q_ref[0, r * tu:(r + 1) * tu, g * HEAD_DIM:(g + 1) * HEAD_DIM]

    upb = s_ref.shape[0]

    def units(i, carry):
        maxes = []
        for j in range(upb):
            sc = _dot(qs_ref[upb * i + j], kt_ref[0])
            s_ref[j] = sc
            maxes.append(jnp.max(sc, axis=-1, keepdims=True))
        for j in range(upb):
            p = jnp.exp2(s_ref[j] - maxes[j])
            os_ref[upb * i + j] = _dot(p.astype(BF16), vv_ref[j % 2])
        return carry

    lax.fori_loop(0, nu // upb, units, 0)

    lane_o = lax.broadcasted_iota(I32, (tu, LANES), 1)
    for r in range(tq // tu):
        for j in range(group // 2):
            even = os_ref[r * group + 2 * j]
            odd = os_ref[r * group + 2 * j + 1]
            o = (jnp.where(lane_o < HEAD_DIM, even, 0.0) / even[:, _SUM_LANE_LO:_SUM_LANE_LO + 1]
                 + jnp.where(lane_o >= HEAD_DIM, odd, 0.0) / odd[:, _SUM_LANE_HI:_SUM_LANE_HI + 1])
            o_ref[0, r * tu:(r + 1) * tu, j * LANES:(j + 1) * LANES] = o.astype(o_ref.dtype)


def _global_attention(qa, kat, va, tq, tu):
    b, s, _ = qa.shape
    gw = A_WIDTH // A_KV_HEADS
    tq = min(tq, s)
    tu = min(tu, tq)
    nu = gw // HEAD_DIM * (tq // tu)
    upb = max(2, min(nu, LOGITS_VMEM_BUDGET // (tu * s * 4)))
    while nu % upb or upb % 2:
        upb -= 1
    return pl.pallas_call(
        functools.partial(_gattn_kernel, tu=tu),
        grid=(b, A_KV_HEADS, s // tq),
        in_specs=[pl.BlockSpec((1, tq, gw), lambda i, k, t: (i, t, k)),
                  pl.BlockSpec((1, HEAD_DIM, s), lambda i, k, t: (i, k, 0)),
                  pl.BlockSpec((1, s, KV_WIDTH), lambda i, k, t: (i, 0, 0))],
        out_specs=pl.BlockSpec((1, tq, gw), lambda i, k, t: (i, t, k)),
        out_shape=jax.ShapeDtypeStruct((b, s, A_WIDTH), BF16),
        scratch_shapes=[pltpu.VMEM((2, s, KV_WIDTH), BF16), pltpu.VMEM((upb, tu, s), F32),
                        pltpu.VMEM((nu, tu, HEAD_DIM), BF16), pltpu.VMEM((nu, tu, LANES), F32)],
        compiler_params=_params(("arbitrary", "arbitrary", "arbitrary")),
        name="global_attention",
    )(qa, kat, va)


def _t5_bucket_np(rel):
    half = N_BUCKETS // 2
    max_exact = half // 2
    base = (rel > 0).astype(np.int32) * half
    n = np.abs(rel)
    large = max_exact + (np.log(np.maximum(n, 1).astype(np.float32) / max_exact)
                         / math.log(MAX_DISTANCE / max_exact) * (half - max_exact)).astype(np.int32)
    large = np.minimum(large, half - 1)
    return base + np.where(n < max_exact, n, large)


def _stack_order(k):
    group = B_HEADS // B_KV_HEADS
    return [k * group + g for g in (0, 2, 1, 3)]


def _wattn_kernel(q_ref, ktp_ref, ktc_ref, ktn_ref, vp_ref, vc_ref, vn_ref, bias_ref, sink_ref, o_ref):
    t = pl.program_id(1)
    nt = pl.num_programs(1)
    span = Q_BLOCK + 2 * WINDOW
    nqb = q_ref.shape[1] // Q_BLOCK
    kt = jnp.concatenate([ktp_ref[0], ktc_ref[0], ktn_ref[0]], axis=1)
    v = jnp.concatenate([vp_ref[0], vc_ref[0], vn_ref[0]], axis=0).astype(F32)
    col = lax.broadcasted_iota(I32, (1, span), 1)
    first_ok = (col >= WINDOW) | (t > 0)
    last_ok = (col < WINDOW + Q_BLOCK) | (t < nt - 1)
    half_rows = 2 * Q_BLOCK
    values = []
    for k in range(B_KV_HEADS):
        lo, hi = _half_lane_variants(v, k == 1)
        values.append((lo.astype(BF16), hi.astype(BF16)))
    sinks = [sink_ref[k][:, 0:1] for k in range(B_KV_HEADS)]
    items = [(k, jb) for k in range(B_KV_HEADS) for jb in range(nqb)]
    keys = lambda jb: slice(jb * Q_BLOCK, jb * Q_BLOCK + span)
    rows = lambda jb: slice(jb * Q_BLOCK, (jb + 1) * Q_BLOCK)

    logits = []
    for k, jb in items:
        q4 = jnp.concatenate([q_ref[0, rows(jb), h * HEAD_DIM:(h + 1) * HEAD_DIM] for h in _stack_order(k)],
                             axis=0)
        lg = _dot(q4, kt[k * HEAD_DIM:(k + 1) * HEAD_DIM, keys(jb)]) + bias_ref[k]
        if jb == 0:
            lg = jnp.where(first_ok, lg, NEG_INF)
        if jb == nqb - 1:
            lg = jnp.where(last_ok, lg, NEG_INF)
        logits.append(lg)
    maxes = [jnp.maximum(jnp.max(lg, axis=-1, keepdims=True), sinks[k]) for (k, _), lg in zip(items, logits)]
    probs = [jnp.exp(lg - m) for lg, m in zip(logits, maxes)]
    dens = [jnp.sum(p, axis=-1, keepdims=True) + jnp.exp(sinks[k] - m)
            for (k, _), p, m in zip(items, probs, maxes)]
    outs = []
    for (k, jb), p, den in zip(items, probs, dens):
        pb = p.astype(BF16)
        vlo, vhi = values[k]
        o_even = _dot(pb[:half_rows], vlo[keys(jb)]) / den[:half_rows]
        o_odd = _dot(pb[half_rows:], vhi[keys(jb)]) / den[half_rows:]
        outs.append((o_even, o_odd))
    for (k, jb), (o_even, o_odd) in zip(items, outs):
        for j in range(2):
            pair = o_even[j * Q_BLOCK:(j + 1) * Q_BLOCK] + o_odd[j * Q_BLOCK:(j + 1) * Q_BLOCK]
            pj = 2 * k + j
            o_ref[0, rows(jb), pj * LANES:(pj + 1) * LANES] = pair.astype(o_ref.dtype)


def _window_attention(qb, kbt, vb, rel_bias, sink, tq):
    b, s, _ = qb.shape
    tq = min(tq, s)
    nb = s // Q_BLOCK
    per = tq // Q_BLOCK
    span = Q_BLOCK + 2 * WINDOW
    rel = np.arange(span)[None, :] - WINDOW - np.arange(Q_BLOCK)[:, None]
    band = np.abs(rel) <= WINDOW
    onehot = np.eye(N_BUCKETS, dtype=np.float32)[:, _t5_bucket_np(rel).reshape(-1)]
    bias = jnp.dot(rel_bias.astype(F32).T, jnp.asarray(onehot), precision=lax.Precision.HIGHEST)
    bias = jnp.where(jnp.asarray(band)[None], bias.reshape(B_HEADS, Q_BLOCK, span), NEG_INF)
    order = np.array([_stack_order(k) for k in range(B_KV_HEADS)])
    bias4 = bias[order].reshape(B_KV_HEADS, 4 * Q_BLOCK, span)
    sink4 = jnp.broadcast_to(sink.astype(F32)[order][:, :, None, None],
                             (B_KV_HEADS, 4, Q_BLOCK, LANES)).reshape(B_KV_HEADS, 4 * Q_BLOCK, LANES)
    prev = lambda i, t: jnp.maximum(t * per - 1, 0)
    nxt = lambda i, t: jnp.minimum((t + 1) * per, nb - 1)
    kt_edge = lambda f: pl.BlockSpec((1, KV_WIDTH, Q_BLOCK), lambda i, t: (i, 0, f(i, t)))
    v_edge = lambda f: pl.BlockSpec((1, Q_BLOCK, KV_WIDTH), lambda i, t: (i, f(i, t), 0))
    return pl.pallas_call(
        _wattn_kernel,
        grid=(b, s // tq),
        in_specs=[pl.BlockSpec((1, tq, B_WIDTH), lambda i, t: (i, t, 0)),
                  kt_edge(prev), pl.BlockSpec((1, KV_WIDTH, tq), lambda i, t: (i, 0, t)), kt_edge(nxt),
                  v_edge(prev), pl.BlockSpec((1, tq, KV_WIDTH), lambda i, t: (i, t, 0)), v_edge(nxt),
                  pl.BlockSpec((B_KV_HEADS, 4 * Q_BLOCK, span), lambda i, t: (0, 0, 0)),
                  pl.BlockSpec((B_KV_HEADS, 4 * Q_BLOCK, LANES), lambda i, t: (0, 0, 0))],
        out_specs=pl.BlockSpec((1, tq, B_WIDTH), lambda i, t: (i, t, 0)),
        out_shape=jax.ShapeDtypeStruct((b, s, B_WIDTH), BF16),
        compiler_params=_params(("arbitrary", "arbitrary")),
        name="window_attention",
    )(qb, kbt, kbt, kbt, vb, vb, vb, bias4, sink4)


def _pack_bf16_pairs(x):
    k = x.shape[1] // 2
    hi = pltpu.bitcast(x[:, :k].astype(BF16).astype(F32), U32)
    lo = pltpu.bitcast(x[:, k:].astype(BF16).astype(F32), U32)
    return hi | (lo >> 16)


def _unpack_bf16_pairs(w):
    hi = pltpu.bitcast(w & jnp.uint32(0xFFFF0000), F32)
    lo = pltpu.bitcast(w << 16, F32)
    return jnp.concatenate([hi, lo], axis=1).astype(BF16)


def _outproj_kernel(oa_ref, ob_ref, sga_ref, sgb_ref, x_ref, gt1_ref, sc2_ref, sh2_ref, g2_ref,
                    wa_ref, wb_ref, wo_ref, wr_ref, x1_ref, h2p_ref, afft_ref):
    tm = x_ref.shape[1]
    halves = [slice(0, tm // 2), slice(tm // 2, tm)]
    merged = []
    for r in halves:
        a = _dot(oa_ref[0, r], wa_ref[...])
        b = _dot(ob_ref[0, r], wb_ref[...])
        merged.append((sga_ref[0, r].astype(F32) * a + sgb_ref[0, r].astype(F32) * b).astype(BF16))
    h2s = []
    for r, mg in zip(halves, merged):
        x1 = x_ref[0, r] + gt1_ref[0] * _dot(mg, wo_ref[...])
        x1_ref[0, r] = x1
        ms = jnp.mean(x1 * x1, axis=-1, keepdims=True)
        h2s.append((x1 * lax.rsqrt(ms + EPS) * g2_ref[...]) * (1.0 + sc2_ref[0]) + sh2_ref[0])
    for r, h2 in zip(halves, h2s):
        h2p_ref[r] = _pack_bf16_pairs(h2).reshape(tm // 2, 1, PACKED)
        logits = _dot(h2.astype(BF16), wr_ref[...])
        lane = lax.broadcasted_iota(I32, logits.shape, 1)
        logits = jnp.where(lane < N_EXPERTS, logits, -jnp.inf)
        m = jnp.max(logits, axis=-1, keepdims=True)
        e = jnp.exp(logits - m)
        aff = e / jnp.sum(e, axis=-1, keepdims=True)
        afft_ref[:, r] = aff.T[:N_EXPERTS, :]


def _out_projection(oa, ob, sga, sgb, x, gt1, sc2, sh2, g2, wa, wb, wo, wr, tm):
    b, s, d = x.shape
    nt = s // tm
    n = b * s
    tok = lambda w: pl.BlockSpec((1, tm, w), lambda i, t: (i, t, 0))
    vec = pl.BlockSpec((1, 1, d), lambda i, t: (i, 0, 0))
    full = lambda r, c: pl.BlockSpec((r, c), lambda i, t: (0, 0))
    sd = jax.ShapeDtypeStruct
    return pl.pallas_call(
        _outproj_kernel,
        grid=(b, nt),
        in_specs=[tok(A_WIDTH), tok(B_WIDTH), tok(d), tok(d), tok(d), vec, vec, vec, full(1, d),
                  full(A_WIDTH, d), full(B_WIDTH, d), full(d, d), full(d, LANES)],
        out_specs=[tok(d),
                   pl.BlockSpec((tm, 1, PACKED), lambda i, t: (i * nt + t, 0, 0)),
                   pl.BlockSpec((N_EXPERTS, tm), lambda i, t: (0, i * nt + t))],
        out_shape=[sd((b, s, d), F32), sd((n, 1, PACKED), U32), sd((N_EXPERTS, n), F32)],
        compiler_params=_params(("arbitrary", "arbitrary")),
        name="out_projection",
    )(oa, ob, sga, sgb, x, gt1, sc2, sh2, g2.reshape(1, d), wa, wb, wo, wr)


def _threshold_kernel(aff_ref, thr_ref, cut_ref, *, cap):
    bits = pltpu.bitcast(aff_ref[...], I32)
    n = bits.shape[1]

    def value_step(i, lo):
        cand = lo | jnp.left_shift(jnp.int32(1), 30 - i)
        cnt = jnp.sum((bits >= cand).astype(F32), axis=1, keepdims=True)
        return jnp.where(cnt >= cap, cand, lo)

    thr = lax.fori_loop(0, 31, value_step, jnp.zeros((N_EXPERTS, 1), I32))
    need = cap - jnp.sum((bits > thr).astype(F32), axis=1, keepdims=True)
    eq = bits == thr
    tpos = lax.broadcasted_iota(I32, (1, n), 1)
    nbits = max(n.bit_length(), 1)

    def index_step(i, cut):
        cand = cut + jnp.left_shift(jnp.int32(1), nbits - 1 - i)
        below = jnp.sum(jnp.where(eq & (tpos < cand), 1.0, 0.0), axis=1, keepdims=True)
        return jnp.where((below <= need) & (cand <= n), cand, cut)

    cut = lax.fori_loop(0, nbits, index_step, jnp.zeros((N_EXPERTS, 1), I32))
    thr_ref[...] = jnp.broadcast_to(thr, thr_ref.shape)
    cut_ref[...] = jnp.broadcast_to(cut, cut_ref.shape)


def _capacity_threshold(afft, cap):
    e, n = afft.shape
    out = pl.BlockSpec((e, LANES), lambda i: (0, 0))
    return pl.pallas_call(
        functools.partial(_threshold_kernel, cap=cap),
        grid=(1,),
        in_specs=[pl.BlockSpec((e, n), lambda i: (0, 0))],
        out_specs=[out, out],
        out_shape=[jax.ShapeDtypeStruct((e, LANES), I32)] * 2,
        compiler_params=_params(("arbitrary",)),
        name="capacity_threshold",
    )(afft)


_AUX_ROWS = SUBLANES


def _prefix_kernel(aff_ref, thr_ref, cut_ref, u_ref, ls_ref, posm_ref, slot_ref, tokrow_ref, offs_ref,
                   run_ref, *, tb, ch):
    i = pl.program_id(0)

    @pl.when(i == 0)
    def _():
        run_ref[...] = jnp.zeros_like(run_ref)

    bits = pltpu.bitcast(aff_ref[...], I32)
    tpos = i * tb + lax.broadcasted_iota(I32, (1, tb), 1)
    thr = thr_ref[:, 0:1]
    sel = (bits > thr) | ((bits == thr) & (tpos < cut_ref[:, 0:1]))
    s = jnp.where(sel, 1.0, 0.0)
    for j in range(tb // ch):
        sj = s[:, j * ch:(j + 1) * ch]
        cntj = jnp.sum(sj, axis=0, keepdims=True)
        x = jnp.concatenate([sj, cntj, jnp.zeros((_AUX_ROWS - 1, ch), F32)], axis=0).astype(BF16)
        run = run_ref[:, 0:1]
        incl = _dot(x, u_ref[...]) + run
        posm_ref[:, j * ch:(j + 1) * ch] = jnp.where(sj > 0.0, incl[:N_EXPERTS] - sj, -1.0)
        tok_end = incl[N_EXPERTS:N_EXPERTS + 1]
        tok_off = tok_end - cntj
        slot_ref[:, j * ch:(j + 1) * ch] = tok_off + _dot(ls_ref[...], sj.astype(BF16))
        tokrow_ref[:, j * ch:(j + 1) * ch] = jnp.concatenate(
            [tok_off, tok_end, jnp.zeros((6, ch), F32)], axis=0)
        offs_ref[j] = jnp.broadcast_to(run[:N_EXPERTS], (N_EXPERTS, LANES))
        run_ref[...] = jnp.broadcast_to(incl[:, ch - 1:ch], run_ref.shape)


def _routing_prefix(afft, thr, cut, tb, ch):
    e, n = afft.shape
    upper = jnp.asarray(np.triu(np.ones((ch, ch), np.float32)), BF16)
    lower_strict = jnp.asarray(np.tril(np.ones((e, e), np.float32), -1), BF16)
    rows = lambda r: pl.BlockSpec((r, tb), lambda i: (0, i))
    const = lambda r, c: pl.BlockSpec((r, c), lambda i: (0, 0))
    sd = jax.ShapeDtypeStruct
    return pl.pallas_call(
        functools.partial(_prefix_kernel, tb=tb, ch=ch),
        grid=(n // tb,),
        in_specs=[rows(e), const(e, LANES), const(e, LANES), const(ch, ch), const(e, e)],
        out_specs=[rows(e), rows(e), rows(8), pl.BlockSpec((tb // ch, e, LANES), lambda i: (i, 0, 0))],
        out_shape=[sd((e, n), F32), sd((e, n), F32), sd((8, n), F32), sd((n // ch, e, LANES), F32)],
        scratch_shapes=[pltpu.VMEM((e + _AUX_ROWS, LANES), F32)],
        compiler_params=_params(("arbitrary",)),
        name="routing_prefix",
    )(afft, thr, cut, upper, lower_strict)


def _compact_kernel(win_ref, aff_ref, posm_ref, slot_ref, list_ref, *, tb, ch, nc):
    i = pl.program_id(0)

    @pl.when(i == 0)
    def _():
        list_ref[...] = jnp.zeros_like(list_ref)

    rank = lax.broadcasted_iota(I32, (2 * ch, ch), 0).astype(F32)
    tok_local = lax.broadcasted_iota(I32, (1, ch), 1).astype(F32)

    def expert_body(e, carry):
        for j in range(tb // ch):
            c = i * (tb // ch) + j
            w0 = win_ref[e * nc + c]
            lanes = slice(j * ch, (j + 1) * ch)
            rel = posm_ref[pl.ds(e, 1), lanes] - (w0 * ch).astype(F32)
            onehot_t = jnp.where(rank == rel, 1.0, 0.0).astype(BF16)
            slot = slot_ref[pl.ds(e, 1), lanes].astype(I32)
            g = aff_ref[pl.ds(e, 1), lanes]
            g_hi = g.astype(BF16).astype(F32)
            g_mid = (g - g_hi).astype(BF16).astype(F32)
            g_lo = g - g_hi - g_mid
            vals = jnp.concatenate(
                [tok_local, jnp.full((1, ch), c, I32).astype(F32),
                 (slot & (DIGIT - 1)).astype(F32), ((slot >> DIGIT_BITS) & (DIGIT - 1)).astype(F32),
                 (slot >> (2 * DIGIT_BITS)).astype(F32),
                 g_hi, g_mid, g_lo], axis=0).astype(BF16)
            out = lax.dot_general(vals, onehot_t, (((1,), (1,)), ((), ())), preferred_element_type=F32)
            list_ref[e, w0] += out[:, :ch]
            list_ref[e, w0 + 1] += out[:, ch:]
        return carry

    lax.fori_loop(0, N_EXPERTS, expert_body, 0)


def _routing_compact(win, afft, posm, slot, cap, tb, ch):
    e, n = afft.shape
    nc = n // ch
    nwin = cap // ch
    rows = pl.BlockSpec((e, tb), lambda i, w: (0, i))
    grid_spec = pltpu.PrefetchScalarGridSpec(
        num_scalar_prefetch=1,
        grid=(n // tb,),
        in_specs=[rows, rows, rows],
        out_specs=pl.BlockSpec((e, nwin + 2, 8, ch), lambda i, w: (0, 0, 0, 0)),
    )
    return pl.pallas_call(
        functools.partial(_compact_kernel, tb=tb, ch=ch, nc=nc),
        grid_spec=grid_spec,
        out_shape=jax.ShapeDtypeStruct((e, nwin + 2, 8, ch), F32),
        compiler_params=_params(("arbitrary",)),
        name="routing_compact",
    )(win, afft, posm, slot)


def _routing(afft, cap, tm, tt, blk, tb=2048, ch=256):
    e, n = afft.shape
    assert n // ch <= DIGIT and ch <= DIGIT and cap % ch == 0 and cap % tm == 0 and e * cap < (1 << 24)
    tb = min(tb, n)
    thr, cut = _capacity_threshold(afft, cap)
    posm, slot, tokrow, offs = _routing_prefix(afft, thr, cut, tb, ch)
    win = (offs[:, :, 0].astype(I32) // ch).T.reshape(-1)
    lists = _routing_compact(win, afft, posm, slot, cap, tb, ch)[:, :cap // ch]
    nsteps = e * cap // tm
    idx = (lists[:, :, 1] * ch + lists[:, :, 0]).astype(I32).reshape(nsteps, tm)
    dst = (lists[:, :, 2] + lists[:, :, 3] * float(DIGIT) + lists[:, :, 4] * float(DIGIT * DIGIT)).astype(I32)
    dst = dst.reshape(nsteps, tm)
    gate = ((lists[:, :, 5] + lists[:, :, 6]) + lists[:, :, 7]).reshape(nsteps, 1, tm)
    meta = jnp.concatenate([idx, dst], axis=1)
    gate8 = jnp.broadcast_to(gate, (nsteps, SUBLANES, tm))

    npairs = e * cap
    ntile, nblk = n // tt, npairs // blk
    start = tokrow[0, ::tt].astype(I32)
    end = jnp.concatenate([start[1:], jnp.full((1,), npairs, I32)])
    b0 = jnp.minimum(start // blk, nblk - 1)
    b1 = jnp.where(end > start, (end - 1) // blk, b0)
    nb = b1 - b0 + 1
    wend = jnp.cumsum(nb)
    woff = wend - nb
    w = jnp.arange(nblk + ntile, dtype=I32)
    wt = jnp.minimum(jnp.sum((w[:, None] >= wend[None, :]).astype(I32), axis=1), ntile - 1)
    valid = (w < wend[-1]).astype(I32)
    wblk = jnp.where(valid == 1, b0[wt] + (w - woff[wt]), b1[ntile - 1]).astype(I32)
    return meta, gate8, tokrow, wt, wblk, valid


def _moe_kernel(meta_hbm, gate_ref, h2_hbm, wg_ref, wu_ref, wd_ref, z_hbm,
                meta_smem, xbuf, x2d, zbuf, sem_m, sem_g, sem_s, *, tm, nsteps):
    nt = pl.num_programs(1)
    s = pl.program_id(0) * nt + pl.program_id(1)
    slot = s % 2

    def meta_copy(step, mslot):
        return pltpu.make_async_copy(meta_hbm.at[step], meta_smem.at[mslot], sem_m)

    def issue_gather(mslot, bslot):
        for r in range(tm):
            tok = meta_smem[mslot, r]
            pltpu.make_async_copy(h2_hbm.at[tok], xbuf.at[bslot, r], sem_g.at[bslot]).start(priority=r % 2)

    def wait_gather(bslot):
        pltpu.make_async_copy(h2_hbm.at[pl.ds(0, tm)], xbuf.at[bslot], sem_g.at[bslot]).wait()

    def issue_scatter(mslot, bslot):
        for r in range(tm):
            dst = meta_smem[mslot, tm + r]
            pltpu.make_async_copy(zbuf.at[bslot, r], z_hbm.at[dst], sem_s.at[bslot]).start(priority=r % 2)

    def wait_scatter(bslot):
        pltpu.make_async_copy(zbuf.at[bslot], z_hbm.at[pl.ds(0, tm)], sem_s.at[bslot]).wait()

    last = nsteps - 1

    @pl.when(s == 0)
    def _():
        for step, mslot in ((0, 0), (min(1, last), 1)):
            c = meta_copy(step, mslot)
            c.start()
            c.wait()
        for r in range(tm):
            meta_smem[3, tm + r] = nsteps * tm + r
        zbuf[1] = jnp.zeros(zbuf.shape[1:], zbuf.dtype)
        issue_gather(0, 0)

    wait_gather(slot)

    @pl.when(s >= 1)
    def _():
        wait_scatter(slot)

    x2d[...] = xbuf[slot].reshape(tm, PACKED)
    prefetch = meta_copy(jnp.minimum(s + 2, last), (s + 2) % 4)
    prefetch.start()
    issue_gather((s + 1) % 4, 1 - slot)
    issue_scatter((s + 3) % 4, 1 - slot)
    gcol = gate_ref[...].T[:, 0:1]
    sub = min(tm, MLP_ROWS)
    for i in range(tm // sub):
        xe = _unpack_bf16_pairs(x2d[i * sub:(i + 1) * sub])
        gate = _dot(xe, wg_ref[...])
        up = _dot(xe, wu_ref[...])
        hid = (gate * jax.nn.sigmoid(gate) * up).astype(BF16)
        halves = [slice(0, sub // 2), slice(sub // 2, sub)]
        ye = [_dot(hid[h], wd_ref[...]) for h in halves]
        for h, y in zip(halves, ye):
            rows = slice(i * sub + h.start, i * sub + h.stop)
            zbuf[slot, rows] = _pack_bf16_pairs(y * gcol[rows]).reshape(sub // 2, 1, PACKED)
    prefetch.wait()

    @pl.when(s == last)
    def _():
        issue_scatter(s % 4, slot)
        wait_scatter(slot)
        wait_scatter(1 - slot)
        wait_gather(1 - slot)


def _expert_mlp(meta, gate8, h2p, wg, wu, wd, tm):
    nsteps = meta.shape[0]
    nt = nsteps // N_EXPERTS
    d, f = wg.shape[1], wg.shape[2]
    step = lambda e, i: (e * nt + i, 0, 0)
    return pl.pallas_call(
        functools.partial(_moe_kernel, tm=tm, nsteps=nsteps),
        grid=(N_EXPERTS, nt),
        in_specs=[pl.BlockSpec(memory_space=pl.ANY),
                  pl.BlockSpec((None, SUBLANES, tm), step),
                  pl.BlockSpec(memory_space=pl.ANY),
                  pl.BlockSpec((None, d, f), lambda e, i: (e, 0, 0)),
                  pl.BlockSpec((None, d, f), lambda e, i: (e, 0, 0)),
                  pl.BlockSpec((None, f, d), lambda e, i: (e, 0, 0))],
        out_specs=pl.BlockSpec(memory_space=pl.ANY),
        out_shape=jax.ShapeDtypeStruct(((nsteps + 1) * tm, 1, PACKED), U32),
        scratch_shapes=[pltpu.SMEM((4, 2 * tm), I32),
                        pltpu.VMEM((2, tm, 1, PACKED), U32),
                        pltpu.VMEM((tm, PACKED), U32),
                        pltpu.VMEM((2, tm, 1, PACKED), U32),
                        pltpu.SemaphoreType.DMA,
                        pltpu.SemaphoreType.DMA((2,)),
                        pltpu.SemaphoreType.DMA((2,))],
        compiler_params=_params(("arbitrary", "arbitrary")),
        name="expert_mlp",
    )(meta, gate8, h2p, wg, wu, wd)


def _combine_kernel(wtile_ref, wblk_ref, wvalid_ref, z_ref, tokrow_ref, x1_ref, gt2_ref, gf_ref,
                    y_ref, acc_ref, z2d, *, tt):
    w = pl.program_id(0)
    nw = pl.num_programs(0)
    tile = wtile_ref[w]
    first = jnp.logical_or(w == 0, wtile_ref[jnp.maximum(w - 1, 0)] != tile)
    last = jnp.logical_or(w == nw - 1, wtile_ref[jnp.minimum(w + 1, nw - 1)] != tile)

    @pl.when(first)
    def _():
        acc_ref[...] = jnp.zeros_like(acc_ref)

    @pl.when(wvalid_ref[w] == 1)
    def _():
        blk = z2d.shape[0]
        z2d[...] = z_ref[...].reshape(z2d.shape)
        zrows = _unpack_bf16_pairs(z2d[...])
        pair = (wblk_ref[w] * blk + lax.broadcasted_iota(I32, (blk, tt), 0)).astype(F32)
        owned = (pair >= tokrow_ref[0:1, :]) & (pair < tokrow_ref[1:2, :])
        onehot = jnp.where(owned, 1.0, 0.0).T.astype(BF16)
        acc_ref[...] += _dot(onehot, zrows)

    @pl.when(last)
    def _():
        x2 = x1_ref[...] + gt2_ref[0] * acc_ref[...]
        ms = jnp.mean(x2 * x2, axis=-1, keepdims=True)
        y_ref[...] = x2 * lax.rsqrt(ms + EPS) * gf_ref[...]


def _combine(wtile, wblk, wvalid, z, tokrow, x1, gt2, gf, s, tt, blk):
    n, d = x1.shape
    nw = wtile.shape[0]
    grid_spec = pltpu.PrefetchScalarGridSpec(
        num_scalar_prefetch=3,
        grid=(nw,),
        in_specs=[pl.BlockSpec((blk, 1, PACKED), lambda w, wt, wb, wv: (wb[w], 0, 0)),
                  pl.BlockSpec((8, tt), lambda w, wt, wb, wv: (0, wt[w])),
                  pl.BlockSpec((tt, d), lambda w, wt, wb, wv: (wt[w], 0)),
                  pl.BlockSpec((1, 1, d), lambda w, wt, wb, wv: (wt[w] * tt // s, 0, 0)),
                  pl.BlockSpec((1, d), lambda w, wt, wb, wv: (0, 0))],
        out_specs=pl.BlockSpec((tt, d), lambda w, wt, wb, wv: (wt[w], 0)),
        scratch_shapes=[pltpu.VMEM((tt, d), F32), pltpu.VMEM((blk, PACKED), U32)],
    )
    return pl.pallas_call(
        functools.partial(_combine_kernel, tt=tt),
        grid_spec=grid_spec,
        out_shape=jax.ShapeDtypeStruct((n, d), F32),
        compiler_params=_params(("arbitrary",)),
        name="combine_final_norm",
    )(wtile, wblk, wvalid, z, tokrow, x1, gt2, gf.reshape(1, d))


def _trunk(x, mod, w, tm_proj=512, tq=512, tu=256, tq_win=1024, tm_moe=1024, route_ch=256, tt=512, blk=512):
    b, s, d = x.shape
    n = b * s
    cap = CAPACITY_FACTOR * n // N_EXPERTS
    sh1, sc1, gt1, sh2, sc2, gt2 = [m.reshape(b, 1, d) for m in jnp.split(mod, N_MOD, axis=-1)]

    qa, kat, va, qb, kbt, vb, sga, sgb = _in_projection(
        x, sc1, sh1, w["norm1_g"], w["w_in"], w["q_norm_g"], w["k_norm_g"], tm_proj)
    oa = _global_attention(qa, kat, va, tq, tu)
    ob = _window_attention(qb, kbt, vb, w["rel_bias"], w["sink"], tq_win)
    x1, h2p, afft = _out_projection(oa, ob, sga, sgb, x, gt1, sc2, sh2, w["norm2_g"],
                                    w["w_branch_a"], w["w_branch_b"], w["w_out"], w["w_router"], tm_proj)
    meta, gate8, tokrow, wtile, wblk, wvalid = _routing(afft, cap, tm_moe, tt, blk, ch=route_ch)
    z = _expert_mlp(meta, gate8, h2p, w["w_e_gate"], w["w_e_up"], w["w_e_down"], tm_moe)
    y = _combine(wtile, wblk, wvalid, z, tokrow, x1.reshape(n, d), gt2, w["norm_f_g"], s, tt, blk)
    return y.reshape(b, s, d)


def kernel(x_prompt, x_sample, c_prompt, c_sample, w_ada, b_ada, norm1_g, w_in, q_norm_g, k_norm_g, sink,
           w_branch_a, w_branch_b, w_out, norm2_g, w_router, w_e_gate, w_e_up, w_e_down, rel_bias, norm_f_g):
    assert w_ada.shape[0] == 1, "single layer"
    bp, bs = c_prompt.shape[0], c_sample.shape[0]
    rows = -(-(bp + bs) // 16) * 16
    c = jnp.concatenate([c_prompt, c_sample, jnp.zeros((rows - bp - bs, D_MODEL), F32)], axis=0)
    mod = _modulation(c, w_ada[0], b_ada[0])

    router = jnp.zeros((D_MODEL, LANES), BF16).at[:, :N_EXPERTS].set(w_router[0].astype(BF16))
    w = dict(norm1_g=norm1_g[0], w_in=w_in[0].astype(BF16), q_norm_g=q_norm_g[0], k_norm_g=k_norm_g[0],
             sink=sink[0], w_branch_a=w_branch_a[0].astype(BF16), w_branch_b=w_branch_b[0].astype(BF16),
             w_out=w_out[0].astype(BF16), norm2_g=norm2_g[0], w_router=router,
             w_e_gate=w_e_gate[0].astype(BF16), w_e_up=w_e_up[0].astype(BF16),
             w_e_down=w_e_down[0].astype(BF16), rel_bias=rel_bias, norm_f_g=norm_f_g)
    y_prompt = _trunk(x_prompt, mod[:bp], w)
    y_sample = _trunk(x_sample, mod[bp:bp + bs], w)
    return (y_prompt, y_sample)
```

```python
import functools
import math

import jax
import jax.numpy as jnp
import numpy as np
from jax import lax
from jax.experimental import pallas as pl
from jax.experimental.pallas import tpu as pltpu

F32 = jnp.float32
BF16 = jnp.bfloat16
I32 = jnp.int32
U32 = jnp.uint32

D_MODEL = 1024
HEAD_DIM = 64
A_HEADS = 8
A_KV_HEADS = 2
B_HEADS = 8
B_KV_HEADS = 2
A_WIDTH = A_HEADS * HEAD_DIM
B_WIDTH = B_HEADS * HEAD_DIM
KV_WIDTH = A_KV_HEADS * HEAD_DIM
Q_BLOCK = 128
WINDOW = 128
GRID_W = 64
ROPE_THETA = 10000.0
AXIS_DIM = HEAD_DIM // 2
N_BUCKETS = 32
MAX_DISTANCE = 128
N_EXPERTS = 16
CAPACITY_FACTOR = 2
D_FF = 2048
N_MOD = 6
EPS = 1e-6
NEG_INF = -1e30
IN_COLS = A_WIDTH + 2 * KV_WIDTH + B_WIDTH + 2 * KV_WIDTH + 2 * D_MODEL

LANES = 128
SUBLANES = 8
ROT_HALF = AXIS_DIM // 2
DIGIT_BITS = 8
DIGIT = 1 << DIGIT_BITS
PACKED = D_MODEL // 2
VMEM_LIMIT = 56 * 1024 * 1024
LOGITS_VMEM_BUDGET = 16 * 1024 * 1024

_QA, _KA, _VA = 0, A_WIDTH, A_WIDTH + KV_WIDTH
_QB = A_WIDTH + 2 * KV_WIDTH
_KB, _VB = _QB + B_WIDTH, _QB + B_WIDTH + KV_WIDTH
_GA = _QB + B_WIDTH + 2 * KV_WIDTH
_GB = _GA + D_MODEL


def _params(sem, vmem=VMEM_LIMIT):
    return pltpu.CompilerParams(dimension_semantics=sem, vmem_limit_bytes=vmem)


def _dot(a, b):
    return jnp.dot(a, b, preferred_element_type=F32)


def _mod_kernel(c_ref, w_ref, b_ref, o_ref):
    c = c_ref[...]
    s = c * jax.nn.sigmoid(c)
    o_ref[...] = _dot(s.astype(BF16), w_ref[...].astype(BF16)) + b_ref[...]


def _modulation(c, w_ada, b_ada):
    bp, d = c.shape
    n = w_ada.shape[1]
    tn = 1536
    return pl.pallas_call(
        _mod_kernel,
        grid=(n // tn,),
        in_specs=[pl.BlockSpec((bp, d), lambda j: (0, 0)),
                  pl.BlockSpec((d, tn), lambda j: (0, j)),
                  pl.BlockSpec((1, tn), lambda j: (0, j))],
        out_specs=pl.BlockSpec((bp, tn), lambda j: (0, j)),
        out_shape=jax.ShapeDtypeStruct((bp, n), F32),
        compiler_params=_params(("arbitrary",)),
        name="modulation",
    )(c, w_ada, b_ada.reshape(1, n))


def _swap16(x):
    n = x.shape[-1]
    left = pltpu.roll(x, n - ROT_HALF, axis=1)
    right = pltpu.roll(x, ROT_HALF, axis=1)
    lane = lax.broadcasted_iota(I32, x.shape, 1)
    return jnp.where((lane % AXIS_DIM) < ROT_HALF, left, right)


def _inproj_kernel(x_ref, sc_ref, sh_ref, g1_ref, w_ref, aq_ref, bq_ref, ak_ref, bk_ref, hm_ref,
                   qa_ref, kat_ref, va_ref, qb_ref, kbt_ref, vb_ref, sga_ref, sgb_ref):
    tm = x_ref.shape[1]
    halves = [slice(0, tm // 2), slice(tm // 2, tm)]
    hs = []
    for r in halves:
        x = x_ref[0, r]
        ms = jnp.mean(x * x, axis=-1, keepdims=True)
        xn = x * lax.rsqrt(ms + EPS) * g1_ref[...]
        hs.append((xn * (1.0 + sc_ref[0]) + sh_ref[0]).astype(BF16))

    def proj(h, lo, width):
        return _dot(h, w_ref[:, lo:lo + width])

    def head_rms(q):
        width = q.shape[1]
        if width == LANES:
            msq = _dot((q * q).astype(BF16), hm_ref[:LANES, :LANES])
        else:
            msq = jnp.concatenate([_dot((q[:, j:j + 2 * LANES] * q[:, j:j + 2 * LANES]).astype(BF16), hm_ref[...])
                                   for j in range(0, width, 2 * LANES)], axis=1)
        return q * lax.rsqrt(msq + EPS)

    def rope(qh, a, b):
        reps = qh.shape[1] // LANES
        if reps > 1:
            a = jnp.concatenate([a] * reps, axis=1)
            b = jnp.concatenate([b] * reps, axis=1)
        return qh * a + _swap16(qh) * b

    for r, h in zip(halves, hs):
        sga_ref[0, r] = jax.nn.sigmoid(proj(h, _GA, D_MODEL)).astype(BF16)
    for r, h in zip(halves, hs):
        qa_ref[0, r] = rope(head_rms(proj(h, _QA, A_WIDTH)), aq_ref[r], bq_ref[r]).astype(BF16)
    for r, h in zip(halves, hs):
        kv = proj(h, _KA, 2 * KV_WIDTH)
        kat_ref[0, :, r] = rope(head_rms(kv[:, :KV_WIDTH]), ak_ref[r], bk_ref[r]).T.astype(BF16)
        va_ref[0, r] = kv[:, KV_WIDTH:].astype(BF16)
    for r, h in zip(halves, hs):
        qb_ref[0, r] = (proj(h, _QB, B_WIDTH) * (HEAD_DIM ** -0.5)).astype(BF16)
    for r, h in zip(halves, hs):
        kv = proj(h, _KB, 2 * KV_WIDTH)
        kbt_ref[0, :, r] = kv[:, :KV_WIDTH].T.astype(BF16)
        vb_ref[0, r] = kv[:, KV_WIDTH:].astype(BF16)
    for r, h in zip(halves, hs):
        sgb_ref[0, r] = jax.nn.sigmoid(proj(h, _GB, D_MODEL)).astype(BF16)


def _rope_tables(s, gain, scale):
    pos = jnp.arange(s, dtype=I32)
    row = (pos // GRID_W).astype(F32)
    col = (pos % GRID_W).astype(F32)
    inv_freq = ROPE_THETA ** (-jnp.arange(0, AXIS_DIM, 2, dtype=F32) / AXIS_DIM)
    ang_r = row[:, None] * inv_freq
    ang_c = col[:, None] * inv_freq
    cos64 = jnp.concatenate([jnp.cos(ang_r)] * 2 + [jnp.cos(ang_c)] * 2, axis=1)
    sin64 = jnp.concatenate([-jnp.sin(ang_r), jnp.sin(ang_r), -jnp.sin(ang_c), jnp.sin(ang_c)], axis=1)
    g = gain.astype(F32)
    h, a = ROT_HALF, AXIS_DIM
    gp = jnp.concatenate([g[h:a], g[0:h], g[a + h:2 * a], g[a:a + h]])
    a = cos64 * g[None, :] * scale
    b = sin64 * gp[None, :] * scale
    return jnp.concatenate([a, a], axis=1), jnp.concatenate([b, b], axis=1)


def _in_projection(x, sc1, sh1, g1, w_in, q_gain, k_gain, tm):
    b, s, d = x.shape
    aq, bq = _rope_tables(s, q_gain, HEAD_DIM ** -0.5 * math.log2(math.e))
    ak, bk = _rope_tables(s, k_gain, 1.0)
    head_mean = jnp.asarray(np.kron(np.eye(2 * LANES // HEAD_DIM), np.full((HEAD_DIM, HEAD_DIM), 1.0 / HEAD_DIM)),
                            BF16)
    tok = lambda w: pl.BlockSpec((1, tm, w), lambda i, t: (i, t, 0))
    tr = pl.BlockSpec((1, KV_WIDTH, tm), lambda i, t: (i, 0, t))
    vec = pl.BlockSpec((1, 1, d), lambda i, t: (i, 0, 0))
    tab = pl.BlockSpec((tm, LANES), lambda i, t: (t, 0))
    sd = jax.ShapeDtypeStruct
    return pl.pallas_call(
        _inproj_kernel,
        grid=(b, s // tm),
        in_specs=[tok(d), vec, vec,
                  pl.BlockSpec((1, d), lambda i, t: (0, 0)),
                  pl.BlockSpec((d, IN_COLS), lambda i, t: (0, 0)),
                  tab, tab, tab, tab,
                  pl.BlockSpec((2 * LANES, 2 * LANES), lambda i, t: (0, 0))],
        out_specs=[tok(A_WIDTH), tr, tok(KV_WIDTH), tok(B_WIDTH), tr, tok(KV_WIDTH), tok(d), tok(d)],
        out_shape=[sd((b, s, A_WIDTH), BF16), sd((b, KV_WIDTH, s), BF16), sd((b, s, KV_WIDTH), BF16),
                   sd((b, s, B_WIDTH), BF16), sd((b, KV_WIDTH, s), BF16), sd((b, s, KV_WIDTH), BF16),
                   sd((b, s, d), BF16), sd((b, s, d), BF16)],
        compiler_params=_params(("parallel", "parallel")),
        name="in_projection",
    )(x, sc1, sh1, g1.reshape(1, d), w_in, aq, bq, ak, bk, head_mean)


def _half_lane_variants(v, k_is_one):
    lane = lax.broadcasted_iota(I32, v.shape, 1)
    vr = pltpu.roll(v, HEAD_DIM, axis=1)
    own_lo = jnp.where(k_is_one, vr, v)
    own_hi = jnp.where(k_is_one, v, vr)
    lo = jnp.where(lane < HEAD_DIM, own_lo, 0.0)
    hi = jnp.where(lane >= HEAD_DIM, own_hi, 0.0)
    return lo, hi


_SUM_LANE_LO = LANES - 1
_SUM_LANE_HI = 0


def _gattn_kernel(q_ref, kt_ref, v_ref, o_ref, vv_ref, s_ref, qs_ref, os_ref, *, tu):
    k = pl.program_id(1)
    qi = pl.program_id(2)
    tq = q_ref.shape[1]
    group = A_HEADS // A_KV_HEADS
    nu = group * (tq // tu)

    @pl.when(qi == 0)
    def _():
        lo, hi = _half_lane_variants(v_ref[0].astype(F32), k == 1)
        lane = lax.broadcasted_iota(I32, lo.shape, 1)
        vv_ref[0] = jnp.where(lane == _SUM_LANE_LO, 1.0, lo).astype(BF16)
        vv_ref[1] = jnp.where(lane == _SUM_LANE_HI, 1.0, hi).astype(BF16)

    for u in range(nu):
        r, g = divmod(u, group)
        qs_ref[u] = q_ref[0, r * tu:(r + 1) * tu, g * HEAD_DIM:(g + 1) * HEAD_DIM]

    upb = s_ref.shape[0]

    def units(i, carry):
        maxes = []
        for j in range(upb):
            sc = _dot(qs_ref[upb * i + j], kt_ref[0])
            s_ref[j] = sc
            maxes.append(jnp.max(sc, axis=-1, keepdims=True))
        for j in range(upb):
            p = jnp.exp2(s_ref[j] - maxes[j])
            os_ref[upb * i + j] = _dot(p.astype(BF16), vv_ref[j % 2])
        return carry

    lax.fori_loop(0, nu // upb, units, 0)

    lane_o = lax.broadcasted_iota(I32, (tu, LANES), 1)
    for r in range(tq // tu):
        for j in range(group // 2):
            even = os_ref[r * group + 2 * j]
            odd = os_ref[r * group + 2 * j + 1]
            o = (jnp.where(lane_o < HEAD_DIM, even, 0.0) / even[:, _SUM_LANE_LO:_SUM_LANE_LO + 1]
                 + jnp.where(lane_o >= HEAD_DIM, odd, 0.0) / odd[:, _SUM_LANE_HI:_SUM_LANE_HI + 1])
            o_ref[0, r * tu:(r + 1) * tu, j * LANES:(j + 1) * LANES] = o.astype(o_ref.dtype)


def _global_attention(qa, kat, va, tq, tu):
    b, s, _ = qa.shape
    gw = A_WIDTH // A_KV_HEADS
    tq = min(tq, s)
    tu = min(tu, tq)
    nu = gw // HEAD_DIM * (tq // tu)
    upb = max(2, min(nu, LOGITS_VMEM_BUDGET // (tu * s * 4)))
    while nu % upb or upb % 2:
        upb -= 1
    return pl.pallas_call(
        functools.partial(_gattn_kernel, tu=tu),
        grid=(b, A_KV_HEADS, s // tq),
        in_specs=[pl.BlockSpec((1, tq, gw), lambda i, k, t: (i, t, k)),
                  pl.BlockSpec((1, HEAD_DIM, s), lambda i, k, t: (i, k, 0)),
                  pl.BlockSpec((1, s, KV_WIDTH), lambda i, k, t: (i, 0, 0))],
        out_specs=pl.BlockSpec((1, tq, gw), lambda i, k, t: (i, t, k)),
        out_shape=jax.ShapeDtypeStruct((b, s, A_WIDTH), BF16),
        scratch_shapes=[pltpu.VMEM((2, s, KV_WIDTH), BF16), pltpu.VMEM((upb, tu, s), F32),
                        pltpu.VMEM((nu, tu, HEAD_DIM), BF16), pltpu.VMEM((nu, tu, LANES), F32)],
        compiler_params=_params(("arbitrary", "arbitrary", "arbitrary")),
        name="global_attention",
    )(qa, kat, va)


def _t5_bucket_np(rel):
    half = N_BUCKETS // 2
    max_exact = half // 2
    base = (rel > 0).astype(np.int32) * half
    n = np.abs(rel)
    large = max_exact + (np.log(np.maximum(n, 1).astype(np.float32) / max_exact)
                         / math.log(MAX_DISTANCE / max_exact) * (half - max_exact)).astype(np.int32)
    large = np.minimum(large, half - 1)
    return base + np.where(n < max_exact, n, large)


def _stack_order(k):
    group = B_HEADS // B_KV_HEADS
    return [k * group + g for g in (0, 2, 1, 3)]


def _wattn_kernel(q_ref, ktp_ref, ktc_ref, ktn_ref, vp_ref, vc_ref, vn_ref, bias_ref, sink_ref, o_ref):
    t = pl.program_id(1)
    nt = pl.num_programs(1)
    span = Q_BLOCK + 2 * WINDOW
    nqb = q_ref.shape[1] // Q_BLOCK
    kt = jnp.concatenate([ktp_ref[0], ktc_ref[0], ktn_ref[0]], axis=1)
    v = jnp.concatenate([vp_ref[0], vc_ref[0], vn_ref[0]], axis=0).astype(F32)
    col = lax.broadcasted_iota(I32, (1, span), 1)
    first_ok = (col >= WINDOW) | (t > 0)
    last_ok = (col < WINDOW + Q_BLOCK) | (t < nt - 1)
    half_rows = 2 * Q_BLOCK
    values = []
    for k in range(B_KV_HEADS):
        lo, hi = _half_lane_variants(v, k == 1)
        values.append((lo.astype(BF16), hi.astype(BF16)))
    sinks = [sink_ref[k][:, 0:1] for k in range(B_KV_HEADS)]
    items = [(k, jb) for k in range(B_KV_HEADS) for jb in range(nqb)]
    keys = lambda jb: slice(jb * Q_BLOCK, jb * Q_BLOCK + span)
    rows = lambda jb: slice(jb * Q_BLOCK, (jb + 1) * Q_BLOCK)

    logits = []
    for k, jb in items:
        q4 = jnp.concatenate([q_ref[0, rows(jb), h * HEAD_DIM:(h + 1) * HEAD_DIM] for h in _stack_order(k)],
                             axis=0)
        lg = _dot(q4, kt[k * HEAD_DIM:(k + 1) * HEAD_DIM, keys(jb)]) + bias_ref[k]
        if jb == 0:
            lg = jnp.where(first_ok, lg, NEG_INF)
        if jb == nqb - 1:
            lg = jnp.where(last_ok, lg, NEG_INF)
        logits.append(lg)
    maxes = [jnp.maximum(jnp.max(lg, axis=-1, keepdims=True), sinks[k]) for (k, _), lg in zip(items, logits)]
    probs = [jnp.exp(lg - m) for lg, m in zip(logits, maxes)]
    dens = [jnp.sum(p, axis=-1, keepdims=True) + jnp.exp(sinks[k] - m)
            for (k, _), p, m in zip(items, probs, maxes)]
    outs = []
    for (k, jb), p, den in zip(items, probs, dens):
        pb = p.astype(BF16)
        vlo, vhi = values[k]
        o_even = _dot(pb[:half_rows], vlo[keys(jb)]) / den[:half_rows]
        o_odd = _dot(pb[half_rows:], vhi[keys(jb)]) / den[half_rows:]
        outs.append((o_even, o_odd))
    for (k, jb), (o_even, o_odd) in zip(items, outs):
        for j in range(2):
            pair = o_even[j * Q_BLOCK:(j + 1) * Q_BLOCK] + o_odd[j * Q_BLOCK:(j + 1) * Q_BLOCK]
            pj = 2 * k + j
            o_ref[0, rows(jb), pj * LANES:(pj + 1) * LANES] = pair.astype(o_ref.dtype)


def _window_attention(qb, kbt, vb, rel_bias, sink, tq):
    b, s, _ = qb.shape
    tq = min(tq, s)
    nb = s // Q_BLOCK
    per = tq // Q_BLOCK
    span = Q_BLOCK + 2 * WINDOW
    rel = np.arange(span)[None, :] - WINDOW - np.arange(Q_BLOCK)[:, None]
    band = np.abs(rel) <= WINDOW
    onehot = np.eye(N_BUCKETS, dtype=np.float32)[:, _t5_bucket_np(rel).reshape(-1)]
    bias = jnp.dot(rel_bias.astype(F32).T, jnp.asarray(onehot), precision=lax.Precision.HIGHEST)
    bias = jnp.where(jnp.asarray(band)[None], bias.reshape(B_HEADS, Q_BLOCK, span), NEG_INF)
    order = np.array([_stack_order(k) for k in range(B_KV_HEADS)])
    bias4 = bias[order].reshape(B_KV_HEADS, 4 * Q_BLOCK, span)
    sink4 = jnp.broadcast_to(sink.astype(F32)[order][:, :, None, None],
                             (B_KV_HEADS, 4, Q_BLOCK, LANES)).reshape(B_KV_HEADS, 4 * Q_BLOCK, LANES)
    prev = lambda i, t: jnp.maximum(t * per - 1, 0)
    nxt = lambda i, t: jnp.minimum((t + 1) * per, nb - 1)
    kt_edge = lambda f: pl.BlockSpec((1, KV_WIDTH, Q_BLOCK), lambda i, t: (i, 0, f(i, t)))
    v_edge = lambda f: pl.BlockSpec((1, Q_BLOCK, KV_WIDTH), lambda i, t: (i, f(i, t), 0))
    return pl.pallas_call(
        _wattn_kernel,
        grid=(b, s // tq),
        in_specs=[pl.BlockSpec((1, tq, B_WIDTH), lambda i, t: (i, t, 0)),
                  kt_edge(prev), pl.BlockSpec((1, KV_WIDTH, tq), lambda i, t: (i, 0, t)), kt_edge(nxt),
                  v_edge(prev), pl.BlockSpec((1, tq, KV_WIDTH), lambda i, t: (i, t, 0)), v_edge(nxt),
                  pl.BlockSpec((B_KV_HEADS, 4 * Q_BLOCK, span), lambda i, t: (0, 0, 0)),
                  pl.BlockSpec((B_KV_HEADS, 4 * Q_BLOCK, LANES), lambda i, t: (0, 0, 0))],
        out_specs=pl.BlockSpec((1, tq, B_WIDTH), lambda i, t: (i, t, 0)),
        out_shape=jax.ShapeDtypeStruct((b, s, B_WIDTH), BF16),
        compiler_params=_params(("parallel", "parallel")),
        name="window_attention",
    )(qb, kbt, kbt, kbt, vb, vb, vb, bias4, sink4)


def _pack_bf16_pairs(x):
    k = x.shape[1] // 2
    hi = pltpu.bitcast(x[:, :k].astype(BF16).astype(F32), U32)
    lo = pltpu.bitcast(x[:, k:].astype(BF16).astype(F32), U32)
    return hi | (lo >> 16)


def _unpack_bf16_pairs(w):
    hi = pltpu.bitcast(w & jnp.uint32(0xFFFF0000), F32)
    lo = pltpu.bitcast(w << 16, F32)
    return jnp.concatenate([hi, lo], axis=1).astype(BF16)


def _outproj_kernel(oa_ref, ob_ref, sga_ref, sgb_ref, x_ref, gt1_ref, sc2_ref, sh2_ref, g2_ref,
                    wa_ref, wb_ref, wo_ref, wr_ref, x1_ref, h2p_ref, afft_ref):
    tm = x_ref.shape[1]
    halves = [slice(0, tm // 2), slice(tm // 2, tm)]
    merged = []
    for r in halves:
        a = _dot(oa_ref[0, r], wa_ref[...])
        b = _dot(ob_ref[0, r], wb_ref[...])
        merged.append((sga_ref[0, r].astype(F32) * a + sgb_ref[0, r].astype(F32) * b).astype(BF16))
    h2s = []
    for r, mg in zip(halves, merged):
        x1 = x_ref[0, r] + gt1_ref[0] * _dot(mg, wo_ref[...])
        x1_ref[0, r] = x1
        ms = jnp.mean(x1 * x1, axis=-1, keepdims=True)
        h2s.append((x1 * lax.rsqrt(ms + EPS) * g2_ref[...]) * (1.0 + sc2_ref[0]) + sh2_ref[0])
    for r, h2 in zip(halves, h2s):
        h2p_ref[r] = _pack_bf16_pairs(h2).reshape(tm // 2, 1, PACKED)
        logits = _dot(h2.astype(BF16), wr_ref[...])
        lane = lax.broadcasted_iota(I32, logits.shape, 1)
        logits = jnp.where(lane < N_EXPERTS, logits, -jnp.inf)
        m = jnp.max(logits, axis=-1, keepdims=True)
        e = jnp.exp(logits - m)
        aff = e / jnp.sum(e, axis=-1, keepdims=True)
        afft_ref[:, r] = aff.T[:N_EXPERTS, :]


def _out_projection(oa, ob, sga, sgb, x, gt1, sc2, sh2, g2, wa, wb, wo, wr, tm):
    b, s, d = x.shape
    nt = s // tm
    n = b * s
    tok = lambda w: pl.BlockSpec((1, tm, w), lambda i, t: (i, t, 0))
    vec = pl.BlockSpec((1, 1, d), lambda i, t: (i, 0, 0))
    full = lambda r, c: pl.BlockSpec((r, c), lambda i, t: (0, 0))
    sd = jax.ShapeDtypeStruct
    return pl.pallas_call(
        _outproj_kernel,
        grid=(b, nt),
        in_specs=[tok(A_WIDTH), tok(B_WIDTH), tok(d), tok(d), tok(d), vec, vec, vec, full(1, d),
                  full(A_WIDTH, d), full(B_WIDTH, d), full(d, d), full(d, LANES)],
        out_specs=[tok(d),
                   pl.BlockSpec((tm, 1, PACKED), lambda i, t: (i * nt + t, 0, 0)),
                   pl.BlockSpec((N_EXPERTS, tm), lambda i, t: (0, i * nt + t))],
        out_shape=[sd((b, s, d), F32), sd((n, 1, PACKED), U32), sd((N_EXPERTS, n), F32)],
        compiler_params=_params(("parallel", "parallel")),
        name="out_projection",
    )(oa, ob, sga, sgb, x, gt1, sc2, sh2, g2.reshape(1, d), wa, wb, wo, wr)


def _threshold_kernel(aff_ref, thr_ref, cut_ref, *, cap):
    bits = pltpu.bitcast(aff_ref[...], I32)
    n = bits.shape[1]

    def value_step(i, lo):
        cand = lo | jnp.left_shift(jnp.int32(1), 30 - i)
        cnt = jnp.sum((bits >= cand).astype(F32), axis=1, keepdims=True)
        return jnp.where(cnt >= cap, cand, lo)

    thr = lax.fori_loop(0, 31, value_step, jnp.zeros((N_EXPERTS, 1), I32))
    need = cap - jnp.sum((bits > thr).astype(F32), axis=1, keepdims=True)
    eq = bits == thr
    tpos = lax.broadcasted_iota(I32, (1, n), 1)
    nbits = max(n.bit_length(), 1)

    def index_step(i, cut):
        cand = cut + jnp.left_shift(jnp.int32(1), nbits - 1 - i)
        below = jnp.sum(jnp.where(eq & (tpos < cand), 1.0, 0.0), axis=1, keepdims=True)
        return jnp.where((below <= need) & (cand <= n), cand, cut)

    cut = lax.fori_loop(0, nbits, index_step, jnp.zeros((N_EXPERTS, 1), I32))
    thr_ref[...] = jnp.broadcast_to(thr, thr_ref.shape)
    cut_ref[...] = jnp.broadcast_to(cut, cut_ref.shape)


def _capacity_threshold(afft, cap):
    e, n = afft.shape
    out = pl.BlockSpec((e, LANES), lambda i: (0, 0))
    return pl.pallas_call(
        functools.partial(_threshold_kernel, cap=cap),
        grid=(1,),
        in_specs=[pl.BlockSpec((e, n), lambda i: (0, 0))],
        out_specs=[out, out],
        out_shape=[jax.ShapeDtypeStruct((e, LANES), I32)] * 2,
        compiler_params=_params(("arbitrary",)),
        name="capacity_threshold",
    )(afft)


_AUX_ROWS = SUBLANES


def _prefix_kernel(aff_ref, thr_ref, cut_ref, u_ref, ls_ref, posm_ref, slot_ref, tokrow_ref, offs_ref,
                   run_ref, *, tb, ch):
    i = pl.program_id(0)

    @pl.when(i == 0)
    def _():
        run_ref[...] = jnp.zeros_like(run_ref)

    bits = pltpu.bitcast(aff_ref[...], I32)
    tpos = i * tb + lax.broadcasted_iota(I32, (1, tb), 1)
    thr = thr_ref[:, 0:1]
    sel = (bits > thr) | ((bits == thr) & (tpos < cut_ref[:, 0:1]))
    s = jnp.where(sel, 1.0, 0.0)
    for j in range(tb // ch):
        sj = s[:, j * ch:(j + 1) * ch]
        cntj = jnp.sum(sj, axis=0, keepdims=True)
        x = jnp.concatenate([sj, cntj, jnp.zeros((_AUX_ROWS - 1, ch), F32)], axis=0).astype(BF16)
        run = run_ref[:, 0:1]
        incl = _dot(x, u_ref[...]) + run
        posm_ref[:, j * ch:(j + 1) * ch] = jnp.where(sj > 0.0, incl[:N_EXPERTS] - sj, -1.0)
        tok_end = incl[N_EXPERTS:N_EXPERTS + 1]
        tok_off = tok_end - cntj
        slot_ref[:, j * ch:(j + 1) * ch] = tok_off + _dot(ls_ref[...], sj.astype(BF16))
        tokrow_ref[:, j * ch:(j + 1) * ch] = jnp.concatenate(
            [tok_off, tok_end, jnp.zeros((6, ch), F32)], axis=0)
        offs_ref[j] = jnp.broadcast_to(run[:N_EXPERTS], (N_EXPERTS, LANES))
        run_ref[...] = jnp.broadcast_to(incl[:, ch - 1:ch], run_ref.shape)


def _routing_prefix(afft, thr, cut, tb, ch):
    e, n = afft.shape
    upper = jnp.asarray(np.triu(np.ones((ch, ch), np.float32)), BF16)
    lower_strict = jnp.asarray(np.tril(np.ones((e, e), np.float32), -1), BF16)
    rows = lambda r: pl.BlockSpec((r, tb), lambda i: (0, i))
    const = lambda r, c: pl.BlockSpec((r, c), lambda i: (0, 0))
    sd = jax.ShapeDtypeStruct
    return pl.pallas_call(
        functools.partial(_prefix_kernel, tb=tb, ch=ch),
        grid=(n // tb,),
        in_specs=[rows(e), const(e, LANES), const(e, LANES), const(ch, ch), const(e, e)],
        out_specs=[rows(e), rows(e), rows(8), pl.BlockSpec((tb // ch, e, LANES), lambda i: (i, 0, 0))],
        out_shape=[sd((e, n), F32), sd((e, n), F32), sd((8, n), F32), sd((n // ch, e, LANES), F32)],
        scratch_shapes=[pltpu.VMEM((e + _AUX_ROWS, LANES), F32)],
        compiler_params=_params(("arbitrary",)),
        name="routing_prefix",
    )(afft, thr, cut, upper, lower_strict)


def _compact_kernel(win_ref, aff_ref, posm_ref, slot_ref, list_ref, *, tb, ch, nc):
    i = pl.program_id(0)

    @pl.when(i == 0)
    def _():
        list_ref[...] = jnp.zeros_like(list_ref)

    rank = lax.broadcasted_iota(I32, (2 * ch, ch), 0).astype(F32)
    tok_local = lax.broadcasted_iota(I32, (1, ch), 1).astype(F32)

    def expert_body(e, carry):
        for j in range(tb // ch):
            c = i * (tb // ch) + j
            w0 = win_ref[e * nc + c]
            lanes = slice(j * ch, (j + 1) * ch)
            rel = posm_ref[pl.ds(e, 1), lanes] - (w0 * ch).astype(F32)
            onehot_t = jnp.where(rank == rel, 1.0, 0.0).astype(BF16)
            slot = slot_ref[pl.ds(e, 1), lanes].astype(I32)
            g = aff_ref[pl.ds(e, 1), lanes]
            g_hi = g.astype(BF16).astype(F32)
            g_mid = (g - g_hi).astype(BF16).astype(F32)
            g_lo = g - g_hi - g_mid
            vals = jnp.concatenate(
                [tok_local, jnp.full((1, ch), c, I32).astype(F32),
                 (slot & (DIGIT - 1)).astype(F32), ((slot >> DIGIT_BITS) & (DIGIT - 1)).astype(F32),
                 (slot >> (2 * DIGIT_BITS)).astype(F32),
                 g_hi, g_mid, g_lo], axis=0).astype(BF16)
            out = lax.dot_general(vals, onehot_t, (((1,), (1,)), ((), ())), preferred_element_type=F32)
            list_ref[e, w0] += out[:, :ch]
            list_ref[e, w0 + 1] += out[:, ch:]
        return carry

    lax.fori_loop(0, N_EXPERTS, expert_body, 0)


def _routing_compact(win, afft, posm, slot, cap, tb, ch):
    e, n = afft.shape
    nc = n // ch
    nwin = cap // ch
    rows = pl.BlockSpec((e, tb), lambda i, w: (0, i))
    grid_spec = pltpu.PrefetchScalarGridSpec(
        num_scalar_prefetch=1,
        grid=(n // tb,),
        in_specs=[rows, rows, rows],
        out_specs=pl.BlockSpec((e, nwin + 2, 8, ch), lambda i, w: (0, 0, 0, 0)),
    )
    return pl.pallas_call(
        functools.partial(_compact_kernel, tb=tb, ch=ch, nc=nc),
        grid_spec=grid_spec,
        out_shape=jax.ShapeDtypeStruct((e, nwin + 2, 8, ch), F32),
        compiler_params=_params(("arbitrary",)),
        name="routing_compact",
    )(win, afft, posm, slot)


def _routing(afft, cap, tm, tt, blk, tb=2048, ch=256):
    e, n = afft.shape
    assert n // ch <= DIGIT and ch <= DIGIT and cap % ch == 0 and cap % tm == 0 and e * cap < (1 << 24)
    tb = min(tb, n)
    thr, cut = _capacity_threshold(afft, cap)
    posm, slot, tokrow, offs = _routing_prefix(afft, thr, cut, tb, ch)
    win = (offs[:, :, 0].astype(I32) // ch).T.reshape(-1)
    lists = _routing_compact(win, afft, posm, slot, cap, tb, ch)[:, :cap // ch]
    nsteps = e * cap // tm
    idx = (lists[:, :, 1] * ch + lists[:, :, 0]).astype(I32).reshape(nsteps, tm)
    dst = (lists[:, :, 2] + lists[:, :, 3] * float(DIGIT) + lists[:, :, 4] * float(DIGIT * DIGIT)).astype(I32)
    dst = dst.reshape(nsteps, tm)
    gate = ((lists[:, :, 5] + lists[:, :, 6]) + lists[:, :, 7]).reshape(nsteps, 1, tm)
    meta = jnp.concatenate([idx, dst], axis=1)
    gate8 = jnp.broadcast_to(gate, (nsteps, SUBLANES, tm))

    npairs = e * cap
    ntile, nblk = n // tt, npairs // blk
    start = tokrow[0, ::tt].astype(I32)
    end = jnp.concatenate([start[1:], jnp.full((1,), npairs, I32)])
    b0 = jnp.minimum(start // blk, nblk - 1)
    b1 = jnp.where(end > start, (end - 1) // blk, b0)
    nb = b1 - b0 + 1
    wend = jnp.cumsum(nb)
    woff = wend - nb
    w = jnp.arange(nblk + ntile, dtype=I32)
    wt = jnp.minimum(jnp.sum((w[:, None] >= wend[None, :]).astype(I32), axis=1), ntile - 1)
    valid = (w < wend[-1]).astype(I32)
    wblk = jnp.where(valid == 1, b0[wt] + (w - woff[wt]), b1[ntile - 1]).astype(I32)
    return meta, gate8, tokrow, wt, wblk, valid


def _moe_kernel(meta_hbm, gate_ref, h2_hbm, wg_ref, wu_ref, wd_ref, z_hbm,
                meta_smem, xbuf, x2d, zbuf, sem_m, sem_g, sem_s, *, tm, nsteps):
    nt = pl.num_programs(1)
    s = pl.program_id(0) * nt + pl.program_id(1)
    slot = s % 2

    def meta_copy(step, mslot):
        return pltpu.make_async_copy(meta_hbm.at[step], meta_smem.at[mslot], sem_m)

    def issue_gather(mslot, bslot):
        for r in range(tm):
            tok = meta_smem[mslot, r]
            pltpu.make_async_copy(h2_hbm.at[tok], xbuf.at[bslot, r], sem_g.at[bslot]).start(priority=r % 2)

    def wait_gather(bslot):
        pltpu.make_async_copy(h2_hbm.at[pl.ds(0, tm)], xbuf.at[bslot], sem_g.at[bslot]).wait()

    def issue_scatter(mslot, bslot):
        for r in range(tm):
            dst = meta_smem[mslot, tm + r]
            pltpu.make_async_copy(zbuf.at[bslot, r], z_hbm.at[dst], sem_s.at[bslot]).start(priority=r % 2)

    def wait_scatter(bslot):
        pltpu.make_async_copy(zbuf.at[bslot], z_hbm.at[pl.ds(0, tm)], sem_s.at[bslot]).wait()

    last = nsteps - 1

    @pl.when(s == 0)
    def _():
        for step, mslot in ((0, 0), (min(1, last), 1)):
            c = meta_copy(step, mslot)
            c.start()
            c.wait()
        for r in range(tm):
            meta_smem[3, tm + r] = nsteps * tm + r
        zbuf[1] = jnp.zeros(zbuf.shape[1:], zbuf.dtype)
        issue_gather(0, 0)

    wait_gather(slot)

    @pl.when(s >= 1)
    def _():
        wait_scatter(slot)

    x2d[...] = xbuf[slot].reshape(tm, PACKED)
    prefetch = meta_copy(jnp.minimum(s + 2, last), (s + 2) % 4)
    prefetch.start()
    issue_gather((s + 1) % 4, 1 - slot)
    issue_scatter((s + 3) % 4, 1 - slot)
    xe = _unpack_bf16_pairs(x2d[...])
    gate = _dot(xe, wg_ref[...])
    up = _dot(xe, wu_ref[...])
    hid = (gate * jax.nn.sigmoid(gate) * up).astype(BF16)
    gcol = gate_ref[...].T[:, 0:1]
    halves = [slice(0, tm // 2), slice(tm // 2, tm)]
    ye = [_dot(hid[h], wd_ref[...]) for h in halves]
    for h, y in zip(halves, ye):
        zbuf[slot, h] = _pack_bf16_pairs(y * gcol[h]).reshape(tm // 2, 1, PACKED)
    prefetch.wait()

    @pl.when(s == last)
    def _():
        issue_scatter(s % 4, slot)
        wait_scatter(slot)
        wait_scatter(1 - slot)
        wait_gather(1 - slot)


def _expert_mlp(meta, gate8, h2p, wg, wu, wd, tm):
    nsteps = meta.shape[0]
    nt = nsteps // N_EXPERTS
    d, f = wg.shape[1], wg.shape[2]
    step = lambda e, i: (e * nt + i, 0, 0)
    return pl.pallas_call(
        functools.partial(_moe_kernel, tm=tm, nsteps=nsteps),
        grid=(N_EXPERTS, nt),
        in_specs=[pl.BlockSpec(memory_space=pl.ANY),
                  pl.BlockSpec((None, SUBLANES, tm), step),
                  pl.BlockSpec(memory_space=pl.ANY),
                  pl.BlockSpec((None, d, f), lambda e, i: (e, 0, 0)),
                  pl.BlockSpec((None, d, f), lambda e, i: (e, 0, 0)),
                  pl.BlockSpec((None, f, d), lambda e, i: (e, 0, 0))],
        out_specs=pl.BlockSpec(memory_space=pl.ANY),
        out_shape=jax.ShapeDtypeStruct(((nsteps + 1) * tm, 1, PACKED), U32),
        scratch_shapes=[pltpu.SMEM((4, 2 * tm), I32),
                        pltpu.VMEM((2, tm, 1, PACKED), U32),
                        pltpu.VMEM((tm, PACKED), U32),
                        pltpu.VMEM((2, tm, 1, PACKED), U32),
                        pltpu.SemaphoreType.DMA,
                        pltpu.SemaphoreType.DMA((2,)),
                        pltpu.SemaphoreType.DMA((2,))],
        compiler_params=_params(("arbitrary", "arbitrary")),
        name="expert_mlp",
    )(meta, gate8, h2p, wg, wu, wd)


def _combine_kernel(wtile_ref, wblk_ref, wvalid_ref, z_ref, tokrow_ref, x1_ref, gt2_ref, gf_ref,
                    y_ref, acc_ref, z2d, *, tt):
    w = pl.program_id(0)
    nw = pl.num_programs(0)
    tile = wtile_ref[w]
    first = jnp.logical_or(w == 0, wtile_ref[jnp.maximum(w - 1, 0)] != tile)
    last = jnp.logical_or(w == nw - 1, wtile_ref[jnp.minimum(w + 1, nw - 1)] != tile)

    @pl.when(first)
    def _():
        acc_ref[...] = jnp.zeros_like(acc_ref)

    @pl.when(wvalid_ref[w] == 1)
    def _():
        blk = z2d.shape[0]
        z2d[...] = z_ref[...].reshape(z2d.shape)
        zrows = _unpack_bf16_pairs(z2d[...])
        pair = (wblk_ref[w] * blk + lax.broadcasted_iota(I32, (blk, tt), 0)).astype(F32)
        owned = (pair >= tokrow_ref[0:1, :]) & (pair < tokrow_ref[1:2, :])
        onehot = jnp.where(owned, 1.0, 0.0).T.astype(BF16)
        acc_ref[...] += _dot(onehot, zrows)

    @pl.when(last)
    def _():
        x2 = x1_ref[...] + gt2_ref[0] * acc_ref[...]
        ms = jnp.mean(x2 * x2, axis=-1, keepdims=True)
        y_ref[...] = x2 * lax.rsqrt(ms + EPS) * gf_ref[...]


def _combine(wtile, wblk, wvalid, z, tokrow, x1, gt2, gf, s, tt, blk):
    n, d = x1.shape
    nw = wtile.shape[0]
    grid_spec = pltpu.PrefetchScalarGridSpec(
        num_scalar_prefetch=3,
        grid=(nw,),
        in_specs=[pl.BlockSpec((blk, 1, PACKED), lambda w, wt, wb, wv: (wb[w], 0, 0)),
                  pl.BlockSpec((8, tt), lambda w, wt, wb, wv: (0, wt[w])),
                  pl.BlockSpec((tt, d), lambda w, wt, wb, wv: (wt[w], 0)),
                  pl.BlockSpec((1, 1, d), lambda w, wt, wb, wv: (wt[w] * tt // s, 0, 0)),
                  pl.BlockSpec((1, d), lambda w, wt, wb, wv: (0, 0))],
        out_specs=pl.BlockSpec((tt, d), lambda w, wt, wb, wv: (wt[w], 0)),
        scratch_shapes=[pltpu.VMEM((tt, d), F32), pltpu.VMEM((blk, PACKED), U32)],
    )
    return pl.pallas_call(
        functools.partial(_combine_kernel, tt=tt),
        grid_spec=grid_spec,
        out_shape=jax.ShapeDtypeStruct((n, d), F32),
        compiler_params=_params(("arbitrary",)),
        name="combine_final_norm",
    )(wtile, wblk, wvalid, z, tokrow, x1, gt2, gf.reshape(1, d))


def _trunk(x, mod, w, tm_proj=512, tq=512, tu=256, tq_win=1024, tm_moe=512, route_ch=256, tt=512, blk=512):
    b, s, d = x.shape
    n = b * s
    cap = CAPACITY_FACTOR * n // N_EXPERTS
    sh1, sc1, gt1, sh2, sc2, gt2 = [m.reshape(b, 1, d) for m in jnp.split(mod, N_MOD, axis=-1)]

    qa, kat, va, qb, kbt, vb, sga, sgb = _in_projection(
        x, sc1, sh1, w["norm1_g"], w["w_in"], w["q_norm_g"], w["k_norm_g"], tm_proj)
    oa = _global_attention(qa, kat, va, tq, tu)
    ob = _window_attention(qb, kbt, vb, w["rel_bias"], w["sink"], tq_win)
    x1, h2p, afft = _out_projection(oa, ob, sga, sgb, x, gt1, sc2, sh2, w["norm2_g"],
                                    w["w_branch_a"], w["w_branch_b"], w["w_out"], w["w_router"], tm_proj)
    meta, gate8, tokrow, wtile, wblk, wvalid = _routing(afft, cap, tm_moe, tt, blk, ch=route_ch)
    z = _expert_mlp(meta, gate8, h2p, w["w_e_gate"], w["w_e_up"], w["w_e_down"], tm_moe)
    y = _combine(wtile, wblk, wvalid, z, tokrow, x1.reshape(n, d), gt2, w["norm_f_g"], s, tt, blk)
    return y.reshape(b, s, d)


def kernel(x_prompt, x_sample, c_prompt, c_sample, w_ada, b_ada, norm1_g, w_in, q_norm_g, k_norm_g, sink,
           w_branch_a, w_branch_b, w_out, norm2_g, w_router, w_e_gate, w_e_up, w_e_down, rel_bias, norm_f_g):
    assert w_ada.shape[0] == 1, "single layer"
    bp, bs = c_prompt.shape[0], c_sample.shape[0]
    rows = -(-(bp + bs) // 16) * 16
    c = jnp.concatenate([c_prompt, c_sample, jnp.zeros((rows - bp - bs, D_MODEL), F32)], axis=0)
    mod = _modulation(c, w_ada[0], b_ada[0])

    router = jnp.zeros((D_MODEL, LANES), BF16).at[:, :N_EXPERTS].set(w_router[0].astype(BF16))
    w = dict(norm1_g=norm1_g[0], w_in=w_in[0].astype(BF16), q_norm_g=q_norm_g[0], k_norm_g=k_norm_g[0],
             sink=sink[0], w_branch_a=w_branch_a[0].astype(BF16), w_branch_b=w_branch_b[0].astype(BF16),
             w_out=w_out[0].astype(BF16), norm2_g=norm2_g[0], w_router=router,
             w_e_gate=w_e_gate[0].astype(BF16), w_e_up=w_e_up[0].astype(BF16),
             w_e_down=w_e_down[0].astype(BF16), rel_bias=rel_bias, norm_f_g=norm_f_g)
    y_prompt = _trunk(x_prompt, mod[:bp], w)
    y_sample = _trunk(x_sample, mod[bp:bp + bs], w)
    return (y_prompt, y_sample)
```
